```python
import jax, jax.numpy as jnp
from jax import lax
import numpy as np

D_MODEL = 1024
BATCH = 2
SEQ = 8192
DEPTH = 2
DEC_BATCH = 32
DEC_SEQ = 4
PAST_LEN = 16384
PAGE_SIZE = 128

HEAD_DIM = 64
A_GROUPS = 4
A_WIDTH = A_GROUPS * HEAD_DIM
CHUNK = 128
B_HEADS = 4
B_WIDTH = B_HEADS * HEAD_DIM
LORA_W = 64
LORA_A = 64
LORA_G = 64
B_PROJ = 3 * B_WIDTH + LORA_W + LORA_A + LORA_G
RWKV_GN_EPS = 64e-5
C_HEADS = 8
C_KV_HEADS = 2
C_GROUP = C_HEADS // C_KV_HEADS
C_WIDTH = C_HEADS * HEAD_DIM
C_KV_COLS = 3 * 2 * C_KV_HEADS * HEAD_DIM
C_GATE_COLS = 3 * C_HEADS
CMP_STRIDE = 16
CMP_BLOCK = 2 * CMP_STRIDE
CMP_HID = HEAD_DIM
SLC_BLOCK = 64
SLC_TOPK = 16
WINDOW = 512
Q_BLOCK = 128
MIX_WIDTH = A_WIDTH + B_WIDTH + C_WIDTH
IN_WIDTH = 2 * A_WIDTH + B_PROJ + C_WIDTH + C_KV_COLS + C_GATE_COLS
PEER_HEADS = 8
N_KEYS = 128
N_EXPERTS = N_KEYS * N_KEYS
PEER_TOPK = 16
PEER_QDIM = 256
PEER_HALF = PEER_QDIM // 2
PEER_BLOCK = 128
NEG_BIG = -1e30
POS_BIG = 1e30

kernel_name = 'hymba_gmlp_rwkv7_nsa_peer_step'


def rmsnorm(x, g, eps=1e-6):
    xf = x.astype(jnp.float32)
    y = xf * lax.rsqrt(jnp.mean(xf * xf, axis=-1, keepdims=True) + eps)
    return (y * g.astype(jnp.float32)).astype(x.dtype)


def masked_softmax(s, valid):
    p = jax.nn.softmax(jnp.where(valid, s, NEG_BIG), axis=-1)
    return jnp.where(valid, p, 0.0)


def alibi_slopes():
    return 2.0 ** (-8.0 * (jnp.arange(C_HEADS, dtype=jnp.float32) + 1.0) / C_HEADS)


def split_proj(z):
    cuts = np.cumsum([A_WIDTH, A_WIDTH, B_PROJ, C_WIDTH, C_KV_COLS]).tolist()
    return jnp.split(z, cuts, axis=-1)


def chunk_mlp(a_u, a_v, lw):
    B_, T, _ = a_u.shape
    u = jax.nn.gelu(a_u)
    v = rmsnorm(jax.nn.gelu(a_v), lw['a_vnorm_g'])
    Tp = -(-T // CHUNK) * CHUNK
    vc = jnp.pad(v, ((0, 0), (0, Tp - T), (0, 0))).reshape(B_, Tp // CHUNK, CHUNK, A_GROUPS, HEAD_DIM)
    wm = jnp.where(jnp.tril(jnp.ones((CHUNK, CHUNK), bool)), lw['a_ws'], 0.0)
    mixed = jnp.einsum('gts,bcsgd->bctgd', wm, vc) + lw['a_bs'].T[None, None, :, :, None]
    mixed = mixed.reshape(B_, Tp, A_WIDTH)[:, :T]
    return u * mixed, v


def rwkv7(proj, shift_prev, wkv0, lw):
    B_, T, _ = proj.shape
    f32 = jnp.float32
    prev = jnp.concatenate([shift_prev[:, None, :].astype(proj.dtype), proj[:, :-1]], axis=1)
    xs = proj + (prev - proj) * lw['b_mu']
    cuts = np.cumsum([B_WIDTH, B_WIDTH, B_WIDTH, LORA_W, LORA_A]).tolist()
    r, k, v, wl, al, gl = jnp.split(xs, cuts, axis=-1)
    heads = lambda a: a.reshape(B_, T, B_HEADS, HEAD_DIM).astype(f32)
    hv = lambda p: p.reshape(B_HEADS, HEAD_DIM).astype(f32)
    w = -jax.nn.softplus(-(lw['b_w0'] + jnp.tanh(wl) @ lw['b_w2']).astype(f32)) - 0.5
    decay = heads(jnp.exp(-jnp.exp(w)))
    a = heads(jax.nn.sigmoid((lw['b_a0'] + al @ lw['b_a2']).astype(f32)))
    g = jax.nn.sigmoid(gl) @ lw['b_g2']
    r, k, v = heads(r), heads(k), heads(v)
    kk = k * hv(lw['b_kk'])
    kk = kk / jnp.maximum(jnp.sqrt(jnp.sum(kk * kk, axis=-1, keepdims=True)), 1e-12)
    k = k * (1.0 + (a - 1.0) * hv(lw['b_ka']))

    def step(S, inp):
        r_t, w_t, k_t, v_t, kk_t, a_t = inp
        sa = jnp.einsum('bhvk,bhk->bhv', S, -kk_t)
        S = (S * w_t[:, :, None, :] + sa[..., None] * (kk_t * a_t)[:, :, None, :]
             + v_t[..., None] * k_t[:, :, None, :])
        return S, jnp.einsum('bhvk,bhk->bhv', S, r_t)

    seq = tuple(jnp.moveaxis(z, 1, 0) for z in (r, decay, k, v, kk, a))
    S, y = lax.scan(step, wkv0.astype(f32), seq)
    y = jnp.moveaxis(y, 0, 1)
    mu = jnp.mean(y, axis=-1, keepdims=True)
    var = jnp.mean(jnp.square(y - mu), axis=-1, keepdims=True)
    y = (y - mu) * lax.rsqrt(var + RWKV_GN_EPS) * hv(lw['b_lnx_g']) + hv(lw['b_lnx_b'])
    y = y + jnp.sum(r * k * hv(lw['b_rk']), axis=-1, keepdims=True) * v
    y = y.reshape(B_, T, B_WIDTH).astype(proj.dtype) * g
    return y, S.astype(wkv0.dtype), proj[:, -1]


def nsa_project(c_q, c_kv, c_gate, lw):
    B_, T = c_q.shape[:2]
    q = rmsnorm(c_q.reshape(B_, T, C_HEADS, HEAD_DIM), lw['c_qnorm_g'])
    kv = c_kv.reshape(B_, T, 3, 2, C_KV_HEADS, HEAD_DIM)
    kg = lw['c_knorm_g']
    cmp_kv = kv[:, :, 0]
    slc_kv = jnp.stack([rmsnorm(kv[:, :, 1, 0], kg[1]), kv[:, :, 1, 1]], axis=2)
    win_kv = jnp.stack([rmsnorm(kv[:, :, 2, 0], kg[2]), kv[:, :, 2, 1]], axis=2)
    gates = jax.nn.sigmoid(c_gate.reshape(B_, T, C_HEADS, 3))
    return q, gates, cmp_kv, slc_kv, win_kv


def compress(rows, w1, b1, w2):
    B_, L = rows.shape[:2]
    n_piece = L // CMP_STRIDE
    pieces = rows[:, :n_piece * CMP_STRIDE].reshape(B_, n_piece, CMP_STRIDE, C_KV_HEADS, HEAD_DIM)
    pieces = pieces.transpose(0, 1, 3, 2, 4).reshape(B_, n_piece, C_KV_HEADS, CMP_STRIDE * HEAD_DIM)
    w1h = w1.reshape(CMP_BLOCK // CMP_STRIDE, CMP_STRIDE * HEAD_DIM, CMP_HID)
    hid = jax.nn.gelu(pieces[:, :-1] @ w1h[0] + pieces[:, 1:] @ w1h[1] + b1)
    return hid @ w2


def compressed_kv(cmp_rows, lw):
    w1, b1, w2 = lw['c_cmp_w1'], lw['c_cmp_b1'], lw['c_cmp_w2']
    kc = compress(cmp_rows[:, :, 0], w1[0], b1[0], w2[0])
    vc = compress(cmp_rows[:, :, 1], w1[1], b1[1], w2[1])
    return rmsnorm(kc, lw['c_knorm_g'][0]), vc


def to_blocks(kv):
    B_, L = kv.shape[:2]
    n = -(-L // SLC_BLOCK)
    kv = jnp.pad(kv, ((0, 0), (0, n * SLC_BLOCK - L), (0, 0), (0, 0), (0, 0)))
    kv = kv.reshape(B_, n, SLC_BLOCK, 2, C_KV_HEADS, HEAD_DIM).transpose(3, 0, 4, 1, 2, 5)
    return kv[0], kv[1]


def nsa_attend(q, gates, t, kc, vc, ks_b, vs_b, kw, vw, pos_w, slopes):
    f32 = jnp.float32
    B_, Qb = q.shape[:2]
    scale = HEAD_DIM ** -0.5
    qg = q.reshape(B_, Qb, C_KV_HEADS, C_GROUP, HEAD_DIM)
    sl = slopes.reshape(C_KV_HEADS, C_GROUP, 1, 1)
    n_cmp = kc.shape[1]
    c_pos = jnp.arange(n_cmp) * CMP_STRIDE + (CMP_BLOCK - 1)
    c_dist = t[:, None] - c_pos[None, :]
    s_c = jnp.einsum('bqhgd,bnhd->bhgqn', qg, kc).astype(f32) * scale - sl * c_dist.astype(f32)
    p_c = masked_softmax(s_c, c_dist >= 0)
    o_c = jnp.einsum('bhgqn,bnhd->bqhgd', p_c.astype(vc.dtype), vc)
    n_slc = ks_b.shape[2]
    ratio = SLC_BLOCK // CMP_STRIDE
    imp = jnp.sum(p_c, axis=2)
    imp = jnp.pad(imp, ((0, 0), (0, 0), (0, 0), (0, n_slc * ratio - n_cmp)))
    imp = imp.reshape(B_, C_KV_HEADS, Qb, n_slc, ratio).sum(-1)
    b_idx = jnp.arange(n_slc)[None, :]
    imp = jnp.where(b_idx * SLC_BLOCK <= t[:, None], imp, NEG_BIG)
    imp = jnp.where((b_idx == 0) | (b_idx == (t // SLC_BLOCK)[:, None]), POS_BIG, imp)
    _, sel = lax.top_k(imp, min(SLC_TOPK, n_slc))
    bi = jnp.arange(B_)[:, None, None, None]
    hi = jnp.arange(C_KV_HEADS)[None, :, None, None]
    k_sel = ks_b[bi, hi, sel]
    v_sel = vs_b[bi, hi, sel]
    s_pos = sel[..., None] * SLC_BLOCK + jnp.arange(SLC_BLOCK)
    s_dist = (t[None, None, :, None, None] - s_pos)[:, :, None]
    sl6 = slopes.reshape(1, C_KV_HEADS, C_GROUP, 1, 1, 1)
    s_s = jnp.einsum('bqhgd,bhqkjd->bhgqkj', qg, k_sel).astype(f32) * scale - sl6 * s_dist.astype(f32)
    flat = s_s.shape[:4] + (-1,)
    p_s = masked_softmax(s_s.reshape(flat), (s_dist >= 0).reshape(B_, C_KV_HEADS, 1, Qb, -1)).reshape(s_s.shape)
    o_s = jnp.einsum('bhgqkj,bhqkjd->bqhgd', p_s.astype(v_sel.dtype), v_sel)
    w_dist = t[:, None] - pos_w[None, :]
    w_valid = (w_dist >= 0) & (w_dist < WINDOW) & (pos_w[None, :] >= 0)
    s_w = jnp.einsum('bqhgd,blhd->bhgql', qg, kw).astype(f32) * scale - sl * w_dist.astype(f32)
    p_w = masked_softmax(s_w, w_valid)
    o_w = jnp.einsum('bhgql,blhd->bqhgd', p_w.astype(vw.dtype), vw)
    gg = gates.reshape(B_, Qb, C_KV_HEADS, C_GROUP, 3)
    o = gg[..., 0:1] * o_c + gg[..., 1:2] * o_s + gg[..., 2:3] * o_w
    return o.reshape(B_, Qb, C_WIDTH)


def nsa_prompt(q, gates, cmp_kv, slc_kv, win_kv, lw, slopes):
    B_, T = q.shape[:2]
    kc, vc = compressed_kv(cmp_kv, lw)
    ks_b, vs_b = to_blocks(slc_kv)
    win_pad = jnp.pad(win_kv, ((0, 0), (WINDOW, 0), (0, 0), (0, 0), (0, 0)))

    def one(i):
        s0 = i * Q_BLOCK
        qb = lax.dynamic_slice_in_dim(q, s0, Q_BLOCK, 1)
        gb = lax.dynamic_slice_in_dim(gates, s0, Q_BLOCK, 1)
        wb = lax.dynamic_slice_in_dim(win_pad, s0, WINDOW + Q_BLOCK, 1)
        t = s0 + jnp.arange(Q_BLOCK)
        pos_w = s0 - WINDOW + jnp.arange(WINDOW + Q_BLOCK)
        return nsa_attend(qb, gb, t, kc, vc, ks_b, vs_b, wb[:, :, 0], wb[:, :, 1], pos_w, slopes)

    out = lax.map(one, jnp.arange(T // Q_BLOCK))
    return out.transpose(1, 0, 2, 3).reshape(B_, T, C_WIDTH)


def nsa_sample(q, gates, cmp_kv, slc_kv, win_kv, past_cmp, past_slc, win_buf, past_len, lw, slopes):
    Tn = q.shape[1]
    full_cmp = jnp.concatenate([past_cmp.astype(cmp_kv.dtype), cmp_kv], axis=1)
    full_slc = jnp.concatenate([past_slc.astype(slc_kv.dtype), slc_kv], axis=1)
    kc, vc = compressed_kv(full_cmp, lw)
    ks_b, vs_b = to_blocks(full_slc)
    n_buf = win_buf.shape[1]
    wkv = jnp.concatenate([win_buf.astype(win_kv.dtype), win_kv], axis=1)
    pos_w = past_len - n_buf + jnp.arange(n_buf + Tn)
    t = past_len + jnp.arange(Tn)
    y = nsa_attend(q, gates, t, kc, vc, ks_b, vs_b, wkv[:, :, 0], wkv[:, :, 1], pos_w, slopes)
    return y, wkv[:, wkv.shape[1] - n_buf:]


def peer(x, lw):
    lead = x.shape[:-1]
    xf = x.reshape(-1, D_MODEL)
    n = xf.shape[0]
    n_pad = -(-n // PEER_BLOCK) * PEER_BLOCK
    xb = jnp.pad(xf, ((0, n_pad - n), (0, 0))).reshape(n_pad // PEER_BLOCK, PEER_BLOCK, D_MODEL)
    wq, qg, subkeys, u_tab, v_tab = lw['p_wq'], lw['p_qnorm_g'], lw['p_subkeys'], lw['p_u'], lw['p_v']

    def blk(xi):
        q = rmsnorm((xi @ wq).reshape(PEER_BLOCK, PEER_HEADS, 2, PEER_HALF), qg.reshape(2, PEER_HALF))
        s = jnp.einsum('nhcd,hckd->nhck', q, subkeys).astype(jnp.float32)
        s1, i1 = lax.top_k(s[:, :, 0], PEER_TOPK)
        s2, i2 = lax.top_k(s[:, :, 1], PEER_TOPK)
        n_cand = PEER_TOPK * PEER_TOPK
        cand = (s1[..., :, None] + s2[..., None, :]).reshape(PEER_BLOCK, PEER_HEADS, n_cand)
        cidx = (i1[..., :, None] * N_KEYS + i2[..., None, :]).reshape(PEER_BLOCK, PEER_HEADS, n_cand)
        top_s, top_pos = lax.top_k(cand, PEER_TOPK)
        eidx = jnp.take_along_axis(cidx, top_pos, axis=-1)
        gate = jax.nn.softmax(top_s, axis=-1)
        act = jax.nn.gelu(jnp.einsum('nd,nhkd->nhk', xi, u_tab[eidx]).astype(jnp.float32)) * gate
        return jnp.einsum('nhk,nhkd->nd', act.astype(xi.dtype), v_tab[eidx])

    out = lax.map(blk, xb).reshape(n_pad, D_MODEL)[:n]
    return out.reshape(*lead, D_MODEL)


def layer_prompt(x, lw, slopes):
    B_, T, _ = x.shape
    h = rmsnorm(x, lw['ln1_g'])
    a_u, a_v, b_proj, c_q, c_kv, c_gate = split_proj(h @ lw['w_in'])
    ya, _ = chunk_mlp(a_u, a_v, lw)
    shift0 = jnp.zeros((B_, B_PROJ), b_proj.dtype)
    wkv0 = jnp.zeros((B_, B_HEADS, HEAD_DIM, HEAD_DIM), jnp.float32)
    yb, wkv, shift = rwkv7(b_proj, shift0, wkv0, lw)
    q, gates, cmp_kv, slc_kv, win_kv = nsa_project(c_q, c_kv, c_gate, lw)
    yc = nsa_prompt(q, gates, cmp_kv, slc_kv, win_kv, lw, slopes)
    x = x + jnp.concatenate([ya, yb, yc], axis=-1) @ lw['w_out']
    x = x + peer(rmsnorm(x, lw['ln2_g']), lw)
    n_win = min(WINDOW, T)
    return x, (cmp_kv, slc_kv, win_kv[:, T - n_win:], wkv, shift)


def layer_sample(x, lw, past_cmp, past_slc, win_buf, wkv_prev, shift_prev, past_len, slopes):
    h = rmsnorm(x, lw['ln1_g'])
    a_u, a_v, b_proj, c_q, c_kv, c_gate = split_proj(h @ lw['w_in'])
    ya, v_rows = chunk_mlp(a_u, a_v, lw)
    yb, wkv, shift = rwkv7(b_proj, shift_prev, wkv_prev, lw)
    q, gates, cmp_kv, slc_kv, win_kv = nsa_project(c_q, c_kv, c_gate, lw)
    yc, win_new = nsa_sample(q, gates, cmp_kv, slc_kv, win_kv, past_cmp, past_slc, win_buf, past_len, lw, slopes)
    x = x + jnp.concatenate([ya, yb, yc], axis=-1) @ lw['w_out']
    x = x + peer(rmsnorm(x, lw['ln2_g']), lw)
    return x, (cmp_kv, slc_kv, win_new, wkv, shift, v_rows)


def setup_inputs(seed: int = 0) -> dict:
    key = jax.random.key(seed)
    keys = iter(jax.random.split(key, 64))
    f32 = jnp.float32

    def nrm(shape, scale):
        return jax.random.normal(next(keys), shape, f32) * scale

    def gain(shape, s=0.02):
        return 1.0 + s * jax.random.normal(next(keys), shape, f32)

    n_pages = PAST_LEN // PAGE_SIZE
    n_used = DEC_BATCH * n_pages
    n_pool = (n_used * 5) // 4
    win_buf = min(WINDOW, PAST_LEN)
    x_prompt = nrm((BATCH, SEQ, D_MODEL), 1.0)
    x_sample = nrm((DEC_BATCH, DEC_SEQ, D_MODEL), 1.0)
    cache_cmp_kv = nrm((DEPTH, n_pool, PAGE_SIZE, 2, C_KV_HEADS, HEAD_DIM), 1.0)
    cache_slc_kv = nrm((DEPTH, n_pool, PAGE_SIZE, 2, C_KV_HEADS, HEAD_DIM), 1.0)
    cache_win_kv = nrm((DEPTH, DEC_BATCH, win_buf, 2, C_KV_HEADS, HEAD_DIM), 1.0)
    state_wkv = nrm((DEPTH, DEC_BATCH, B_HEADS, HEAD_DIM, HEAD_DIM), 0.1)
    state_shift = nrm((DEPTH, DEC_BATCH, B_PROJ), 0.5)
    perm = jax.random.permutation(next(keys), n_pool)
    page_table = perm[:n_used].reshape(DEC_BATCH, n_pages).astype(jnp.int32)
    return {
        'x_prompt': x_prompt,
        'x_sample': x_sample,
        'cache_cmp_kv': cache_cmp_kv,
        'cache_slc_kv': cache_slc_kv,
        'cache_win_kv': cache_win_kv,
        'state_wkv': state_wkv,
        'state_shift': state_shift,
        'page_table': page_table,
        'ln1_g': gain((DEPTH, D_MODEL)),
        'ln2_g': gain((DEPTH, D_MODEL)),
        'w_in': nrm((DEPTH, D_MODEL, IN_WIDTH), D_MODEL ** -0.5),
        'w_out': nrm((DEPTH, MIX_WIDTH, D_MODEL), MIX_WIDTH ** -0.5),
        'a_vnorm_g': gain((DEPTH, A_WIDTH)),
        'a_ws': nrm((DEPTH, A_GROUPS, CHUNK, CHUNK), CHUNK ** -0.5),
        'a_bs': gain((DEPTH, A_GROUPS, CHUNK), 0.1),
        'b_mu': jax.random.uniform(next(keys), (DEPTH, B_PROJ), f32, 0.0, 1.0),
        'b_w0': jax.random.uniform(next(keys), (DEPTH, B_WIDTH), f32, -6.0, -0.5),
        'b_w2': nrm((DEPTH, LORA_W, B_WIDTH), 0.1),
        'b_a0': nrm((DEPTH, B_WIDTH), 0.1),
        'b_a2': nrm((DEPTH, LORA_A, B_WIDTH), 0.1),
        'b_g2': nrm((DEPTH, LORA_G, B_WIDTH), LORA_G ** -0.5),
        'b_kk': gain((DEPTH, B_WIDTH), 0.1),
        'b_ka': gain((DEPTH, B_WIDTH), 0.1),
        'b_rk': nrm((DEPTH, B_HEADS, HEAD_DIM), 0.1),
        'b_lnx_g': gain((DEPTH, B_WIDTH)),
        'b_lnx_b': nrm((DEPTH, B_WIDTH), 0.01),
        'c_qnorm_g': gain((DEPTH, HEAD_DIM)),
        'c_knorm_g': gain((DEPTH, 3, HEAD_DIM)),
        'c_cmp_w1': nrm((DEPTH, 2, CMP_BLOCK * HEAD_DIM, CMP_HID), (CMP_BLOCK * HEAD_DIM) ** -0.5),
        'c_cmp_b1': nrm((DEPTH, 2, CMP_HID), 0.01),
        'c_cmp_w2': nrm((DEPTH, 2, CMP_HID, HEAD_DIM), CMP_HID ** -0.5),
        'p_wq': nrm((DEPTH, D_MODEL, PEER_HEADS * PEER_QDIM), D_MODEL ** -0.5),
        'p_qnorm_g': gain((DEPTH, PEER_QDIM)),
        'p_subkeys': nrm((DEPTH, PEER_HEADS, 2, N_KEYS, PEER_HALF), PEER_HALF ** -0.5),
        'p_u': nrm((DEPTH, N_EXPERTS, D_MODEL), D_MODEL ** -0.5),
        'p_v': nrm((DEPTH, N_EXPERTS, D_MODEL), PEER_HEADS ** -0.5),
    }


def reference(x_prompt, x_sample, cache_cmp_kv, cache_slc_kv, cache_win_kv, state_wkv, state_shift, page_table,
              ln1_g, ln2_g, w_in, w_out, a_vnorm_g, a_ws, a_bs, b_mu, b_w0, b_w2, b_a0, b_a2, b_g2, b_kk, b_ka,
              b_rk, b_lnx_g, b_lnx_b, c_qnorm_g, c_knorm_g, c_cmp_w1, c_cmp_b1, c_cmp_w2, p_wq, p_qnorm_g,
              p_subkeys, p_u, p_v):
    slopes = alibi_slopes()
    n_seq, n_pages = page_table.shape
    past_len = n_pages * PAGE_SIZE
    yp, ys = x_prompt, x_sample
    sp, ss = [], []
    for l in range(DEPTH):
        lw = {
            'ln1_g': ln1_g[l], 'ln2_g': ln2_g[l], 'w_in': w_in[l], 'w_out': w_out[l],
            'a_vnorm_g': a_vnorm_g[l], 'a_ws': a_ws[l], 'a_bs': a_bs[l],
            'b_mu': b_mu[l], 'b_w0': b_w0[l], 'b_w2': b_w2[l], 'b_a0': b_a0[l], 'b_a2': b_a2[l],
            'b_g2': b_g2[l], 'b_kk': b_kk[l], 'b_ka': b_ka[l], 'b_rk': b_rk[l],
            'b_lnx_g': b_lnx_g[l], 'b_lnx_b': b_lnx_b[l],
            'c_qnorm_g': c_qnorm_g[l], 'c_knorm_g': c_knorm_g[l], 'c_cmp_w1': c_cmp_w1[l],
            'c_cmp_b1': c_cmp_b1[l], 'c_cmp_w2': c_cmp_w2[l],
            'p_wq': p_wq[l], 'p_qnorm_g': p_qnorm_g[l], 'p_subkeys': p_subkeys[l], 'p_u': p_u[l], 'p_v': p_v[l],
        }
        yp, st_p = layer_prompt(yp, lw, slopes)
        past_cmp = cache_cmp_kv[l][page_table].reshape(n_seq, past_len, 2, C_KV_HEADS, HEAD_DIM)
        past_slc = cache_slc_kv[l][page_table].reshape(n_seq, past_len, 2, C_KV_HEADS, HEAD_DIM)
        ys, st_s = layer_sample(ys, lw, past_cmp, past_slc, cache_win_kv[l], state_wkv[l], state_shift[l],
                                past_len, slopes)
        sp.append(st_p)
        ss.append(st_s)
    stk = lambda lst, i: jnp.stack([s[i] for s in lst], axis=0)
    return (yp, ys, stk(sp, 0), stk(sp, 1), stk(sp, 2), stk(sp, 3), stk(sp, 4),
            stk(ss, 0), stk(ss, 1), stk(ss, 2), stk(ss, 3), stk(ss, 4), stk(ss, 5))
```

```python
import functools
import math

import jax
import jax.numpy as jnp
from jax import lax
from jax.experimental import pallas as pl
from jax.experimental.pallas import tpu as pltpu

F32 = jnp.float32
BF16 = jnp.bfloat16
HIGHEST = lax.Precision.HIGHEST

HEAD_DIM = 64
CHUNK = 128
A_WIDTH = 256
B_WIDTH = 256
B_HEADS = 4
B_PROJ = 960
C_HEADS = 8
C_KV_HEADS = 2
C_GROUP = 4
C_WIDTH = 512
CMP_STRIDE = 16
SLC_BLOCK = 64
SLC_TOPK = 16
WINDOW = 512
Q_BLOCK = 128
PAGE = 128
PEER_HEADS = 8
N_KEYS = 128
PEER_TOPK = 16
RWKV_GN_EPS = 64e-5
NEG_BIG = -1e30
POS_BIG = 1e30
NEG_INF = float("-inf")
VMEM_LIMIT = 56 * 1024 * 1024


def _cparams(sem):
    return pltpu.CompilerParams(dimension_semantics=sem, vmem_limit_bytes=VMEM_LIMIT)


def _dot(a, b):
    return jnp.dot(a.astype(BF16), b.astype(BF16), preferred_element_type=F32)


def _dot_nt(a, b):
    return lax.dot_general(a.astype(BF16), b.astype(BF16), (((1,), (1,)), ((), ())),
                           preferred_element_type=F32)


def _dot2(x, m):
    hi = x.astype(BF16)
    lo = (x - hi.astype(F32)).astype(BF16)
    return (jnp.dot(hi, m, preferred_element_type=F32) + jnp.dot(lo, m, preferred_element_type=F32))


def _hdot(a, b):
    return jnp.dot(a, b, preferred_element_type=F32, precision=HIGHEST)


def _hdot_nt(a, b):
    return lax.dot_general(a, b, (((1,), (1,)), ((), ())), preferred_element_type=F32, precision=HIGHEST)


def _hdot_tn(a, b):
    return lax.dot_general(a, b, (((0,), (0,)), ((), ())), preferred_element_type=F32, precision=HIGHEST)


def _block_ones(width, group):
    i = jnp.arange(width)
    return (i[:, None] // group == i[None, :] // group).astype(BF16)


IN_PAD = 2944


def _inproj_kernel(x_ref, g_ref, w_ref, avg_ref, wm_ref, brow_ref, qg_ref, kg_ref, bd_ref,
                   ya_ref, v_ref, b_ref, q_ref, cmp_ref, slc_ref, win_ref, gate_ref, kvb_ref, *, tm):
    x = x_ref[...]
    h = x * lax.rsqrt(jnp.mean(x * x, axis=-1, keepdims=True) + 1e-6) * g_ref[...]
    z = jnp.dot(h.astype(BF16), w_ref[...], preferred_element_type=F32)
    b_ref[...] = z[:, 512:512 + B_PROJ]
    u = jax.nn.gelu(z[:, 0:256])
    gv = jax.nn.gelu(z[:, 256:512])
    v = gv * lax.rsqrt(jnp.mean(gv * gv, axis=-1, keepdims=True) + 1e-6) * avg_ref[...]
    v_ref[...] = v
    lane_g = lax.broadcasted_iota(jnp.int32, (CHUNK, A_WIDTH), 1) // HEAD_DIM
    for c in range(tm // CHUNK):
        vc = v[c * CHUNK:(c + 1) * CHUNK].astype(BF16)
        mixed = brow_ref[...]
        for g in range(4):
            mg = jnp.dot(wm_ref[g], vc, preferred_element_type=F32)
            mixed = mixed + jnp.where(lane_g == g, mg, 0.0)
        ya_ref[c * CHUNK:(c + 1) * CHUNK, :] = u[c * CHUNK:(c + 1) * CHUNK] * mixed
    bd = bd_ref[...]
    q = z[:, 1536:2048]
    q_ref[...] = q * lax.rsqrt(_dot2(q * q, bd) * (1.0 / HEAD_DIM) + 1e-6) * qg_ref[...]
    bd128 = bd[0:128, 0:128]
    cmp_ref[...] = z[:, 2048:2304]
    for j, (o_ref, off) in enumerate(((slc_ref, 2304), (win_ref, 2560))):
        k = z[:, off:off + 128]
        kn = k * lax.rsqrt(_dot2(k * k, bd128) * (1.0 / HEAD_DIM) + 1e-6) * kg_ref[j:j + 1, :]
        vv = z[:, off + 128:off + 256]
        o_ref[:, 0:128] = kn
        o_ref[:, 128:256] = vv
        kvb_ref[:, j * 256:j * 256 + 128] = kn.astype(BF16)
        kvb_ref[:, j * 256 + 128:j * 256 + 256] = vv.astype(BF16)
    gate_ref[...] = jax.nn.sigmoid(z[:, 2816:2944])


def _inproj(x, p, wm, brow, tm):
    n = x.shape[0]
    full = lambda shape: pl.BlockSpec(shape, lambda i: (0,) * len(shape))
    rows = lambda w: pl.BlockSpec((tm, w), lambda i: (i, 0))
    outs = [(A_WIDTH, F32), (A_WIDTH, F32), (B_PROJ, F32), (C_WIDTH, F32), (256, F32), (256, F32), (256, F32),
            (128, F32), (512, BF16)]
    return pl.pallas_call(
        functools.partial(_inproj_kernel, tm=tm),
        grid=(n // tm,),
        in_specs=[rows(1024), full((1, 1024)), full((1024, IN_PAD)), full((1, A_WIDTH)), full((4, CHUNK, CHUNK)),
                  full((CHUNK, A_WIDTH)), full((1, C_WIDTH)), full((2, 128)), full((512, 512))],
        out_specs=[rows(w) for w, _ in outs],
        out_shape=[jax.ShapeDtypeStruct((n, w), d) for w, d in outs],
        compiler_params=_cparams(("parallel",)),
        name="inproj",
    )(x, p["ln1_g"], p["w_in"], p["a_vnorm_g"], wm, brow, p["c_qnorm_g"], p["c_knorm_g12"], p["bd512"])


def _rwkv_prep_kernel(x_ref, xp_ref, mu_ref, w0_ref, w2a_ref, a0_ref, g2_ref, kkp_ref, ka_ref, rk_ref, bd_ref,
                      tri_ref, p2_ref, y0_ref, p3_ref, z_ref, g_ref, bonus_ref, *, c, t_valid):
    x = x_ref[...]
    xs = x + (xp_ref[...] - x) * mu_ref[...]
    r = xs[:, 0:256]
    k = xs[:, 256:512]
    v = xs[:, 512:768]
    wa = xs[:, 768:896]
    lane = lax.broadcasted_iota(jnp.int32, wa.shape, 1)
    pre = _hdot(jnp.where(lane < 64, jnp.tanh(wa), wa), w2a_ref[...])
    y = -(w0_ref[...] + pre[:, 0:256])
    softplus = jnp.maximum(y, 0.0) + jnp.log1p(jnp.exp(-jnp.abs(y)))
    ew = jnp.exp(-softplus - 0.5)
    a = jax.nn.sigmoid(a0_ref[...] + pre[:, 256:512])
    g_ref[...] = _hdot(jax.nn.sigmoid(xs[:, 896:960]), g2_ref[...])
    bd = bd_ref[...]
    kkr = k * kkp_ref[...]
    kk = kkr / jnp.maximum(jnp.sqrt(_dot2(kkr * kkr, bd)), 1e-12)
    k2 = k * (1.0 + (a - 1.0) * ka_ref[...])
    bonus_ref[...] = _dot2(r * k2 * rk_ref[...], bd) * v
    if t_valid < c:
        live = lax.broadcasted_iota(jnp.int32, ew.shape, 0) < t_valid
        ew = jnp.where(live, ew, 0.0)
        kk = jnp.where(live, kk, 0.0)
        k2 = jnp.where(live, k2, 0.0)
        v = jnp.where(live, v, 0.0)
    cum = _hdot(tri_ref[...], -ew)
    gam = jnp.exp(cum)
    ginv = jnp.exp(-cum)
    g_end = gam[c - 1:c, :]
    alpha_t = -kk * jnp.exp(cum + ew)
    bhat = kk * a * ginv
    khat = k2 * ginv
    rt = r * gam
    kbar = khat * g_end
    bbar = bhat * g_end
    row = lax.broadcasted_iota(jnp.int32, (c, c), 0)
    col = lax.broadcasted_iota(jnp.int32, (c, c), 1)
    eye_c = (row == col).astype(F32)
    r64 = lax.broadcasted_iota(jnp.int32, (HEAD_DIM, HEAD_DIM), 0)
    c64 = lax.broadcasted_iota(jnp.int32, (HEAD_DIM, HEAD_DIM), 1)
    for h in range(B_HEADS):
        sl = slice(h * HEAD_DIM, (h + 1) * HEAD_DIM)
        al, bh, kh, rh, vh = alpha_t[:, sl], bhat[:, sl], khat[:, sl], rt[:, sl], v[:, sl]
        mb = jnp.where(row > col, _hdot_nt(al, bh), 0.0)
        mk = jnp.where(row > col, _hdot_nt(al, kh), 0.0)
        qk = jnp.where(row >= col, _hdot_nt(rh, kh), 0.0)
        qb = jnp.where(row >= col, _hdot_nt(rh, bh), 0.0)
        tinv = eye_c + mb
        pw = mb
        for _ in range(int(math.log2(c)) - 1):
            pw = _hdot(pw, pw)
            tinv = tinv + _hdot(tinv, pw)
        p1 = _hdot(tinv, al)
        u = _hdot(tinv, _hdot(mk, vh))
        p2_ref[:, sl] = rh + _hdot(qb, p1)
        y0_ref[:, sl] = _hdot(qk, vh) + _hdot(qb, u)
        p3_ref[:, sl] = jnp.where(r64 == c64, g_end[:, sl], 0.0) + _hdot_tn(bbar[:, sl], p1)
        z_ref[:, sl] = _hdot_tn(kbar[:, sl], vh) + _hdot_tn(bbar[:, sl], u)


def _rwkv_seq_kernel(st0_ref, p2_ref, y0_ref, p3_ref, z_ref, g_ref, bonus_ref, lg_ref, lb_ref,
                     y_ref, st_ref, st_scr):
    @pl.when(pl.program_id(1) == 0)
    def _():
        st_scr[...] = st0_ref[...]

    for h in range(B_HEADS):
        sl = slice(h * HEAD_DIM, (h + 1) * HEAD_DIM)
        st = st_scr[:, sl]
        y = _hdot(p2_ref[:, sl], st) + y0_ref[:, sl]
        st_scr[:, sl] = _hdot(p3_ref[:, sl], st) + z_ref[:, sl]
        mu = jnp.mean(y, axis=-1, keepdims=True)
        var = jnp.mean(jnp.square(y - mu), axis=-1, keepdims=True)
        yn = (y - mu) * lax.rsqrt(var + RWKV_GN_EPS) * lg_ref[:, sl] + lb_ref[:, sl]
        y_ref[:, sl] = (yn + bonus_ref[:, sl]) * g_ref[:, sl]
    st_ref[...] = st_scr[...]


def _rwkv(proj, prev, st0, p, nb, nch, c, t_valid):
    n = proj.shape[0]
    full = lambda shape: pl.BlockSpec(shape, lambda b, i: (0,) * len(shape))
    rows = lambda w: pl.BlockSpec((c, w), lambda b, i: (b * nch + i, 0))
    mats = pl.BlockSpec((HEAD_DIM, 256), lambda b, i: (b * nch + i, 0))
    tri = (jnp.arange(c)[:, None] >= jnp.arange(c)[None, :]).astype(F32)
    p2, y0, p3, z, g, bonus = pl.pallas_call(
        functools.partial(_rwkv_prep_kernel, c=c, t_valid=t_valid),
        grid=(nb, nch),
        in_specs=[rows(B_PROJ), rows(B_PROJ), full((1, B_PROJ)), full((1, 256)), full((128, 512)), full((1, 256)),
                  full((64, 256)), full((1, 256)), full((1, 256)), full((1, 256)), full((256, 256)), full((c, c))],
        out_specs=[rows(256), rows(256), mats, mats, rows(256), rows(256)],
        out_shape=[jax.ShapeDtypeStruct((n, 256), F32), jax.ShapeDtypeStruct((n, 256), F32),
                   jax.ShapeDtypeStruct((nb * nch * HEAD_DIM, 256), F32),
                   jax.ShapeDtypeStruct((nb * nch * HEAD_DIM, 256), F32),
                   jax.ShapeDtypeStruct((n, 256), F32), jax.ShapeDtypeStruct((n, 256), F32)],
        compiler_params=_cparams(("parallel", "parallel")),
        name="rwkv_prep",
    )(proj, prev, p["b_mu"], p["b_w0"], p["b_w2a"], p["b_a0"], p["b_g2"], p["b_kk"], p["b_ka"], p["b_rk"],
      p["bd256"], tri)
    st_spec = pl.BlockSpec((None, HEAD_DIM, 256), lambda b, i: (b, 0, 0))
    y, st = pl.pallas_call(
        _rwkv_seq_kernel,
        grid=(nb, nch),
        in_specs=[st_spec, rows(256), rows(256), mats, mats, rows(256), rows(256), full((1, 256)), full((1, 256))],
        out_specs=[rows(256), st_spec],
        out_shape=[jax.ShapeDtypeStruct((n, 256), F32), jax.ShapeDtypeStruct((nb, HEAD_DIM, 256), F32)],
        scratch_shapes=[pltpu.VMEM((HEAD_DIM, 256), F32)],
        compiler_params=_cparams(("parallel", "arbitrary")),
        name="rwkv_seq",
    )(st0, p2, y0, p3, z, g, bonus, p["b_lnx_g"], p["b_lnx_b"])
    return y, st


def _compress_tail(a0, a1, n_out, b1_ref, w2_ref, kg_ref, bd_ref, o_ref):
    n = a1.shape[0]
    hid = jax.nn.gelu(a0 + pltpu.roll(a1, n - 1, 0) + b1_ref[...])[0:n_out]
    out = _dot(hid, w2_ref[...])
    k = out[:, 0:128]
    kn = k * lax.rsqrt(_dot2(k * k, bd_ref[...]) * (1.0 / HEAD_DIM) + 1e-6) * kg_ref[...]
    o_ref[:, 0:128] = kn.astype(BF16)
    o_ref[:, 128:256] = out[:, 128:256].astype(BF16)


def _compress_kernel(x_ref, w1_ref, b1_ref, w2_ref, kg_ref, bd_ref, o_ref):
    x = x_ref[...].astype(BF16)
    a0 = jnp.dot(x, w1_ref[0], preferred_element_type=F32)
    a1 = jnp.dot(x, w1_ref[1], preferred_element_type=F32)
    _compress_tail(a0, a1, x.shape[0], b1_ref, w2_ref, kg_ref, bd_ref, o_ref)


def _compress_paged_kernel(pt_ref, *refs, pg):
    pages = refs[:pg + 1]
    w1_ref, b1_ref, w2_ref, kg_ref, bd_ref, o_ref = refs[pg + 1:]
    x = jnp.concatenate([r[...] for r in pages], axis=0).astype(BF16)
    a0 = jnp.dot(x, w1_ref[0], preferred_element_type=F32)
    a1 = jnp.dot(x, w1_ref[1], preferred_element_type=F32)
    _compress_tail(a0, a1, pg * 8, b1_ref, w2_ref, kg_ref, bd_ref, o_ref)


def _compress_weight_specs(index):
    full = lambda shape: pl.BlockSpec(shape, index(len(shape)))
    return [full((2, 4096, 256)), full((1, 256)), full((256, 256)), full((1, 128)), full((128, 128))]


def _compress(cmp_rows, p, nb):
    n_piece = cmp_rows.shape[0] // nb // CMP_STRIDE
    x = cmp_rows.reshape(nb, n_piece, 4096)
    return pl.pallas_call(
        _compress_kernel,
        grid=(nb,),
        in_specs=[pl.BlockSpec((None, n_piece, 4096), lambda b: (b, 0, 0))]
        + _compress_weight_specs(lambda r: (lambda b: (0,) * r)),
        out_specs=pl.BlockSpec((None, n_piece, 256), lambda b: (b, 0, 0)),
        out_shape=jax.ShapeDtypeStruct((nb, n_piece, 256), BF16),
        compiler_params=_cparams(("parallel",)),
        name="compress",
    )(x, p["c_w1e"], p["c_b1e"], p["c_w2e"], p["c_kg0"], p["bd128"])


CMP_PAGES = 16


def _compress_paged(cache, page_table, p):
    ns, n_pages = page_table.shape
    pg = min(CMP_PAGES, n_pages)
    x = cache.reshape(cache.shape[0], PAGE // CMP_STRIDE, 4096)

    def page_spec(j):
        return pl.BlockSpec((None, PAGE // CMP_STRIDE, 4096),
                            lambda s, i, pt: (pt[s, jnp.minimum(i * pg + j, n_pages - 1)], 0, 0))

    return pl.pallas_call(
        functools.partial(_compress_paged_kernel, pg=pg),
        grid_spec=pltpu.PrefetchScalarGridSpec(
            num_scalar_prefetch=1,
            grid=(ns, n_pages // pg),
            in_specs=[page_spec(j) for j in range(pg + 1)]
            + _compress_weight_specs(lambda r: (lambda s, i, pt: (0,) * r)),
            out_specs=pl.BlockSpec((None, pg * 8, 256), lambda s, i, pt: (s, i, 0)),
        ),
        out_shape=jax.ShapeDtypeStruct((ns, n_pages * 8, 256), BF16),
        compiler_params=_cparams(("parallel", "arbitrary")),
        name="compress_paged",
    )(page_table, *([x] * (pg + 1)), p["c_w1e"], p["c_b1e"], p["c_w2e"], p["c_kg0"], p["bd128"])


def _alibi_slope(head):
    return 2.0 ** (-8.0 * (head + 1.0) / C_HEADS)


def _topk_mask(x, k):
    nl = x.shape[-1]
    lane = lax.broadcasted_iota(jnp.int32, x.shape, x.ndim - 1).astype(F32)
    sel = jnp.zeros(x.shape, F32)
    for _ in range(k):
        m = jnp.max(x, axis=-1, keepdims=True)
        idx = jnp.min(jnp.where(x == m, lane, float(nl)), axis=-1, keepdims=True)
        hit = lane == idx
        sel = jnp.where(hit, 1.0, sel)
        x = jnp.where(hit, NEG_INF, x)
    return sel


def _masked_softmax(s, valid):
    s = jnp.where(valid, s, NEG_BIG)
    e = jnp.exp(s - jnp.max(s, axis=-1, keepdims=True))
    return jnp.where(valid, e / jnp.sum(e, axis=-1, keepdims=True), 0.0)


def _rows4(x):
    return jnp.concatenate([x, x, x, x], axis=0)


def _flash_step(carry, s, ok, v):
    m, l, acc = carry
    s = jnp.where(ok, s, NEG_BIG)
    m_new = jnp.maximum(m, jnp.max(s, axis=-1, keepdims=True))
    alpha = jnp.exp(m - m_new)
    pr = jnp.where(ok, jnp.exp(s - m_new), 0.0)
    l = alpha * l + jnp.sum(pr, axis=-1, keepdims=True)
    acc = alpha * acc + jnp.dot(pr.astype(BF16), v, preferred_element_type=F32)
    return m_new, l, acc


def _flash_init(rows):
    return (jnp.full((rows, 1), NEG_BIG, F32), jnp.zeros((rows, 1), F32), jnp.zeros((rows, HEAD_DIM), F32))


def _nsa_prompt_kernel(q_ref, gate_ref, kc_ref, kvb_ref, pool_ref, o_ref, *, n_slc):
    i = pl.program_id(1)
    qb = Q_BLOCK
    n_cmp = kc_ref.shape[0]
    nbl = pool_ref.shape[1]
    tok = lax.broadcasted_iota(jnp.int32, (qb, 1), 0) + i * qb
    t4 = _rows4(tok)
    lane128 = lax.broadcasted_iota(jnp.int32, (1, 128), 1)
    for h in range(C_KV_HEADS):
        heads = [C_GROUP * h + g for g in range(C_GROUP)]
        qh = (jnp.concatenate([q_ref[:, hd * HEAD_DIM:(hd + 1) * HEAD_DIM] for hd in heads], axis=0)
              * (HEAD_DIM ** -0.5)).astype(BF16)
        slope = jnp.concatenate([jnp.full((qb, 1), _alibi_slope(hd), F32) for hd in heads], axis=0)
        kcol = slice(h * HEAD_DIM, (h + 1) * HEAD_DIM)
        vcol = slice(128 + h * HEAD_DIM, 128 + (h + 1) * HEAD_DIM)
        c_dist = t4 - (lax.broadcasted_iota(jnp.int32, (1, n_cmp), 1) * CMP_STRIDE + (2 * CMP_STRIDE - 1))
        s_c = _dot_nt(qh, kc_ref[:, kcol]) - slope * c_dist.astype(F32)
        p_c = _masked_softmax(s_c, c_dist >= 0)
        o_c = jnp.dot(p_c.astype(BF16), kc_ref[:, vcol], preferred_element_type=F32)
        psum = p_c[0:qb] + p_c[qb:2 * qb] + p_c[2 * qb:3 * qb] + p_c[3 * qb:4 * qb]
        imp = _dot2(psum, pool_ref[...])
        blk = lax.broadcasted_iota(jnp.int32, (1, nbl), 1)
        imp = jnp.where(blk * SLC_BLOCK <= tok, imp, NEG_BIG)
        imp = jnp.where((blk == 0) | (blk == tok // SLC_BLOCK), POS_BIG, imp)
        imp = jnp.where(blk < n_slc, imp, NEG_INF)
        sel = _topk_mask(imp, min(SLC_TOPK, n_slc)).astype(BF16)
        blk_row = lax.broadcasted_iota(jnp.int32, (nbl, 128), 0)
        blk_of_lane = lax.broadcasted_iota(jnp.int32, (nbl, 128), 1) // SLC_BLOCK

        def sel_body(c, carry):
            r0 = pl.multiple_of(c * 128, 128)
            dist = t4 - (c * 128 + lane128)
            expand = (blk_row == 2 * c + blk_of_lane).astype(BF16)
            selx = _rows4(jnp.dot(sel, expand, preferred_element_type=F32))
            s = _dot_nt(qh, kvb_ref[pl.ds(r0, 128), kcol]) - slope * dist.astype(F32)
            return _flash_step(carry, s, (dist >= 0) & (selx > 0.5), kvb_ref[pl.ds(r0, 128), vcol])

        _, l_s, acc_s = lax.fori_loop(0, i + 1, sel_body, _flash_init(4 * qb))

        wk = slice(256 + h * HEAD_DIM, 256 + (h + 1) * HEAD_DIM)
        wv = slice(384 + h * HEAD_DIM, 384 + (h + 1) * HEAD_DIM)

        def win_body(c, carry):
            r0 = pl.multiple_of(c * 128, 128)
            dist = t4 - (c * 128 + lane128)
            s = _dot_nt(qh, kvb_ref[pl.ds(r0, 128), wk]) - slope * dist.astype(F32)
            return _flash_step(carry, s, (dist >= 0) & (dist < WINDOW), kvb_ref[pl.ds(r0, 128), wv])

        _, l_w, acc_w = lax.fori_loop(jnp.maximum(i - WINDOW // 128, 0), i + 1, win_body, _flash_init(4 * qb))
        o_s = acc_s / l_s
        o_w = acc_w / l_w
        for g, hd in enumerate(heads):
            rs = slice(g * qb, (g + 1) * qb)
            gt = gate_ref[:, 3 * hd:3 * hd + 3]
            o_ref[hd] = gt[:, 0:1] * o_c[rs] + gt[:, 1:2] * o_s[rs] + gt[:, 2:3] * o_w[rs]


def _nsa_prompt(q, gate, kcvc, kvb, nb, t_len):
    nq = t_len // Q_BLOCK
    n_cmp = kcvc.shape[1]
    n_slc = -(-t_len // SLC_BLOCK)
    nbl = -(-n_slc // 128) * 128
    pool = (jnp.arange(n_cmp)[:, None] // (SLC_BLOCK // CMP_STRIDE) == jnp.arange(nbl)[None, :]).astype(BF16)
    return pl.pallas_call(
        functools.partial(_nsa_prompt_kernel, n_slc=n_slc),
        grid=(nb, nq),
        in_specs=[pl.BlockSpec((Q_BLOCK, C_WIDTH), lambda b, i: (b * nq + i, 0)),
                  pl.BlockSpec((Q_BLOCK, 128), lambda b, i: (b * nq + i, 0)),
                  pl.BlockSpec((None, n_cmp, 256), lambda b, i: (b, 0, 0)),
                  pl.BlockSpec((t_len, 512), lambda b, i: (b, 0)),
                  pl.BlockSpec((n_cmp, nbl), lambda b, i: (0, 0))],
        out_specs=pl.BlockSpec((None, C_HEADS, Q_BLOCK, HEAD_DIM), lambda b, i: (b, 0, i, 0)),
        out_shape=jax.ShapeDtypeStruct((nb, C_HEADS, t_len, HEAD_DIM), F32),
        compiler_params=_cparams(("parallel", "arbitrary")),
        name="nsa_prompt",
    )(q, gate, kcvc, kvb, pool)


SLC_PAGES = 16


def _nsa_sample_kernel(pt_ref, *refs, pg, dec_seq, past_len, n_buf):
    pages = refs[:pg]
    (q_ref, gate_ref, kc_ref, new_ref, buf_ref, pool_ref, o_ref,
     sel_scr, ocw_scr, m_scr, l_scr, acc_scr) = refs[pg:]
    i = pl.program_id(1)
    n_parts = pl.num_programs(1)
    rows = C_GROUP * dec_seq
    n_cmp = kc_ref.shape[0]
    n_blk = pool_ref.shape[1]
    ridx = lax.broadcasted_iota(jnp.int32, (rows, 1), 0)
    tq = ridx % dec_seq
    t_abs = past_len + tq
    lane128 = lax.broadcasted_iota(jnp.int32, (1, 128), 1)

    def slope_of(h):
        s = jnp.full((rows, 1), _alibi_slope(C_GROUP * h), F32)
        for g in range(1, C_GROUP):
            s = jnp.where(ridx // dec_seq == g, _alibi_slope(C_GROUP * h + g), s)
        return s

    @pl.when(i == 0)
    def _():
        same_tok = (lax.broadcasted_iota(jnp.int32, (rows, rows), 0) % dec_seq
                    == lax.broadcasted_iota(jnp.int32, (rows, rows), 1) % dec_seq).astype(BF16)
        for h in range(C_KV_HEADS):
            slope = slope_of(h)
            qh = (q_ref[h] * (HEAD_DIM ** -0.5)).astype(BF16)
            kcol = slice(h * HEAD_DIM, (h + 1) * HEAD_DIM)
            vcol = slice(128 + h * HEAD_DIM, 128 + (h + 1) * HEAD_DIM)
            c_dist = t_abs - (lax.broadcasted_iota(jnp.int32, (1, n_cmp), 1) * CMP_STRIDE + (2 * CMP_STRIDE - 1))
            s_c = _dot_nt(qh, kc_ref[:, kcol]) - slope * c_dist.astype(F32)
            p_c = _masked_softmax(s_c, c_dist >= 0)
            o_c = jnp.dot(p_c.astype(BF16), kc_ref[:, vcol], preferred_element_type=F32)
            hi = p_c.astype(BF16)
            lo = (p_c - hi.astype(F32)).astype(BF16)
            psum = (jnp.dot(same_tok, hi, preferred_element_type=F32)
                    + jnp.dot(same_tok, lo, preferred_element_type=F32))
            imp = _dot2(psum, pool_ref[...])
            blk = lax.broadcasted_iota(jnp.int32, (1, n_blk), 1)
            imp = jnp.where(blk == 0, POS_BIG, imp)
            sel_scr[h] = _topk_mask(imp, min(SLC_TOPK, n_blk + 1) - 1)
            wk = slice(256 + h * HEAD_DIM, 256 + (h + 1) * HEAD_DIM)
            wv = slice(384 + h * HEAD_DIM, 384 + (h + 1) * HEAD_DIM)
            bk = slice(h * HEAD_DIM, (h + 1) * HEAD_DIM)
            bv = slice(128 + h * HEAD_DIM, 128 + (h + 1) * HEAD_DIM)
            d_buf = (n_buf + tq) - lax.broadcasted_iota(jnp.int32, (1, n_buf), 1)
            s_b = _dot_nt(qh, buf_ref[:, bk]) - slope * d_buf.astype(F32)
            carry = _flash_step(_flash_init(rows), s_b, (d_buf >= 0) & (d_buf < WINDOW),
                                buf_ref[:, bv].astype(BF16))
            d_new = tq - lane128
            s_n = _dot_nt(qh, new_ref[:, wk]) - slope * d_new.astype(F32)
            _, l_w, acc_w = _flash_step(carry, s_n, (d_new >= 0) & (lane128 < dec_seq),
                                        new_ref[:, wv].astype(BF16))
            gt = gate_ref[h]
            ocw_scr[h] = gt[:, 0:1] * o_c + gt[:, 2:3] * (acc_w / l_w)
            m0, l0, a0 = _flash_init(rows)
            m_scr[h] = m0
            l_scr[h] = l0
            acc_scr[h] = a0

    nk = pg * PAGE
    key_pos = i * nk + lax.broadcasted_iota(jnp.int32, (1, nk), 1)
    expand = (lax.broadcasted_iota(jnp.int32, (n_blk, nk), 0)
              == i * (nk // SLC_BLOCK) + lax.broadcasted_iota(jnp.int32, (n_blk, nk), 1) // SLC_BLOCK).astype(BF16)
    for h in range(C_KV_HEADS):
        slope = slope_of(h)
        qh = (q_ref[h] * (HEAD_DIM ** -0.5)).astype(BF16)
        kcol = slice(h * HEAD_DIM, (h + 1) * HEAD_DIM)
        vcol = slice(128 + h * HEAD_DIM, 128 + (h + 1) * HEAD_DIM)
        k_all = jnp.concatenate([r[:, kcol] for r in pages], axis=0).astype(BF16)
        v_all = jnp.concatenate([r[:, vcol] for r in pages], axis=0).astype(BF16)
        dist = t_abs - key_pos
        selx = jnp.dot(sel_scr[h].astype(BF16), expand, preferred_element_type=F32)
        s = _dot_nt(qh, k_all) - slope * dist.astype(F32)
        carry = _flash_step((m_scr[h], l_scr[h], acc_scr[h]), s, (dist >= 0) & (selx > 0.5), v_all)
        m_scr[h], l_scr[h], acc_scr[h] = carry

    @pl.when(i == n_parts - 1)
    def _():
        for h in range(C_KV_HEADS):
            slope = slope_of(h)
            qh = (q_ref[h] * (HEAD_DIM ** -0.5)).astype(BF16)
            kcol = slice(h * HEAD_DIM, (h + 1) * HEAD_DIM)
            vcol = slice(128 + h * HEAD_DIM, 128 + (h + 1) * HEAD_DIM)
            d_new = tq - lane128
            s_n = _dot_nt(qh, new_ref[:, kcol]) - slope * d_new.astype(F32)
            _, l_s, acc_s = _flash_step((m_scr[h], l_scr[h], acc_scr[h]), s_n,
                                        (d_new >= 0) & (lane128 < dec_seq), new_ref[:, vcol].astype(BF16))
            o_ref[h] = ocw_scr[h] + gate_ref[h][:, 1:2] * (acc_s / l_s)


def _nsa_sample(page_table, q16, gate16, kcvc, cache_slc, new_rows, win_buf, dec_seq):
    ns, n_pages = page_table.shape
    pg = min(SLC_PAGES, n_pages)
    past_len = n_pages * PAGE
    n_buf = win_buf.shape[1]
    n_cmp = kcvc.shape[1]
    n_blk = past_len // SLC_BLOCK
    rows = C_GROUP * dec_seq
    pool = (jnp.arange(n_cmp)[:, None] // (SLC_BLOCK // CMP_STRIDE) == jnp.arange(n_blk)[None, :]).astype(BF16)
    per_seq = lambda shape: pl.BlockSpec((None,) + shape, lambda s, i, pt: (s,) + (0,) * len(shape))

    def page_spec(j):
        return pl.BlockSpec((None, PAGE, 256), lambda s, i, pt: (pt[s, i * pg + j], 0, 0))

    return pl.pallas_call(
        functools.partial(_nsa_sample_kernel, pg=pg, dec_seq=dec_seq, past_len=past_len, n_buf=n_buf),
        grid_spec=pltpu.PrefetchScalarGridSpec(
            num_scalar_prefetch=1,
            grid=(ns, n_pages // pg),
            in_specs=[page_spec(j) for j in range(pg)]
            + [per_seq((C_KV_HEADS, rows, HEAD_DIM)), per_seq((C_KV_HEADS, rows, 3)), per_seq((n_cmp, 256)),
               per_seq((128, 512)), per_seq((n_buf, 256)),
               pl.BlockSpec((n_cmp, n_blk), lambda s, i, pt: (0, 0))],
            out_specs=per_seq((C_KV_HEADS, rows, HEAD_DIM)),
            scratch_shapes=[pltpu.VMEM((C_KV_HEADS, rows, n_blk), F32), pltpu.VMEM((C_KV_HEADS, rows, HEAD_DIM), F32),
                            pltpu.VMEM((C_KV_HEADS, rows, 1), F32), pltpu.VMEM((C_KV_HEADS, rows, 1), F32),
                            pltpu.VMEM((C_KV_HEADS, rows, HEAD_DIM), F32)],
        ),
        out_shape=jax.ShapeDtypeStruct((ns, C_KV_HEADS, rows, HEAD_DIM), F32),
        compiler_params=_cparams(("parallel", "arbitrary")),
        name="nsa_sample",
    )(page_table, *([cache_slc] * pg), q16, gate16, kcvc, new_rows, win_buf, pool)


def _outproj_kernel(x_ref, ya_ref, yb_ref, yc_ref, w_ref, o_ref):
    acc = x_ref[...] + _dot(ya_ref[...], w_ref[0:256, :]) + _dot(yb_ref[...], w_ref[256:512, :])
    for hd in range(C_HEADS):
        r0 = 512 + hd * HEAD_DIM
        acc = acc + _dot(yc_ref[hd], w_ref[r0:r0 + HEAD_DIM, :])
    o_ref[...] = acc


def _outproj(x, ya, yb, yc, w_out, nb, t_len, tm):
    nt = t_len // tm
    rows = lambda w: pl.BlockSpec((tm, w), lambda b, i: (b * nt + i, 0))
    return pl.pallas_call(
        _outproj_kernel,
        grid=(nb, nt),
        in_specs=[rows(1024), rows(256), rows(256),
                  pl.BlockSpec((None, C_HEADS, tm, HEAD_DIM), lambda b, i: (b, 0, i, 0)),
                  pl.BlockSpec((1024, 1024), lambda b, i: (0, 0))],
        out_specs=rows(1024),
        out_shape=jax.ShapeDtypeStruct(x.shape, F32),
        compiler_params=_cparams(("parallel", "parallel")),
        name="outproj",
    )(x, ya, yb, yc, w_out)


def _top_vals_rows(x, k):
    n, cols = x.shape
    ridx = lax.broadcasted_iota(jnp.int32, x.shape, 0).astype(F32)
    orow = lax.broadcasted_iota(jnp.int32, (k, cols), 0)
    acc = jnp.zeros((k, cols), F32)
    for j in range(k):
        m = jnp.max(x, axis=0, keepdims=True)
        idx = jnp.min(jnp.where(x == m, ridx, float(n)), axis=0, keepdims=True)
        x = jnp.where(ridx == idx, NEG_INF, x)
        acc = jnp.where(orow == j, m, acc)
    return acc


def _peer_route_kernel(x_ref, g_ref, wq_ref, qg_ref, sk_ref, h2_ref, s1_ref, e1_ref, s2_ref, e2_ref, tau_ref):
    x = x_ref[...]
    hb = (x * lax.rsqrt(jnp.mean(x * x, axis=-1, keepdims=True) + 1e-6) * g_ref[...]).astype(BF16)
    h2_ref[...] = hb
    qp = jnp.dot(hb, wq_ref[...], preferred_element_type=F32)
    for hd in range(PEER_HEADS):
        st, top = [], []
        for c in range(2):
            qc = qp[:, (2 * hd + c) * N_KEYS:(2 * hd + c + 1) * N_KEYS]
            qn = qc * lax.rsqrt(jnp.mean(qc * qc, axis=-1, keepdims=True) + 1e-6) * qg_ref[:, c * 128:(c + 1) * 128]
            s = _dot_nt(sk_ref[hd, c], qn)
            st.append(s)
            top.append(_top_vals_rows(s, PEER_TOPK))
        cand = jnp.concatenate([top[0][a:a + 1, :] + top[1] for a in range(PEER_TOPK)], axis=0)
        best = _top_vals_rows(cand, PEER_TOPK)
        tau = best[PEER_TOPK - 1:PEER_TOPK, :]
        z = jnp.sum(jnp.where(cand >= tau, jnp.exp(cand - best[0:1, :]), 0.0), axis=0, keepdims=True)
        s1_ref[hd] = st[0]
        s2_ref[hd] = st[1]
        e1_ref[hd] = jnp.exp(st[0] - top[0][0:1, :]) / z
        e2_ref[hd] = jnp.exp(st[1] - top[1][0:1, :])
        tau_ref[hd:hd + 1, :] = tau


def _peer_dense_kernel(x_ref, h2_ref, u_ref, vt_ref, s1_ref, e1_ref, s2_ref, e2_ref, tau_ref, o_ref, acc_ref, *, ea):
    j = pl.program_id(1)

    @pl.when(j == 0)
    def _():
        acc_ref[...] = jnp.zeros(acc_ref.shape, F32)

    h2 = h2_ref[...]
    ws = []
    for aa in range(ea):
        a = j * ea + aa
        act = jax.nn.gelu(_dot_nt(u_ref[aa * N_KEYS:(aa + 1) * N_KEYS, :], h2))
        gate = jnp.zeros(act.shape, F32)
        for hd in range(PEER_HEADS):
            cand = s2_ref[hd] + s1_ref[hd, pl.ds(a, 1), :]
            gate = gate + jnp.where(cand >= tau_ref[hd:hd + 1, :], e2_ref[hd], 0.0) * e1_ref[hd, pl.ds(a, 1), :]
        ws.append((act * gate).astype(BF16))
    acc_ref[...] += jnp.dot(vt_ref[...], jnp.concatenate(ws, axis=0), preferred_element_type=F32)

    @pl.when(j == pl.num_programs(1) - 1)
    def _():
        o_ref[...] = x_ref[...] + acc_ref[...].T


PEER_EA = 4


def _peer(x1, p, tm):
    n = x1.shape[0]
    nt = n // tm
    full = lambda shape: pl.BlockSpec(shape, lambda i: (0,) * len(shape))
    tk = pl.BlockSpec((PEER_HEADS, N_KEYS, tm), lambda i: (0, 0, i))
    sd = jax.ShapeDtypeStruct((PEER_HEADS, N_KEYS, n), F32)
    h2, s1, e1, s2, e2, tau = pl.pallas_call(
        _peer_route_kernel,
        grid=(nt,),
        in_specs=[pl.BlockSpec((tm, 1024), lambda i: (i, 0)), full((1, 1024)), full((1024, 2048)), full((1, 256)),
                  full((PEER_HEADS, 2, N_KEYS, N_KEYS))],
        out_specs=[pl.BlockSpec((tm, 1024), lambda i: (i, 0)), tk, tk, tk, tk,
                   pl.BlockSpec((PEER_HEADS, tm), lambda i: (0, i))],
        out_shape=[jax.ShapeDtypeStruct((n, 1024), BF16), sd, sd, sd, sd,
                   jax.ShapeDtypeStruct((PEER_HEADS, n), F32)],
        compiler_params=_cparams(("parallel",)),
        name="peer_route",
    )(x1, p["ln2_g"], p["p_wq"], p["p_qnorm_g"], p["p_subkeys"])
    et = PEER_EA * N_KEYS
    n_exp = p["p_u"].shape[0]
    tk2 = pl.BlockSpec((PEER_HEADS, N_KEYS, tm), lambda i, j: (0, 0, i))
    return pl.pallas_call(
        functools.partial(_peer_dense_kernel, ea=PEER_EA),
        grid=(nt, n_exp // et),
        in_specs=[pl.BlockSpec((tm, 1024), lambda i, j: (i, 0)), pl.BlockSpec((tm, 1024), lambda i, j: (i, 0)),
                  pl.BlockSpec((et, 1024), lambda i, j: (j, 0)), pl.BlockSpec((1024, et), lambda i, j: (0, j)),
                  tk2, tk2, tk2, tk2, pl.BlockSpec((PEER_HEADS, tm), lambda i, j: (0, i))],
        out_specs=pl.BlockSpec((tm, 1024), lambda i, j: (i, 0)),
        out_shape=jax.ShapeDtypeStruct((n, 1024), F32),
        scratch_shapes=[pltpu.VMEM((1024, tm), F32)],
        compiler_params=_cparams(("parallel", "arbitrary")),
        name="peer_dense",
    )(x1, h2, p["p_u"], p["p_vt"], s1, e1, s2, e2, tau)


def _prep_layer(l, ln1_g, ln2_g, w_in, w_out, a_vnorm_g, a_ws, a_bs, b_mu, b_w0, b_w2, b_a0, b_a2, b_g2, b_kk,
                b_ka, b_rk, b_lnx_g, b_lnx_b, c_qnorm_g, c_knorm_g, c_cmp_w1, c_cmp_b1, c_cmp_w2, p_wq,
                p_qnorm_g, p_subkeys, p_u, p_v, dec_seq):
    p = {}
    p["ln1_g"] = ln1_g[l][None, :]
    p["ln2_g"] = ln2_g[l][None, :]
    w = w_in[l]
    z = lambda c: jnp.zeros((w.shape[0], c), w.dtype)
    p["w_in"] = jnp.concatenate([w[:, :1472], z(64), w[:, 1472:], z(104)], axis=1).astype(BF16)
    p["a_vnorm_g"] = a_vnorm_g[l][None, :]
    wm = jnp.where(jnp.tril(jnp.ones((CHUNK, CHUNK), bool)), a_ws[l], 0.0)
    p["wm_prompt"] = wm.astype(BF16)
    p["brow_prompt"] = jnp.repeat(a_bs[l].T, HEAD_DIM, axis=1)
    reps = CHUNK // dec_seq
    eye = jnp.eye(reps, dtype=F32)
    p["wm_sample"] = jnp.stack([jnp.kron(eye, wm[g, :dec_seq, :dec_seq]) for g in range(4)]).astype(BF16)
    p["brow_sample"] = jnp.tile(p["brow_prompt"][:dec_seq], (reps, 1))
    p["c_qnorm_g"] = jnp.tile(c_qnorm_g[l], C_HEADS)[None, :]
    p["c_knorm_g12"] = jnp.stack([jnp.tile(c_knorm_g[l, 1], 2), jnp.tile(c_knorm_g[l, 2], 2)])
    p["bd512"] = _block_ones(512, HEAD_DIM)
    p["bd256"] = _block_ones(256, HEAD_DIM)
    p["bd128"] = _block_ones(128, HEAD_DIM)
    eye2 = jnp.eye(2, dtype=F32)
    w1h = c_cmp_w1[l].reshape(2, 2, CMP_STRIDE, HEAD_DIM, HEAD_DIM)
    p["c_w1e"] = jnp.einsum("kjsdc,kK,hH->jsKHdkhc", w1h, eye2, eye2).reshape(2, 4096, 256).astype(BF16)
    p["c_b1e"] = jnp.broadcast_to(c_cmp_b1[l][:, None, :], (2, 2, HEAD_DIM)).reshape(1, 256)
    p["c_w2e"] = jnp.einsum("kcd,kK,hH->khcKHd", c_cmp_w2[l], eye2, eye2).reshape(256, 256).astype(BF16)
    p["c_kg0"] = jnp.tile(c_knorm_g[l, 0], 2)[None, :]
    row = lambda a: a.reshape(1, -1)
    p["b_mu"] = row(b_mu[l])
    p["b_w0"] = row(b_w0[l])
    p["b_a0"] = row(b_a0[l])
    zz = jnp.zeros((64, 256), F32)
    p["b_w2a"] = jnp.concatenate([jnp.concatenate([b_w2[l], zz], axis=1),
                                  jnp.concatenate([zz, b_a2[l]], axis=1)], axis=0)
    p["b_g2"] = b_g2[l]
    p["b_kk"] = row(b_kk[l])
    p["b_ka"] = row(b_ka[l])
    p["b_rk"] = row(b_rk[l])
    p["b_lnx_g"] = row(b_lnx_g[l])
    p["b_lnx_b"] = row(b_lnx_b[l])
    p["w_out"] = w_out[l].astype(BF16)
    p["p_wq"] = p_wq[l].astype(BF16)
    p["p_qnorm_g"] = row(p_qnorm_g[l])
    p["p_subkeys"] = p_subkeys[l].astype(BF16)
    p["p_u"] = p_u[l].astype(BF16)
    p["p_vt"] = p_v[l].T.astype(BF16)
    return p


def _st_in(wkv):
    b = wkv.shape[0]
    return wkv.transpose(0, 3, 1, 2).reshape(b, HEAD_DIM, 256)


def _st_out(st):
    b = st.shape[0]
    return st.reshape(b, HEAD_DIM, B_HEADS, HEAD_DIM).transpose(0, 2, 3, 1)


def kernel(x_prompt, x_sample, cache_cmp_kv, cache_slc_kv, cache_win_kv, state_wkv, state_shift, page_table, ln1_g, ln2_g, w_in, w_out, a_vnorm_g, a_ws, a_bs, b_mu, b_w0, b_w2, b_a0, b_a2, b_g2, b_kk, b_ka, b_rk, b_lnx_g, b_lnx_b, c_qnorm_g, c_knorm_g, c_cmp_w1, c_cmp_b1, c_cmp_w2, p_wq, p_qnorm_g, p_subkeys, p_u, p_v):
    weights = (ln1_g, ln2_g, w_in, w_out, a_vnorm_g, a_ws, a_bs, b_mu, b_w0, b_w2, b_a0, b_a2, b_g2, b_kk, b_ka,
               b_rk, b_lnx_g, b_lnx_b, c_qnorm_g, c_knorm_g, c_cmp_w1, c_cmp_b1, c_cmp_w2, p_wq, p_qnorm_g,
               p_subkeys, p_u, p_v)
    nb, t_len, d_model = x_prompt.shape
    ns, dec_seq, _ = x_sample.shape
    depth = ln1_g.shape[0]
    n_pool = cache_cmp_kv.shape[1]
    n_s = ns * dec_seq
    n_sp = -(-n_s // 128) * 128
    rwkv_c = 64
    dec_c = 8
    pad_rows = lambda a, n: jnp.pad(a, ((0, n - a.shape[0]),) + ((0, 0),) * (a.ndim - 1))
    xp = x_prompt.reshape(nb * t_len, d_model)
    xs = pad_rows(x_sample.reshape(n_s, d_model), n_sp)
    kv5 = lambda a, b, t: a.reshape(b, t, 2, C_KV_HEADS, HEAD_DIM)
    outs_p, outs_s = [], []
    for l in range(depth):
        p = _prep_layer(l, *weights, dec_seq=dec_seq)
        ya, _, bproj, q, cmp, slc, win, gate, kvb = _inproj(xp, p, p["wm_prompt"], p["brow_prompt"], 256)
        bp3 = bproj.reshape(nb, t_len, B_PROJ)
        prev = jnp.concatenate([jnp.zeros((nb, 1, B_PROJ), F32), bp3[:, :-1]], axis=1).reshape(nb * t_len, B_PROJ)
        yb, st = _rwkv(bproj, prev, jnp.zeros((nb, HEAD_DIM, 256), F32), p, nb, t_len // rwkv_c, rwkv_c, rwkv_c)
        kcvc = _compress(cmp, p, nb)
        yc = _nsa_prompt(q, gate, kcvc, kvb, nb, t_len)
        x1 = _outproj(xp, ya, yb, yc, p["w_out"], nb, t_len, 512)
        xp = _peer(x1, p, 512)
        n_win = min(WINDOW, t_len)
        outs_p.append((kv5(cmp, nb, t_len), kv5(slc, nb, t_len), kv5(win, nb, t_len)[:, t_len - n_win:],
                       _st_out(st), bp3[:, -1]))
        sya, sv, sbproj, sq, scmp, sslc, swin, sgate, _ = _inproj(xs, p, p["wm_sample"], p["brow_sample"], 128)
        sb3 = sbproj[:n_s].reshape(ns, dec_seq, B_PROJ)
        sprev = jnp.concatenate([state_shift[l][:, None], sb3[:, :-1]], axis=1)
        pad_c = lambda a: jnp.pad(a, ((0, 0), (0, dec_c - dec_seq), (0, 0))).reshape(ns * dec_c, B_PROJ)
        syb, sst = _rwkv(pad_c(sb3), pad_c(sprev), _st_in(state_wkv[l]), p, ns, 1, dec_c, dec_seq)
        syb = pad_rows(syb.reshape(ns, dec_c, 256)[:, :dec_seq].reshape(n_s, 256), n_sp)
        skc = _compress_paged(cache_cmp_kv[l].reshape(n_pool, PAGE, 256), page_table, p)
        q16 = (sq[:n_s].reshape(ns, dec_seq, C_KV_HEADS, C_GROUP, HEAD_DIM).transpose(0, 2, 3, 1, 4)
               .reshape(ns, C_KV_HEADS, C_GROUP * dec_seq, HEAD_DIM))
        g16 = (sgate[:n_s, :3 * C_HEADS].reshape(ns, dec_seq, C_KV_HEADS, C_GROUP, 3).transpose(0, 2, 3, 1, 4)
               .reshape(ns, C_KV_HEADS, C_GROUP * dec_seq, 3))
        new_rows = jnp.concatenate([sslc[:n_s], swin[:n_s]], axis=1).reshape(ns, dec_seq, 512)
        new_rows = jnp.pad(new_rows, ((0, 0), (0, 128 - dec_seq), (0, 0)))
        win_buf = cache_win_kv[l].reshape(ns, -1, 256)
        so = _nsa_sample(page_table, q16, g16, skc, cache_slc_kv[l].reshape(n_pool, PAGE, 256), new_rows, win_buf,
                         dec_seq)
        syc = (so.reshape(ns, C_KV_HEADS, C_GROUP, dec_seq, HEAD_DIM).transpose(1, 2, 0, 3, 4)
               .reshape(C_HEADS, n_s, HEAD_DIM))
        syc = jnp.pad(syc, ((0, 0), (0, n_sp - n_s), (0, 0)))[None]
        sx1 = _outproj(xs, sya, syb, syc, p["w_out"], 1, n_sp, 128)
        xs = _peer(sx1, p, 128)
        swin3 = swin[:n_s].reshape(ns, dec_seq, 256)
        win_new = jnp.concatenate([win_buf, swin3], axis=1)[:, dec_seq:]
        outs_s.append((kv5(scmp[:n_s], ns, dec_seq), kv5(sslc[:n_s], ns, dec_seq),
                       kv5(win_new, ns, win_buf.shape[1]), _st_out(sst), sb3[:, -1],
                       sv[:n_s].reshape(ns, dec_seq, A_WIDTH)))
    stk = lambda lst, i: jnp.stack([s[i] for s in lst], axis=0)
    return (xp.reshape(nb, t_len, d_model), xs[:n_s].reshape(ns, dec_seq, d_model),
            stk(outs_p, 0), stk(outs_p, 1), stk(outs_p, 2), stk(outs_p, 3), stk(outs_p, 4),
            stk(outs_s, 0), stk(outs_s, 1), stk(outs_s, 2), stk(outs_s, 3), stk(outs_s, 4), stk(outs_s, 5))
```

```python
import functools
import math

import jax
import jax.numpy as jnp
from jax import lax
from jax.experimental import pallas as pl
from jax.experimental.pallas import tpu as pltpu

F32 = jnp.float32
BF16 = jnp.bfloat16
HIGHEST = lax.Precision.HIGHEST

HEAD_DIM = 64
CHUNK = 128
A_WIDTH = 256
B_WIDTH = 256
B_HEADS = 4
B_PROJ = 960
C_HEADS = 8
C_KV_HEADS = 2
C_GROUP = 4
C_WIDTH = 512
CMP_STRIDE = 16
SLC_BLOCK = 64
SLC_TOPK = 16
WINDOW = 512
Q_BLOCK = 128
PAGE = 128
PEER_HEADS = 8
N_KEYS = 128
PEER_TOPK = 16
RWKV_GN_EPS = 64e-5
NEG_BIG = -1e30
POS_BIG = 1e30
NEG_INF = float("-inf")
VMEM_LIMIT = 56 * 1024 * 1024


def _cparams(sem):
    return pltpu.CompilerParams(dimension_semantics=sem, vmem_limit_bytes=VMEM_LIMIT)


def _dot(a, b):
    return jnp.dot(a.astype(BF16), b.astype(BF16), preferred_element_type=F32)


def _dot_nt(a, b):
    return lax.dot_general(a.astype(BF16), b.astype(BF16), (((1,), (1,)), ((), ())),
                           preferred_element_type=F32)


def _dot2(x, m):
    hi = x.astype(BF16)
    lo = (x - hi.astype(F32)).astype(BF16)
    return (jnp.dot(hi, m, preferred_element_type=F32) + jnp.dot(lo, m, preferred_element_type=F32))


def _dot3(a, b, dims):
    a_hi = a.astype(BF16)
    b_hi = b.astype(BF16)
    a_lo = (a - a_hi.astype(F32)).astype(BF16)
    b_lo = (b - b_hi.astype(F32)).astype(BF16)
    dg = lambda x, y: lax.dot_general(x, y, (dims, ((), ())), preferred_element_type=F32)
    return dg(a_hi, b_hi) + (dg(a_hi, b_lo) + dg(a_lo, b_hi))


def _hdot(a, b):
    return _dot3(a, b, ((1,), (0,)))


def _hdot_nt(a, b):
    return _dot3(a, b, ((1,), (1,)))


def _hdot_tn(a, b):
    return _dot3(a, b, ((0,), (0,)))


def _block_ones(width, group):
    i = jnp.arange(width)
    return (i[:, None] // group == i[None, :] // group).astype(BF16)


IN_PAD = 2944


def _inproj_kernel(x_ref, g_ref, w_ref, avg_ref, wm_ref, brow_ref, qg_ref, kg_ref, bd_ref,
                   ya_ref, v_ref, b_ref, q_ref, cmp_ref, slc_ref, win_ref, gate_ref, kvb_ref, *, tm):
    x = x_ref[...]
    h = x * lax.rsqrt(jnp.mean(x * x, axis=-1, keepdims=True) + 1e-6) * g_ref[...]
    z = jnp.dot(h.astype(BF16), w_ref[...], preferred_element_type=F32)
    b_ref[...] = z[:, 512:512 + B_PROJ]
    u = jax.nn.gelu(z[:, 0:256])
    gv = jax.nn.gelu(z[:, 256:512])
    v = gv * lax.rsqrt(jnp.mean(gv * gv, axis=-1, keepdims=True) + 1e-6) * avg_ref[...]
    v_ref[...] = v
    lane_g = lax.broadcasted_iota(jnp.int32, (CHUNK, A_WIDTH), 1) // HEAD_DIM
    for c in range(tm // CHUNK):
        vc = v[c * CHUNK:(c + 1) * CHUNK].astype(BF16)
        mixed = brow_ref[...]
        for g in range(4):
            mg = jnp.dot(wm_ref[g], vc, preferred_element_type=F32)
            mixed = mixed + jnp.where(lane_g == g, mg, 0.0)
        ya_ref[c * CHUNK:(c + 1) * CHUNK, :] = u[c * CHUNK:(c + 1) * CHUNK] * mixed
    bd = bd_ref[...]
    q = z[:, 1536:2048]
    q_ref[...] = q * lax.rsqrt(_dot2(q * q, bd) * (1.0 / HEAD_DIM) + 1e-6) * qg_ref[...]
    bd128 = bd[0:128, 0:128]
    cmp_ref[...] = z[:, 2048:2304]
    for j, (o_ref, off) in enumerate(((slc_ref, 2304), (win_ref, 2560))):
        k = z[:, off:off + 128]
        kn = k * lax.rsqrt(_dot2(k * k, bd128) * (1.0 / HEAD_DIM) + 1e-6) * kg_ref[j:j + 1, :]
        vv = z[:, off + 128:off + 256]
        o_ref[:, 0:128] = kn
        o_ref[:, 128:256] = vv
        kvb_ref[:, j * 256:j * 256 + 128] = kn.astype(BF16)
        kvb_ref[:, j * 256 + 128:j * 256 + 256] = vv.astype(BF16)
    gate_ref[...] = jax.nn.sigmoid(z[:, 2816:2944])


def _inproj(x, p, wm, brow, tm):
    n = x.shape[0]
    full = lambda shape: pl.BlockSpec(shape, lambda i: (0,) * len(shape))
    rows = lambda w: pl.BlockSpec((tm, w), lambda i: (i, 0))
    outs = [(A_WIDTH, F32), (A_WIDTH, F32), (B_PROJ, F32), (C_WIDTH, F32), (256, F32), (256, F32), (256, F32),
            (128, F32), (512, BF16)]
    return pl.pallas_call(
        functools.partial(_inproj_kernel, tm=tm),
        grid=(n // tm,),
        in_specs=[rows(1024), full((1, 1024)), full((1024, IN_PAD)), full((1, A_WIDTH)), full((4, CHUNK, CHUNK)),
                  full((CHUNK, A_WIDTH)), full((1, C_WIDTH)), full((2, 128)), full((512, 512))],
        out_specs=[rows(w) for w, _ in outs],
        out_shape=[jax.ShapeDtypeStruct((n, w), d) for w, d in outs],
        compiler_params=_cparams(("parallel",)),
        name="inproj",
    )(x, p["ln1_g"], p["w_in"], p["a_vnorm_g"], wm, brow, p["c_qnorm_g"], p["c_knorm_g12"], p["bd512"])


def _rwkv_prep_kernel(x_ref, xp_ref, mu_ref, w0_ref, w2a_ref, a0_ref, g2_ref, kkp_ref, ka_ref, rk_ref, bd_ref,
                      tri_ref, p2_ref, y0_ref, p3_ref, z_ref, g_ref, bonus_ref, *, c, t_valid):
    x = x_ref[...]
    xs = x + (xp_ref[...] - x) * mu_ref[...]
    r = xs[:, 0:256]
    k = xs[:, 256:512]
    v = xs[:, 512:768]
    wa = xs[:, 768:896]
    lane = lax.broadcasted_iota(jnp.int32, wa.shape, 1)
    pre = _hdot(jnp.where(lane < 64, jnp.tanh(wa), wa), w2a_ref[...])
    y = -(w0_ref[...] + pre[:, 0:256])
    softplus = jnp.maximum(y, 0.0) + jnp.log1p(jnp.exp(-jnp.abs(y)))
    ew = jnp.exp(-softplus - 0.5)
    a = jax.nn.sigmoid(a0_ref[...] + pre[:, 256:512])
    g_ref[...] = _hdot(jax.nn.sigmoid(xs[:, 896:960]), g2_ref[...])
    bd = bd_ref[...]
    kkr = k * kkp_ref[...]
    kk = kkr / jnp.maximum(jnp.sqrt(_dot2(kkr * kkr, bd)), 1e-12)
    k2 = k * (1.0 + (a - 1.0) * ka_ref[...])
    bonus_ref[...] = _dot2(r * k2 * rk_ref[...], bd) * v
    if t_valid < c:
        live = lax.broadcasted_iota(jnp.int32, ew.shape, 0) < t_valid
        ew = jnp.where(live, ew, 0.0)
        kk = jnp.where(live, kk, 0.0)
        k2 = jnp.where(live, k2, 0.0)
        v = jnp.where(live, v, 0.0)
    cum = jnp.dot(tri_ref[...], -ew, preferred_element_type=F32, precision=HIGHEST)
    gam = jnp.exp(cum)
    ginv = jnp.exp(-cum)
    g_end = gam[c - 1:c, :]
    alpha_t = -kk * jnp.exp(cum + ew)
    bhat = kk * a * ginv
    khat = k2 * ginv
    rt = r * gam
    kbar = khat * g_end
    bbar = bhat * g_end
    row = lax.broadcasted_iota(jnp.int32, (c, c), 0)
    col = lax.broadcasted_iota(jnp.int32, (c, c), 1)
    eye_c = (row == col).astype(F32)
    r64 = lax.broadcasted_iota(jnp.int32, (HEAD_DIM, HEAD_DIM), 0)
    c64 = lax.broadcasted_iota(jnp.int32, (HEAD_DIM, HEAD_DIM), 1)
    hs = range(B_HEADS)
    sls = [slice(h * HEAD_DIM, (h + 1) * HEAD_DIM) for h in hs]
    al = [alpha_t[:, s] for s in sls]
    bh = [bhat[:, s] for s in sls]
    kh = [khat[:, s] for s in sls]
    rh = [rt[:, s] for s in sls]
    vh = [v[:, s] for s in sls]
    mb = [jnp.where(row > col, _hdot_nt(al[h], bh[h]), 0.0) for h in hs]
    mk = [jnp.where(row > col, _hdot_nt(al[h], kh[h]), 0.0) for h in hs]
    qk = [jnp.where(row >= col, _hdot_nt(rh[h], kh[h]), 0.0) for h in hs]
    qb = [jnp.where(row >= col, _hdot_nt(rh[h], bh[h]), 0.0) for h in hs]
    mkv = [_hdot(mk[h], vh[h]) for h in hs]
    tinv = [eye_c + mb[h] for h in hs]
    pw = mb
    for _ in range(int(math.log2(c)) - 1):
        pw = [_hdot(pw[h], pw[h]) for h in hs]
        tinv = [tinv[h] + _hdot(tinv[h], pw[h]) for h in hs]
    p1 = [_hdot(tinv[h], al[h]) for h in hs]
    u = [_hdot(tinv[h], mkv[h]) for h in hs]
    for h in hs:
        sl = sls[h]
        p2_ref[:, sl] = rh[h] + _hdot(qb[h], p1[h])
        y0_ref[:, sl] = _hdot(qk[h], vh[h]) + _hdot(qb[h], u[h])
        p3_ref[:, sl] = jnp.where(r64 == c64, g_end[:, sl], 0.0) + _hdot_tn(bbar[:, sl], p1[h])
        z_ref[:, sl] = _hdot_tn(kbar[:, sl], vh[h]) + _hdot_tn(bbar[:, sl], u[h])


def _rwkv_seq_kernel(st0_ref, p2_ref, y0_ref, p3_ref, z_ref, g_ref, bonus_ref, lg_ref, lb_ref,
                     y_ref, st_ref, st_scr):
    @pl.when(pl.program_id(1) == 0)
    def _():
        st_scr[...] = st0_ref[...]

    for h in range(B_HEADS):
        sl = slice(h * HEAD_DIM, (h + 1) * HEAD_DIM)
        st = st_scr[:, sl]
        y = _hdot(p2_ref[:, sl], st) + y0_ref[:, sl]
        st_scr[:, sl] = _hdot(p3_ref[:, sl], st) + z_ref[:, sl]
        mu = jnp.mean(y, axis=-1, keepdims=True)
        var = jnp.mean(jnp.square(y - mu), axis=-1, keepdims=True)
        yn = (y - mu) * lax.rsqrt(var + RWKV_GN_EPS) * lg_ref[:, sl] + lb_ref[:, sl]
        y_ref[:, sl] = (yn + bonus_ref[:, sl]) * g_ref[:, sl]
    st_ref[...] = st_scr[...]


def _rwkv(proj, prev, st0, p, nb, nch, c, t_valid):
    n = proj.shape[0]
    full = lambda shape: pl.BlockSpec(shape, lambda b, i: (0,) * len(shape))
    rows = lambda w: pl.BlockSpec((c, w), lambda b, i: (b * nch + i, 0))
    mats = pl.BlockSpec((HEAD_DIM, 256), lambda b, i: (b * nch + i, 0))
    tri = (jnp.arange(c)[:, None] >= jnp.arange(c)[None, :]).astype(F32)
    p2, y0, p3, z, g, bonus = pl.pallas_call(
        functools.partial(_rwkv_prep_kernel, c=c, t_valid=t_valid),
        grid=(nb, nch),
        in_specs=[rows(B_PROJ), rows(B_PROJ), full((1, B_PROJ)), full((1, 256)), full((128, 512)), full((1, 256)),
                  full((64, 256)), full((1, 256)), full((1, 256)), full((1, 256)), full((256, 256)), full((c, c))],
        out_specs=[rows(256), rows(256), mats, mats, rows(256), rows(256)],
        out_shape=[jax.ShapeDtypeStruct((n, 256), F32), jax.ShapeDtypeStruct((n, 256), F32),
                   jax.ShapeDtypeStruct((nb * nch * HEAD_DIM, 256), F32),
                   jax.ShapeDtypeStruct((nb * nch * HEAD_DIM, 256), F32),
                   jax.ShapeDtypeStruct((n, 256), F32), jax.ShapeDtypeStruct((n, 256), F32)],
        compiler_params=_cparams(("parallel", "parallel")),
        name="rwkv_prep",
    )(proj, prev, p["b_mu"], p["b_w0"], p["b_w2a"], p["b_a0"], p["b_g2"], p["b_kk"], p["b_ka"], p["b_rk"],
      p["bd256"], tri)
    st_spec = pl.BlockSpec((None, HEAD_DIM, 256), lambda b, i: (b, 0, 0))
    y, st = pl.pallas_call(
        _rwkv_seq_kernel,
        grid=(nb, nch),
        in_specs=[st_spec, rows(256), rows(256), mats, mats, rows(256), rows(256), full((1, 256)), full((1, 256))],
        out_specs=[rows(256), st_spec],
        out_shape=[jax.ShapeDtypeStruct((n, 256), F32), jax.ShapeDtypeStruct((nb, HEAD_DIM, 256), F32)],
        scratch_shapes=[pltpu.VMEM((HEAD_DIM, 256), F32)],
        compiler_params=_cparams(("parallel", "arbitrary")),
        name="rwkv_seq",
    )(st0, p2, y0, p3, z, g, bonus, p["b_lnx_g"], p["b_lnx_b"])
    return y, st


def _compress_tail(a0, a1, n_out, b1_ref, w2_ref, kg_ref, bd_ref, o_ref):
    n = a1.shape[0]
    hid = jax.nn.gelu(a0 + pltpu.roll(a1, n - 1, 0) + b1_ref[...])[0:n_out]
    out = _dot(hid, w2_ref[...])
    k = out[:, 0:128]
    kn = k * lax.rsqrt(_dot2(k * k, bd_ref[...]) * (1.0 / HEAD_DIM) + 1e-6) * kg_ref[...]
    o_ref[:, 0:128] = kn.astype(BF16)
    o_ref[:, 128:256] = out[:, 128:256].astype(BF16)


def _compress_kernel(x_ref, w1_ref, b1_ref, w2_ref, kg_ref, bd_ref, o_ref):
    x = x_ref[...].astype(BF16)
    a0 = jnp.dot(x, w1_ref[0], preferred_element_type=F32)
    a1 = jnp.dot(x, w1_ref[1], preferred_element_type=F32)
    _compress_tail(a0, a1, x.shape[0], b1_ref, w2_ref, kg_ref, bd_ref, o_ref)


def _compress_paged_kernel(pt_ref, *refs, pg):
    pages = refs[:pg + 1]
    w1_ref, b1_ref, w2_ref, kg_ref, bd_ref, o_ref = refs[pg + 1:]
    x = jnp.concatenate([r[...] for r in pages], axis=0).astype(BF16)
    a0 = jnp.dot(x, w1_ref[0], preferred_element_type=F32)
    a1 = jnp.dot(x, w1_ref[1], preferred_element_type=F32)
    _compress_tail(a0, a1, pg * 8, b1_ref, w2_ref, kg_ref, bd_ref, o_ref)


def _compress_weight_specs(index):
    full = lambda shape: pl.BlockSpec(shape, index(len(shape)))
    return [full((2, 4096, 256)), full((1, 256)), full((256, 256)), full((1, 128)), full((128, 128))]


def _compress(cmp_rows, p, nb):
    n_piece = cmp_rows.shape[0] // nb // CMP_STRIDE
    x = cmp_rows.reshape(nb, n_piece, 4096)
    return pl.pallas_call(
        _compress_kernel,
        grid=(nb,),
        in_specs=[pl.BlockSpec((None, n_piece, 4096), lambda b: (b, 0, 0))]
        + _compress_weight_specs(lambda r: (lambda b: (0,) * r)),
        out_specs=pl.BlockSpec((None, n_piece, 256), lambda b: (b, 0, 0)),
        out_shape=jax.ShapeDtypeStruct((nb, n_piece, 256), BF16),
        compiler_params=_cparams(("parallel",)),
        name="compress",
    )(x, p["c_w1e"], p["c_b1e"], p["c_w2e"], p["c_kg0"], p["bd128"])


CMP_PAGES = 16


def _compress_paged(cache, page_table, p):
    ns, n_pages = page_table.shape
    pg = min(CMP_PAGES, n_pages)
    x = cache.reshape(cache.shape[0], PAGE // CMP_STRIDE, 4096)

    def page_spec(j):
        return pl.BlockSpec((None, PAGE // CMP_STRIDE, 4096),
                            lambda s, i, pt: (pt[s, jnp.minimum(i * pg + j, n_pages - 1)], 0, 0))

    return pl.pallas_call(
        functools.partial(_compress_paged_kernel, pg=pg),
        grid_spec=pltpu.PrefetchScalarGridSpec(
            num_scalar_prefetch=1,
            grid=(ns, n_pages // pg),
            in_specs=[page_spec(j) for j in range(pg + 1)]
            + _compress_weight_specs(lambda r: (lambda s, i, pt: (0,) * r)),
            out_specs=pl.BlockSpec((None, pg * 8, 256), lambda s, i, pt: (s, i, 0)),
        ),
        out_shape=jax.ShapeDtypeStruct((ns, n_pages * 8, 256), BF16),
        compiler_params=_cparams(("parallel", "arbitrary")),
        name="compress_paged",
    )(page_table, *([x] * (pg + 1)), p["c_w1e"], p["c_b1e"], p["c_w2e"], p["c_kg0"], p["bd128"])


def _alibi_slope(head):
    return 2.0 ** (-8.0 * (head + 1.0) / C_HEADS)


def _topk_mask(x, k):
    nl = x.shape[-1]
    lane = lax.broadcasted_iota(jnp.int32, x.shape, x.ndim - 1).astype(F32)
    sel = jnp.zeros(x.shape, F32)
    for _ in range(k):
        m = jnp.max(x, axis=-1, keepdims=True)
        idx = jnp.min(jnp.where(x == m, lane, float(nl)), axis=-1, keepdims=True)
        hit = lane == idx
        sel = jnp.where(hit, 1.0, sel)
        x = jnp.where(hit, NEG_INF, x)
    return sel


def _masked_softmax(s, valid):
    s = jnp.where(valid, s, NEG_BIG)
    e = jnp.exp(s - jnp.max(s, axis=-1, keepdims=True))
    return jnp.where(valid, e / jnp.sum(e, axis=-1, keepdims=True), 0.0)


def _rows4(x):
    return jnp.concatenate([x, x, x, x], axis=0)


def _flash_step(carry, s, ok, v, v_is_t=False):
    m, l, acc = carry
    if ok is not None:
        s = jnp.where(ok, s, NEG_BIG)
    m_new = jnp.maximum(m, jnp.max(s, axis=-1, keepdims=True))
    alpha = jnp.exp(m - m_new)
    pr = jnp.exp(s - m_new)
    if ok is not None:
        pr = jnp.where(ok, pr, 0.0)
    l = alpha * l + jnp.sum(pr, axis=-1, keepdims=True)
    pv = _dot_nt(pr, v) if v_is_t else jnp.dot(pr.astype(BF16), v, preferred_element_type=F32)
    return m_new, l, alpha * acc + pv


def _flash_init(rows):
    return (jnp.full((rows, 1), NEG_BIG, F32), jnp.zeros((rows, 1), F32), jnp.zeros((rows, HEAD_DIM), F32))


def _nsa_prompt_kernel(q_ref, gate_ref, kc_ref, kvb_ref, pool_ref, o_ref, *, n_slc):
    i = pl.program_id(1)
    qb = Q_BLOCK
    n_cmp = kc_ref.shape[0]
    nbl = pool_ref.shape[1]
    tok = lax.broadcasted_iota(jnp.int32, (qb, 1), 0) + i * qb
    t4 = _rows4(tok)
    lane128 = lax.broadcasted_iota(jnp.int32, (1, 128), 1)
    for h in range(C_KV_HEADS):
        heads = [C_GROUP * h + g for g in range(C_GROUP)]
        qh = (jnp.concatenate([q_ref[:, hd * HEAD_DIM:(hd + 1) * HEAD_DIM] for hd in heads], axis=0)
              * (HEAD_DIM ** -0.5)).astype(BF16)
        slope = jnp.concatenate([jnp.full((qb, 1), _alibi_slope(hd), F32) for hd in heads], axis=0)
        kcol = slice(h * HEAD_DIM, (h + 1) * HEAD_DIM)
        vcol = slice(128 + h * HEAD_DIM, 128 + (h + 1) * HEAD_DIM)
        c_dist = t4 - (lax.broadcasted_iota(jnp.int32, (1, n_cmp), 1) * CMP_STRIDE + (2 * CMP_STRIDE - 1))
        s_c = _dot_nt(qh, kc_ref[:, kcol]) - slope * c_dist.astype(F32)
        p_c = _masked_softmax(s_c, c_dist >= 0)
        o_c = jnp.dot(p_c.astype(BF16), kc_ref[:, vcol], preferred_element_type=F32)
        psum = p_c[0:qb] + p_c[qb:2 * qb] + p_c[2 * qb:3 * qb] + p_c[3 * qb:4 * qb]
        imp = _dot2(psum, pool_ref[...])
        blk = lax.broadcasted_iota(jnp.int32, (1, nbl), 1)
        imp = jnp.where(blk * SLC_BLOCK <= tok, imp, NEG_BIG)
        imp = jnp.where((blk == 0) | (blk == tok // SLC_BLOCK), POS_BIG, imp)
        imp = jnp.where(blk < n_slc, imp, NEG_INF)
        sel_f = _topk_mask(imp, min(SLC_TOPK, n_slc))
        chunk_used = jnp.max(sel_f, axis=0, keepdims=True)
        sel_t = sel_f.T.astype(BF16)
        nq4 = C_GROUP * qb
        slopes = [_alibi_slope(hd) for hd in heads]
        q_t = (jnp.concatenate([q_ref[:, hd * HEAD_DIM:(hd + 1) * HEAD_DIM] for hd in heads], axis=0)
               * (HEAD_DIM ** -0.5)).T.astype(BF16)
        lane4 = lax.broadcasted_iota(jnp.int32, (1, nq4), 1)
        slope_row = jnp.full((1, nq4), slopes[0], F32)
        for g in range(1, C_GROUP):
            slope_row = jnp.where(lane4 // qb == g, slopes[g], slope_row)
        rel_t = (lax.broadcasted_iota(jnp.int32, (128, nq4), 0)
                 - lax.broadcasted_iota(jnp.int32, (128, nq4), 1) % qb).astype(F32)
        bias_t = slope_row * rel_t
        exp_row = lax.broadcasted_iota(jnp.int32, (128, nbl), 0) // SLC_BLOCK
        exp_col = lax.broadcasted_iota(jnp.int32, (128, nbl), 1)

        def tile_step(carry, c, kc, vc, ok):
            m, l, acc = carry
            s = (jnp.dot(kc, q_t, preferred_element_type=F32) + bias_t
                 + slope_row * ((c - i) * 128).astype(F32))
            if ok is not None:
                s = jnp.where(ok, s, NEG_BIG)
            m_new = jnp.maximum(m, jnp.max(s, axis=0, keepdims=True))
            alpha = jnp.exp(m - m_new)
            pr = jnp.exp(s - m_new)
            if ok is not None:
                pr = jnp.where(ok, pr, 0.0)
            l = alpha * l + jnp.sum(pr, axis=0, keepdims=True)
            pv = lax.dot_general(vc, pr.astype(BF16), (((0,), (0,)), ((), ())), preferred_element_type=F32)
            return m_new, l, alpha * acc + pv

        def sel_mask(c):
            expand = (exp_col == 2 * c + exp_row).astype(BF16)
            m1 = jnp.dot(expand, sel_t, preferred_element_type=F32) > 0.5
            return jnp.concatenate([m1] * C_GROUP, axis=1)

        def sel_body(c, carry):
            used = jnp.max(jnp.where(blk // 2 == c, chunk_used, 0.0))

            def visit(carry):
                r0 = pl.multiple_of(c * 128, 128)
                return tile_step(carry, c, kvb_ref[pl.ds(r0, 128), kcol], kvb_ref[pl.ds(r0, 128), vcol], sel_mask(c))

            return lax.cond(used > 0.0, visit, lambda carry: carry, carry)

        init = (jnp.full((1, nq4), NEG_BIG, F32), jnp.zeros((1, nq4), F32), jnp.zeros((HEAD_DIM, nq4), F32))
        carry = lax.fori_loop(0, i, sel_body, init)
        d0 = pl.multiple_of(i * 128, 128)
        causal = rel_t <= 0.0
        _, l_s, acc_s = tile_step(carry, i, kvb_ref[pl.ds(d0, 128), kcol], kvb_ref[pl.ds(d0, 128), vcol],
                                  sel_mask(i) & causal)

        wk = slice(256 + h * HEAD_DIM, 256 + (h + 1) * HEAD_DIM)
        wv = slice(384 + h * HEAD_DIM, 384 + (h + 1) * HEAD_DIM)
        n_wc = WINDOW // 128

        def win_edge(carry):
            r0 = pl.multiple_of((i - n_wc) * 128, 128)
            return tile_step(carry, i - n_wc, kvb_ref[pl.ds(r0, 128), wk], kvb_ref[pl.ds(r0, 128), wv], rel_t > 0.0)

        carry = lax.cond(i >= n_wc, win_edge, lambda carry: carry, init)

        def win_body(c, carry):
            r0 = pl.multiple_of(c * 128, 128)
            return tile_step(carry, c, kvb_ref[pl.ds(r0, 128), wk], kvb_ref[pl.ds(r0, 128), wv], None)

        carry = lax.fori_loop(jnp.maximum(i - n_wc + 1, 0), i, win_body, carry)
        _, l_w, acc_w = tile_step(carry, i, kvb_ref[pl.ds(d0, 128), wk], kvb_ref[pl.ds(d0, 128), wv], causal)
        o_s = (acc_s / l_s).T
        o_w = (acc_w / l_w).T
        for g, hd in enumerate(heads):
            rs = slice(g * qb, (g + 1) * qb)
            gt = gate_ref[:, 3 * hd:3 * hd + 3]
            o_ref[hd] = gt[:, 0:1] * o_c[rs] + gt[:, 1:2] * o_s[rs] + gt[:, 2:3] * o_w[rs]


def _topk_mask_rows(x, k):
    n = x.shape[0]
    ridx = lax.broadcasted_iota(jnp.int32, x.shape, 0).astype(F32)
    sel = jnp.zeros(x.shape, F32)
    for _ in range(k):
        m = jnp.max(x, axis=0, keepdims=True)
        idx = jnp.min(jnp.where(x == m, ridx, float(n)), axis=0, keepdims=True)
        hit = ridx == idx
        sel = jnp.where(hit, 1.0, sel)
        x = jnp.where(hit, NEG_INF, x)
    return sel


def _nsa_prompt_kernel_t(q_ref, gate_ref, kc_ref, kvb_ref, pool_ref, o_ref, *, n_slc):
    i = pl.program_id(1)
    qb = Q_BLOCK
    nq4 = C_GROUP * qb
    n_cmp = kc_ref.shape[0]
    nbl = pool_ref.shape[0]
    lane4 = lax.broadcasted_iota(jnp.int32, (1, nq4), 1)
    tq_row = i * qb + lane4 % qb
    tok_row = i * qb + lax.broadcasted_iota(jnp.int32, (1, qb), 1)
    gate_t = gate_ref[...].T
    blk_col = lax.broadcasted_iota(jnp.int32, (nbl, 1), 0)
    rel2 = (lax.broadcasted_iota(jnp.int32, (256, nq4), 0) - lane4 % qb).astype(F32)
    exp_row = lax.broadcasted_iota(jnp.int32, (256, nbl), 0) // SLC_BLOCK
    exp_col = lax.broadcasted_iota(jnp.int32, (256, nbl), 1)
    half = i // 2
    for h in range(C_KV_HEADS):
        heads = [C_GROUP * h + g for g in range(C_GROUP)]
        q_t = (jnp.concatenate([q_ref[:, hd * HEAD_DIM:(hd + 1) * HEAD_DIM] for hd in heads], axis=0)
               * (HEAD_DIM ** -0.5)).T.astype(BF16)
        slope_row = jnp.full((1, nq4), _alibi_slope(heads[0]), F32)
        for g in range(1, C_GROUP):
            slope_row = jnp.where(lane4 // qb == g, _alibi_slope(heads[g]), slope_row)
        kcol = slice(h * HEAD_DIM, (h + 1) * HEAD_DIM)
        vcol = slice(128 + h * HEAD_DIM, 128 + (h + 1) * HEAD_DIM)
        c_dist = tq_row - (lax.broadcasted_iota(jnp.int32, (n_cmp, 1), 0) * CMP_STRIDE + (2 * CMP_STRIDE - 1))
        valid = c_dist >= 0
        s_c = jnp.dot(kc_ref[:, kcol], q_t, preferred_element_type=F32) - slope_row * c_dist.astype(F32)
        s_c = jnp.where(valid, s_c, NEG_BIG)
        e_c = jnp.exp(s_c - jnp.max(s_c, axis=0, keepdims=True))
        p_c = jnp.where(valid, e_c / jnp.sum(e_c, axis=0, keepdims=True), 0.0)
        o_c = lax.dot_general(kc_ref[:, vcol], p_c.astype(BF16), (((0,), (0,)), ((), ())),
                              preferred_element_type=F32)
        psum = p_c[:, 0:qb] + p_c[:, qb:2 * qb] + p_c[:, 2 * qb:3 * qb] + p_c[:, 3 * qb:4 * qb]
        ps_hi = psum.astype(BF16)
        ps_lo = (psum - ps_hi.astype(F32)).astype(BF16)
        imp = (jnp.dot(pool_ref[...], ps_hi, preferred_element_type=F32)
               + jnp.dot(pool_ref[...], ps_lo, preferred_element_type=F32))
        imp = jnp.where(blk_col * SLC_BLOCK <= tok_row, imp, NEG_BIG)
        imp = jnp.where((blk_col == 0) | (blk_col == tok_row // SLC_BLOCK), POS_BIG, imp)
        imp = jnp.where(blk_col < n_slc, imp, NEG_INF)
        sel_f = _topk_mask_rows(imp, min(SLC_TOPK, n_slc))
        used_col = jnp.max(sel_f, axis=1, keepdims=True)
        sel_t = sel_f.astype(BF16)
        bias2 = slope_row * rel2

        def tile_step(carry, p, kcols, vcols, mask_fn):
            m, l, acc = carry
            r0 = pl.multiple_of(p * 256, 256)
            off = ((2 * p - i) * qb).astype(F32)
            s = jnp.dot(kvb_ref[pl.ds(r0, 256), kcols], q_t, preferred_element_type=F32) + bias2 + slope_row * off
            ok = mask_fn(p, off)
            if ok is not None:
                s = jnp.where(ok, s, NEG_BIG)
            m_new = jnp.maximum(m, jnp.max(s, axis=0, keepdims=True))
            alpha = jnp.exp(m - m_new)
            pr = jnp.exp(s - m_new)
            if ok is not None:
                pr = jnp.where(ok, pr, 0.0)
            l = alpha * l + jnp.sum(pr, axis=0, keepdims=True)
            pv = lax.dot_general(kvb_ref[pl.ds(r0, 256), vcols], pr.astype(BF16), (((0,), (0,)), ((), ())),
                                 preferred_element_type=F32)
            return m_new, l, alpha * acc + pv

        def sel_mask(p, off):
            expand = (exp_col == 4 * p + exp_row).astype(BF16)
            m1 = jnp.dot(expand, sel_t, preferred_element_type=F32) > 0.5
            return jnp.concatenate([m1] * C_GROUP, axis=1)

        def sel_causal_mask(p, off):
            return sel_mask(p, off) & (rel2 + off <= 0.0)

        def win_mask(p, off):
            d = rel2 + off
            return (d <= 0.0) & (d > -float(WINDOW))

        def sel_body(p, carry):
            used = jnp.max(jnp.where(blk_col // 4 == p, used_col, 0.0))
            return lax.cond(used > 0.0, lambda cr: tile_step(cr, p, kcol, vcol, sel_mask), lambda cr: cr, carry)

        init = (jnp.full((1, nq4), NEG_BIG, F32), jnp.zeros((1, nq4), F32), jnp.zeros((HEAD_DIM, nq4), F32))
        carry = lax.fori_loop(0, half, sel_body, init)
        _, l_s, acc_s = tile_step(carry, half, kcol, vcol, sel_causal_mask)

        wk = slice(256 + h * HEAD_DIM, 256 + (h + 1) * HEAD_DIM)
        wv = slice(384 + h * HEAD_DIM, 384 + (h + 1) * HEAD_DIM)
        n_wp = WINDOW // 256
        _, l_w, acc_w = lax.fori_loop(jnp.maximum(half - n_wp, 0), half + 1,
                                      lambda p, cr: tile_step(cr, p, wk, wv, win_mask), init)

        def gate_row(j):
            return jnp.concatenate([gate_t[3 * hd + j:3 * hd + j + 1, :] for hd in heads], axis=1)

        o_t = gate_row(0) * o_c + gate_row(1) * (acc_s / l_s) + gate_row(2) * (acc_w / l_w)
        o = o_t.T
        for g, hd in enumerate(heads):
            o_ref[hd] = o[g * qb:(g + 1) * qb]


def _nsa_prompt(q, gate, kcvc, kvb, nb, t_len):
    assert t_len % 256 == 0
    nq = t_len // Q_BLOCK
    n_cmp = kcvc.shape[1]
    n_slc = -(-t_len // SLC_BLOCK)
    nbl = -(-n_slc // 128) * 128
    pool = (jnp.arange(nbl)[:, None] == jnp.arange(n_cmp)[None, :] // (SLC_BLOCK // CMP_STRIDE)).astype(BF16)
    return pl.pallas_call(
        functools.partial(_nsa_prompt_kernel_t, n_slc=n_slc),
        grid=(nb, nq),
        in_specs=[pl.BlockSpec((Q_BLOCK, C_WIDTH), lambda b, i: (b * nq + i, 0)),
                  pl.BlockSpec((Q_BLOCK, 128), lambda b, i: (b * nq + i, 0)),
                  pl.BlockSpec((None, n_cmp, 256), lambda b, i: (b, 0, 0)),
                  pl.BlockSpec((t_len, 512), lambda b, i: (b, 0)),
                  pl.BlockSpec((nbl, n_cmp), lambda b, i: (0, 0))],
        out_specs=pl.BlockSpec((None, C_HEADS, Q_BLOCK, HEAD_DIM), lambda b, i: (b, 0, i, 0)),
        out_shape=jax.ShapeDtypeStruct((nb, C_HEADS, t_len, HEAD_DIM), F32),
        compiler_params=_cparams(("parallel", "arbitrary")),
        name="nsa_prompt",
    )(q, gate, kcvc, kvb, pool)


SLC_PAGES = 16


def _nsa_sample_kernel(pt_ref, *refs, pg, dec_seq, past_len, n_buf):
    pages = refs[:pg]
    (q_ref, gate_ref, kc_ref, new_ref, buf_ref, pool_ref, o_ref,
     sel_scr, ocw_scr, m_scr, l_scr, acc_scr) = refs[pg:]
    i = pl.program_id(1)
    n_parts = pl.num_programs(1)
    rows = C_GROUP * dec_seq
    n_cmp = kc_ref.shape[0]
    n_blk = pool_ref.shape[1]
    ridx = lax.broadcasted_iota(jnp.int32, (rows, 1), 0)
    tq = ridx % dec_seq
    t_abs = past_len + tq
    lane128 = lax.broadcasted_iota(jnp.int32, (1, 128), 1)

    def slope_of(h):
        s = jnp.full((rows, 1), _alibi_slope(C_GROUP * h), F32)
        for g in range(1, C_GROUP):
            s = jnp.where(ridx // dec_seq == g, _alibi_slope(C_GROUP * h + g), s)
        return s

    @pl.when(i == 0)
    def _():
        same_tok = (lax.broadcasted_iota(jnp.int32, (rows, rows), 0) % dec_seq
                    == lax.broadcasted_iota(jnp.int32, (rows, rows), 1) % dec_seq).astype(BF16)
        for h in range(C_KV_HEADS):
            slope = slope_of(h)
            qh = (q_ref[h] * (HEAD_DIM ** -0.5)).astype(BF16)
            kcol = slice(h * HEAD_DIM, (h + 1) * HEAD_DIM)
            vcol = slice(128 + h * HEAD_DIM, 128 + (h + 1) * HEAD_DIM)
            c_dist = t_abs - (lax.broadcasted_iota(jnp.int32, (1, n_cmp), 1) * CMP_STRIDE + (2 * CMP_STRIDE - 1))
            s_c = _dot_nt(qh, kc_ref[:, kcol]) - slope * c_dist.astype(F32)
            p_c = _masked_softmax(s_c, c_dist >= 0)
            o_c = jnp.dot(p_c.astype(BF16), kc_ref[:, vcol], preferred_element_type=F32)
            hi = p_c.astype(BF16)
            lo = (p_c - hi.astype(F32)).astype(BF16)
            psum = (jnp.dot(same_tok, hi, preferred_element_type=F32)
                    + jnp.dot(same_tok, lo, preferred_element_type=F32))
            imp = _dot2(psum, pool_ref[...])
            blk = lax.broadcasted_iota(jnp.int32, (1, n_blk), 1)
            imp = jnp.where(blk == 0, POS_BIG, imp)
            sel_scr[h] = _topk_mask(imp, min(SLC_TOPK, n_blk + 1) - 1)
            wk = slice(256 + h * HEAD_DIM, 256 + (h + 1) * HEAD_DIM)
            wv = slice(384 + h * HEAD_DIM, 384 + (h + 1) * HEAD_DIM)
            d_buf = (n_buf + tq) - lax.broadcasted_iota(jnp.int32, (1, n_buf), 1)
            s_b = _dot(qh, buf_ref[0, h]) - slope * d_buf.astype(F32)
            carry = _flash_step(_flash_init(rows), s_b, (d_buf >= 0) & (d_buf < WINDOW), buf_ref[1, h], True)
            d_new = tq - lane128
            s_n = _dot_nt(qh, new_ref[:, wk]) - slope * d_new.astype(F32)
            _, l_w, acc_w = _flash_step(carry, s_n, (d_new >= 0) & (lane128 < dec_seq),
                                        new_ref[:, wv].astype(BF16))
            gt = gate_ref[h]
            ocw_scr[h] = gt[:, 0:1] * o_c + gt[:, 2:3] * (acc_w / l_w)
            m0, l0, a0 = _flash_init(rows)
            m_scr[h] = m0
            l_scr[h] = l0
            acc_scr[h] = a0

    nk = pg * PAGE
    key_pos = i * nk + lax.broadcasted_iota(jnp.int32, (1, nk), 1)
    expand = (lax.broadcasted_iota(jnp.int32, (n_blk, nk), 0)
              == i * (nk // SLC_BLOCK) + lax.broadcasted_iota(jnp.int32, (n_blk, nk), 1) // SLC_BLOCK).astype(BF16)
    for h in range(C_KV_HEADS):
        slope = slope_of(h)
        qh = (q_ref[h] * (HEAD_DIM ** -0.5)).astype(BF16)
        k_all = jnp.concatenate([r[0, h] for r in pages], axis=1).astype(BF16)
        v_all = jnp.concatenate([r[1, h] for r in pages], axis=1).astype(BF16)
        dist = t_abs - key_pos
        selx = jnp.dot(sel_scr[h].astype(BF16), expand, preferred_element_type=F32)
        s = jnp.dot(qh, k_all, preferred_element_type=F32) - slope * dist.astype(F32)
        carry = _flash_step((m_scr[h], l_scr[h], acc_scr[h]), s, (dist >= 0) & (selx > 0.5), v_all, True)
        m_scr[h], l_scr[h], acc_scr[h] = carry

    @pl.when(i == n_parts - 1)
    def _():
        for h in range(C_KV_HEADS):
            slope = slope_of(h)
            qh = (q_ref[h] * (HEAD_DIM ** -0.5)).astype(BF16)
            kcol = slice(h * HEAD_DIM, (h + 1) * HEAD_DIM)
            vcol = slice(128 + h * HEAD_DIM, 128 + (h + 1) * HEAD_DIM)
            d_new = tq - lane128
            s_n = _dot_nt(qh, new_ref[:, kcol]) - slope * d_new.astype(F32)
            _, l_s, acc_s = _flash_step((m_scr[h], l_scr[h], acc_scr[h]), s_n,
                                        (d_new >= 0) & (lane128 < dec_seq), new_ref[:, vcol].astype(BF16))
            o_ref[h] = ocw_scr[h] + gate_ref[h][:, 1:2] * (acc_s / l_s)


def _nsa_sample(page_table, q16, gate16, kcvc, cache_slc_t, new_rows, win_buf_t, layer, dec_seq):
    ns, n_pages = page_table.shape
    pg = min(SLC_PAGES, n_pages)
    past_len = n_pages * PAGE
    n_buf = win_buf_t.shape[-1]
    n_cmp = kcvc.shape[1]
    n_blk = past_len // SLC_BLOCK
    rows = C_GROUP * dec_seq
    pool = (jnp.arange(n_cmp)[:, None] // (SLC_BLOCK // CMP_STRIDE) == jnp.arange(n_blk)[None, :]).astype(BF16)
    per_seq = lambda shape: pl.BlockSpec((None,) + shape, lambda s, i, pt: (s,) + (0,) * len(shape))

    kv_tile = (2, C_KV_HEADS, HEAD_DIM)

    def page_spec(j):
        return pl.BlockSpec((None, None) + kv_tile + (PAGE,),
                            lambda s, i, pt: (layer, pt[s, i * pg + j], 0, 0, 0, 0))

    return pl.pallas_call(
        functools.partial(_nsa_sample_kernel, pg=pg, dec_seq=dec_seq, past_len=past_len, n_buf=n_buf),
        grid_spec=pltpu.PrefetchScalarGridSpec(
            num_scalar_prefetch=1,
            grid=(ns, n_pages // pg),
            in_specs=[page_spec(j) for j in range(pg)]
            + [per_seq((C_KV_HEADS, rows, HEAD_DIM)), per_seq((C_KV_HEADS, rows, 3)), per_seq((n_cmp, 256)),
               per_seq((128, 512)),
               pl.BlockSpec((None, None) + kv_tile + (n_buf,), lambda s, i, pt: (layer, s, 0, 0, 0, 0)),
               pl.BlockSpec((n_cmp, n_blk), lambda s, i, pt: (0, 0))],
            out_specs=per_seq((C_KV_HEADS, rows, HEAD_DIM)),
            scratch_shapes=[pltpu.VMEM((C_KV_HEADS, rows, n_blk), F32), pltpu.VMEM((C_KV_HEADS, rows, HEAD_DIM), F32),
                            pltpu.VMEM((C_KV_HEADS, rows, 1), F32), pltpu.VMEM((C_KV_HEADS, rows, 1), F32),
                            pltpu.VMEM((C_KV_HEADS, rows, HEAD_DIM), F32)],
        ),
        out_shape=jax.ShapeDtypeStruct((ns, C_KV_HEADS, rows, HEAD_DIM), F32),
        compiler_params=_cparams(("parallel", "arbitrary")),
        name="nsa_sample",
    )(page_table, *([cache_slc_t] * pg), q16, gate16, kcvc, new_rows, win_buf_t, pool)


def _outproj_kernel(x_ref, ya_ref, yb_ref, yc_ref, w_ref, o_ref):
    acc = x_ref[...] + _dot(ya_ref[...], w_ref[0:256, :]) + _dot(yb_ref[...], w_ref[256:512, :])
    for hd in range(C_HEADS):
        r0 = 512 + hd * HEAD_DIM
        acc = acc + _dot(yc_ref[hd], w_ref[r0:r0 + HEAD_DIM, :])
    o_ref[...] = acc


def _outproj(x, ya, yb, yc, w_out, nb, t_len, tm):
    nt = t_len // tm
    rows = lambda w: pl.BlockSpec((tm, w), lambda b, i: (b * nt + i, 0))
    return pl.pallas_call(
        _outproj_kernel,
        grid=(nb, nt),
        in_specs=[rows(1024), rows(256), rows(256),
                  pl.BlockSpec((None, C_HEADS, tm, HEAD_DIM), lambda b, i: (b, 0, i, 0)),
                  pl.BlockSpec((1024, 1024), lambda b, i: (0, 0))],
        out_specs=rows(1024),
        out_shape=jax.ShapeDtypeStruct(x.shape, F32),
        compiler_params=_cparams(("parallel", "parallel")),
        name="outproj",
    )(x, ya, yb, yc, w_out)


def _top_vals_rows(x, k, rows_out):
    n, cols = x.shape
    ridx = lax.broadcasted_iota(jnp.int32, x.shape, 0).astype(F32)
    orow = lax.broadcasted_iota(jnp.int32, (rows_out, cols), 0)
    acc = jnp.zeros((rows_out, cols), F32)
    for j in range(k):
        m = jnp.max(x, axis=0, keepdims=True)
        idx = jnp.min(jnp.where(x == m, ridx, float(n)), axis=0, keepdims=True)
        x = jnp.where(ridx == idx, NEG_INF, x)
        acc = jnp.where(orow == j, m, acc)
    return acc


def _peer_route_kernel(x_ref, g_ref, wq_ref, qg_ref, sk_ref, h2_ref, e1_ref, e2_ref, th_ref):
    x = x_ref[...]
    hb = (x * lax.rsqrt(jnp.mean(x * x, axis=-1, keepdims=True) + 1e-6) * g_ref[...]).astype(BF16)
    h2_ref[...] = hb
    qp = jnp.dot(hb, wq_ref[...], preferred_element_type=F32)
    k = PEER_TOPK
    for hd in range(PEER_HEADS):
        st, top = [], []
        for c in range(2):
            qc = qp[:, (2 * hd + c) * N_KEYS:(2 * hd + c + 1) * N_KEYS]
            qn = qc * lax.rsqrt(jnp.mean(qc * qc, axis=-1, keepdims=True) + 1e-6) * qg_ref[:, c * 128:(c + 1) * 128]
            s = _dot_nt(sk_ref[hd, c], qn)
            st.append(s)
            top.append(_top_vals_rows(s, k + 1, 24))
        cand = jnp.concatenate([top[0][a:a + 1, :] + top[1][0:k, :] for a in range(k)], axis=0)
        best = _top_vals_rows(cand, k + 1, 24)
        nxt = jnp.maximum(best[k:k + 1, :], jnp.maximum(top[0][k:k + 1, :] + top[1][0:1, :],
                                                        top[0][0:1, :] + top[1][k:k + 1, :]))
        tau = 0.5 * (best[k - 1:k, :] + nxt)
        z = jnp.sum(jnp.where(cand >= tau, jnp.exp(cand - best[0:1, :]), 0.0), axis=0, keepdims=True)
        m2 = top[1][0:1, :]
        e1_ref[hd] = jnp.exp(st[0] - top[0][0:1, :]) / z
        e2_ref[hd] = jnp.exp(st[1] - m2)
        th_ref[hd] = jnp.exp((tau - m2) - st[0])


def _peer_dense_kernel(x_ref, h2_ref, u_ref, vt_ref, e1_ref, e2_ref, th_ref, o_ref, acc_ref, act_ref, w_ref, *, ea):
    j = pl.program_id(1)
    tm = h2_ref.shape[0]

    @pl.when(j == 0)
    def _():
        acc_ref[...] = jnp.zeros(acc_ref.shape, F32)

    h2 = h2_ref[...]
    for aa in range(ea):
        act_ref[aa * N_KEYS:(aa + 1) * N_KEYS, :] = jax.nn.gelu(
            _dot_nt(u_ref[aa * N_KEYS:(aa + 1) * N_KEYS, :], h2))
    hb = N_KEYS // 2
    th_rows = [[th_ref[hd, pl.ds(j * ea + aa, 1), :] for aa in range(ea)] for hd in range(PEER_HEADS)]
    e1_rows = [[e1_ref[hd, pl.ds(j * ea + aa, 1), :] for aa in range(ea)] for hd in range(PEER_HEADS)]
    for ts in range(tm // 128):
        tsl = slice(ts * 128, (ts + 1) * 128)
        for bh in range(2):
            bsl = slice(bh * hb, (bh + 1) * hb)
            gates = [jnp.zeros((hb, 128), F32) for _ in range(ea)]
            for hd in range(PEER_HEADS):
                e2 = e2_ref[hd, bsl, tsl]
                for aa in range(ea):
                    hit = e2 >= th_rows[hd][aa][:, tsl]
                    gates[aa] = gates[aa] + jnp.where(hit, e2, 0.0) * e1_rows[hd][aa][:, tsl]
            for aa in range(ea):
                rsl = slice(aa * N_KEYS + bh * hb, aa * N_KEYS + (bh + 1) * hb)
                w_ref[rsl, tsl] = (act_ref[rsl, tsl] * gates[aa]).astype(BF16)
    acc_ref[...] += jnp.dot(vt_ref[...], w_ref[...], preferred_element_type=F32)

    @pl.when(j == pl.num_programs(1) - 1)
    def _():
        o_ref[...] = x_ref[...] + acc_ref[...].T


PEER_EA = 4


def _peer(x1, p, tm):
    n = x1.shape[0]
    nt = n // tm
    full = lambda shape: pl.BlockSpec(shape, lambda i: (0,) * len(shape))
    tk = pl.BlockSpec((PEER_HEADS, N_KEYS, tm), lambda i: (0, 0, i))
    sd = jax.ShapeDtypeStruct((PEER_HEADS, N_KEYS, n), F32)
    h2, e1, e2, th = pl.pallas_call(
        _peer_route_kernel,
        grid=(nt,),
        in_specs=[pl.BlockSpec((tm, 1024), lambda i: (i, 0)), full((1, 1024)), full((1024, 2048)), full((1, 256)),
                  full((PEER_HEADS, 2, N_KEYS, N_KEYS))],
        out_specs=[pl.BlockSpec((tm, 1024), lambda i: (i, 0)), tk, tk, tk],
        out_shape=[jax.ShapeDtypeStruct((n, 1024), BF16), sd, sd, sd],
        compiler_params=_cparams(("parallel",)),
        name="peer_route",
    )(x1, p["ln2_g"], p["p_wq"], p["p_qnorm_g"], p["p_subkeys"])
    et = PEER_EA * N_KEYS
    n_exp = p["p_u"].shape[0]
    tk2 = pl.BlockSpec((PEER_HEADS, N_KEYS, tm), lambda i, j: (0, 0, i))
    return pl.pallas_call(
        functools.partial(_peer_dense_kernel, ea=PEER_EA),
        grid=(nt, n_exp // et),
        in_specs=[pl.BlockSpec((tm, 1024), lambda i, j: (i, 0)), pl.BlockSpec((tm, 1024), lambda i, j: (i, 0)),
                  pl.BlockSpec((et, 1024), lambda i, j: (j, 0)), pl.BlockSpec((1024, et), lambda i, j: (0, j)),
                  tk2, tk2, tk2],
        out_specs=pl.BlockSpec((tm, 1024), lambda i, j: (i, 0)),
        out_shape=jax.ShapeDtypeStruct((n, 1024), F32),
        scratch_shapes=[pltpu.VMEM((1024, tm), F32), pltpu.VMEM((et, tm), F32), pltpu.VMEM((et, tm), BF16)],
        compiler_params=_cparams(("parallel", "arbitrary")),
        name="peer_dense",
    )(x1, h2, p["p_u"], p["p_vt"], e1, e2, th)


def _prep_layer(l, ln1_g, ln2_g, w_in, w_out, a_vnorm_g, a_ws, a_bs, b_mu, b_w0, b_w2, b_a0, b_a2, b_g2, b_kk,
                b_ka, b_rk, b_lnx_g, b_lnx_b, c_qnorm_g, c_knorm_g, c_cmp_w1, c_cmp_b1, c_cmp_w2, p_wq,
                p_qnorm_g, p_subkeys, p_u, p_v, dec_seq):
    p = {}
    p["ln1_g"] = ln1_g[l][None, :]
    p["ln2_g"] = ln2_g[l][None, :]
    w = w_in[l]
    z = lambda c: jnp.zeros((w.shape[0], c), w.dtype)
    p["w_in"] = jnp.concatenate([w[:, :1472], z(64), w[:, 1472:], z(104)], axis=1).astype(BF16)
    p["a_vnorm_g"] = a_vnorm_g[l][None, :]
    wm = jnp.where(jnp.tril(jnp.ones((CHUNK, CHUNK), bool)), a_ws[l], 0.0)
    p["wm_prompt"] = wm.astype(BF16)
    p["brow_prompt"] = jnp.repeat(a_bs[l].T, HEAD_DIM, axis=1)
    reps = CHUNK // dec_seq
    eye = jnp.eye(reps, dtype=F32)
    p["wm_sample"] = jnp.stack([jnp.kron(eye, wm[g, :dec_seq, :dec_seq]) for g in range(4)]).astype(BF16)
    p["brow_sample"] = jnp.tile(p["brow_prompt"][:dec_seq], (reps, 1))
    p["c_qnorm_g"] = jnp.tile(c_qnorm_g[l], C_HEADS)[None, :]
    p["c_knorm_g12"] = jnp.stack([jnp.tile(c_knorm_g[l, 1], 2), jnp.tile(c_knorm_g[l, 2], 2)])
    p["bd512"] = _block_ones(512, HEAD_DIM)
    p["bd256"] = _block_ones(256, HEAD_DIM)
    p["bd128"] = _block_ones(128, HEAD_DIM)
    eye2 = jnp.eye(2, dtype=F32)
    w1h = c_cmp_w1[l].reshape(2, 2, CMP_STRIDE, HEAD_DIM, HEAD_DIM)
    p["c_w1e"] = jnp.einsum("kjsdc,kK,hH->jsKHdkhc", w1h, eye2, eye2).reshape(2, 4096, 256).astype(BF16)
    p["c_b1e"] = jnp.broadcast_to(c_cmp_b1[l][:, None, :], (2, 2, HEAD_DIM)).reshape(1, 256)
    p["c_w2e"] = jnp.einsum("kcd,kK,hH->khcKHd", c_cmp_w2[l], eye2, eye2).reshape(256, 256).astype(BF16)
    p["c_kg0"] = jnp.tile(c_knorm_g[l, 0], 2)[None, :]
    row = lambda a: a.reshape(1, -1)
    p["b_mu"] = row(b_mu[l])
    p["b_w0"] = row(b_w0[l])
    p["b_a0"] = row(b_a0[l])
    zz = jnp.zeros((64, 256), F32)
    p["b_w2a"] = jnp.concatenate([jnp.concatenate([b_w2[l], zz], axis=1),
                                  jnp.concatenate([zz, b_a2[l]], axis=1)], axis=0)
    p["b_g2"] = b_g2[l]
    p["b_kk"] = row(b_kk[l])
    p["b_ka"] = row(b_ka[l])
    p["b_rk"] = row(b_rk[l])
    p["b_lnx_g"] = row(b_lnx_g[l])
    p["b_lnx_b"] = row(b_lnx_b[l])
    p["w_out"] = w_out[l].astype(BF16)
    p["p_wq"] = p_wq[l].astype(BF16)
    p["p_qnorm_g"] = row(p_qnorm_g[l])
    p["p_subkeys"] = p_subkeys[l].astype(BF16)
    p["p_u"] = p_u[l].astype(BF16)
    p["p_vt"] = p_v[l].T.astype(BF16)
    return p


def _st_in(wkv):
    b = wkv.shape[0]
    return wkv.transpose(0, 3, 1, 2).reshape(b, HEAD_DIM, 256)


def _st_out(st):
    b = st.shape[0]
    return st.reshape(b, HEAD_DIM, B_HEADS, HEAD_DIM).transpose(0, 2, 3, 1)


def kernel(x_prompt, x_sample, cache_cmp_kv, cache_slc_kv, cache_win_kv, state_wkv, state_shift, page_table, ln1_g, ln2_g, w_in, w_out, a_vnorm_g, a_ws, a_bs, b_mu, b_w0, b_w2, b_a0, b_a2, b_g2, b_kk, b_ka, b_rk, b_lnx_g, b_lnx_b, c_qnorm_g, c_knorm_g, c_cmp_w1, c_cmp_b1, c_cmp_w2, p_wq, p_qnorm_g, p_subkeys, p_u, p_v):
    weights = (ln1_g, ln2_g, w_in, w_out, a_vnorm_g, a_ws, a_bs, b_mu, b_w0, b_w2, b_a0, b_a2, b_g2, b_kk, b_ka,
               b_rk, b_lnx_g, b_lnx_b, c_qnorm_g, c_knorm_g, c_cmp_w1, c_cmp_b1, c_cmp_w2, p_wq, p_qnorm_g,
               p_subkeys, p_u, p_v)
    nb, t_len, d_model = x_prompt.shape
    ns, dec_seq, _ = x_sample.shape
    depth = ln1_g.shape[0]
    n_pool = cache_cmp_kv.shape[1]
    n_s = ns * dec_seq
    n_sp = -(-n_s // 128) * 128
    rwkv_c = 64
    dec_c = 8
    pad_rows = lambda a, n: jnp.pad(a, ((0, n - a.shape[0]),) + ((0, 0),) * (a.ndim - 1))
    xp = x_prompt.reshape(nb * t_len, d_model)
    xs = pad_rows(x_sample.reshape(n_s, d_model), n_sp)
    kv5 = lambda a, b, t: a.reshape(b, t, 2, C_KV_HEADS, HEAD_DIM)
    cache_slc_t = cache_slc_kv.transpose(0, 1, 3, 4, 5, 2)
    cache_win_t = cache_win_kv.transpose(0, 1, 3, 4, 5, 2)
    outs_p, outs_s = [], []
    for l in range(depth):
        p = _prep_layer(l, *weights, dec_seq=dec_seq)
        ya, _, bproj, q, cmp, slc, win, gate, kvb = _inproj(xp, p, p["wm_prompt"], p["brow_prompt"], 256)
        bp3 = bproj.reshape(nb, t_len, B_PROJ)
        prev = jnp.concatenate([jnp.zeros((nb, 1, B_PROJ), F32), bp3[:, :-1]], axis=1).reshape(nb * t_len, B_PROJ)
        yb, st = _rwkv(bproj, prev, jnp.zeros((nb, HEAD_DIM, 256), F32), p, nb, t_len // rwkv_c, rwkv_c, rwkv_c)
        kcvc = _compress(cmp, p, nb)
        yc = _nsa_prompt(q, gate, kcvc, kvb, nb, t_len)
        x1 = _outproj(xp, ya, yb, yc, p["w_out"], nb, t_len, 512)
        xp = _peer(x1, p, 512)
        n_win = min(WINDOW, t_len)
        outs_p.append((kv5(cmp, nb, t_len), kv5(slc, nb, t_len), kv5(win, nb, t_len)[:, t_len - n_win:],
                       _st_out(st), bp3[:, -1]))
        sya, sv, sbproj, sq, scmp, sslc, swin, sgate, _ = _inproj(xs, p, p["wm_sample"], p["brow_sample"], 128)
        sb3 = sbproj[:n_s].reshape(ns, dec_seq, B_PROJ)
        sprev = jnp.concatenate([state_shift[l][:, None], sb3[:, :-1]], axis=1)
        pad_c = lambda a: jnp.pad(a, ((0, 0), (0, dec_c - dec_seq), (0, 0))).reshape(ns * dec_c, B_PROJ)
        syb, sst = _rwkv(pad_c(sb3), pad_c(sprev), _st_in(state_wkv[l]), p, ns, 1, dec_c, dec_seq)
        syb = pad_rows(syb.reshape(ns, dec_c, 256)[:, :dec_seq].reshape(n_s, 256), n_sp)
        skc = _compress_paged(cache_cmp_kv[l].reshape(n_pool, PAGE, 256), page_table, p)
        q16 = (sq[:n_s].reshape(ns, dec_seq, C_KV_HEADS, C_GROUP, HEAD_DIM).transpose(0, 2, 3, 1, 4)
               .reshape(ns, C_KV_HEADS, C_GROUP * dec_seq, HEAD_DIM))
        g16 = (sgate[:n_s, :3 * C_HEADS].reshape(ns, dec_seq, C_KV_HEADS, C_GROUP, 3).transpose(0, 2, 3, 1, 4)
               .reshape(ns, C_KV_HEADS, C_GROUP * dec_seq, 3))
        new_rows = jnp.concatenate([sslc[:n_s], swin[:n_s]], axis=1).reshape(ns, dec_seq, 512)
        new_rows = jnp.pad(new_rows, ((0, 0), (0, 128 - dec_seq), (0, 0)))
        win_buf = cache_win_kv[l].reshape(ns, -1, 256)
        so = _nsa_sample(page_table, q16, g16, skc, cache_slc_t, new_rows, cache_win_t, l, dec_seq)
        syc = (so.reshape(ns, C_KV_HEADS, C_GROUP, dec_seq, HEAD_DIM).transpose(1, 2, 0, 3, 4)
               .reshape(C_HEADS, n_s, HEAD_DIM))
        syc = jnp.pad(syc, ((0, 0), (0, n_sp - n_s), (0, 0)))[None]
        sx1 = _outproj(xs, sya, syb, syc, p["w_out"], 1, n_sp, 128)
        xs = _peer(sx1, p, 128)
        swin3 = swin[:n_s].reshape(ns, dec_seq, 256)
        win_new = jnp.concatenate([win_buf, swin3], axis=1)[:, dec_seq:]
        outs_s.append((kv5(scmp[:n_s], ns, dec_seq), kv5(sslc[:n_s], ns, dec_seq),
                       kv5(win_new, ns, win_buf.shape[1]), _st_out(sst), sb3[:, -1],
                       sv[:n_s].reshape(ns, dec_seq, A_WIDTH)))
    stk = lambda lst, i: jnp.stack([s[i] for s in lst], axis=0)
    return (xp.reshape(nb, t_len, d_model), xs[:n_s].reshape(ns, dec_seq, d_model),
            stk(outs_p, 0), stk(outs_p, 1), stk(outs_p, 2), stk(outs_p, 3), stk(outs_p, 4),
            stk(outs_s, 0), stk(outs_s, 1), stk(outs_s, 2), stk(outs_s, 3), stk(outs_s, 4), stk(outs_s, 5))
```

```python
import functools
import math

import jax
import jax.numpy as jnp
from jax import lax
from jax.experimental import pallas as pl
from jax.experimental.pallas import tpu as pltpu

F32 = jnp.float32
BF16 = jnp.bfloat16
HIGHEST = lax.Precision.HIGHEST

HEAD_DIM = 64
CHUNK = 128
A_WIDTH = 256
B_WIDTH = 256
B_HEADS = 4
B_PROJ = 960
C_HEADS = 8
C_KV_HEADS = 2
C_GROUP = 4
C_WIDTH = 512
CMP_STRIDE = 16
SLC_BLOCK = 64
SLC_TOPK = 16
WINDOW = 512
Q_BLOCK = 128
PAGE = 128
PEER_HEADS = 8
N_KEYS = 128
PEER_TOPK = 16
RWKV_GN_EPS = 64e-5
NEG_BIG = -1e30
POS_BIG = 1e30
NEG_INF = float("-inf")
VMEM_LIMIT = 56 * 1024 * 1024


def _cparams(sem):
    return pltpu.CompilerParams(dimension_semantics=sem, vmem_limit_bytes=VMEM_LIMIT)


def _dot(a, b):
    return jnp.dot(a.astype(BF16), b.astype(BF16), preferred_element_type=F32)


def _dot_nt(a, b):
    return lax.dot_general(a.astype(BF16), b.astype(BF16), (((1,), (1,)), ((), ())),
                           preferred_element_type=F32)


def _dot2(x, m):
    hi = x.astype(BF16)
    lo = (x - hi.astype(F32)).astype(BF16)
    return (jnp.dot(hi, m, preferred_element_type=F32) + jnp.dot(lo, m, preferred_element_type=F32))


def _dot3(a, b, dims):
    a_hi = a.astype(BF16)
    b_hi = b.astype(BF16)
    a_lo = (a - a_hi.astype(F32)).astype(BF16)
    b_lo = (b - b_hi.astype(F32)).astype(BF16)
    dg = lambda x, y: lax.dot_general(x, y, (dims, ((), ())), preferred_element_type=F32)
    return dg(a_hi, b_hi) + (dg(a_hi, b_lo) + dg(a_lo, b_hi))


def _hdot(a, b):
    return _dot3(a, b, ((1,), (0,)))


def _hdot_nt(a, b):
    return _dot3(a, b, ((1,), (1,)))


def _hdot_tn(a, b):
    return _dot3(a, b, ((0,), (0,)))


def _block_ones(width, group):
    i = jnp.arange(width)
    return (i[:, None] // group == i[None, :] // group).astype(BF16)


IN_PAD = 2944


def _inproj_kernel(x_ref, g_ref, w_ref, avg_ref, wm_ref, brow_ref, qg_ref, kg_ref, bd_ref,
                   ya_ref, v_ref, b_ref, q_ref, cmp_ref, slc_ref, win_ref, gate_ref, kvb_ref, *, tm):
    x = x_ref[...]
    h = x * lax.rsqrt(jnp.mean(x * x, axis=-1, keepdims=True) + 1e-6) * g_ref[...]
    z = jnp.dot(h.astype(BF16), w_ref[...], preferred_element_type=F32)
    b_ref[...] = z[:, 512:512 + B_PROJ]
    u = jax.nn.gelu(z[:, 0:256])
    gv = jax.nn.gelu(z[:, 256:512])
    v = gv * lax.rsqrt(jnp.mean(gv * gv, axis=-1, keepdims=True) + 1e-6) * avg_ref[...]
    v_ref[...] = v
    lane_g = lax.broadcasted_iota(jnp.int32, (CHUNK, A_WIDTH), 1) // HEAD_DIM
    for c in range(tm // CHUNK):
        vc = v[c * CHUNK:(c + 1) * CHUNK].astype(BF16)
        mixed = brow_ref[...]
        for g in range(4):
            mg = jnp.dot(wm_ref[g], vc, preferred_element_type=F32)
            mixed = mixed + jnp.where(lane_g == g, mg, 0.0)
        ya_ref[c * CHUNK:(c + 1) * CHUNK, :] = u[c * CHUNK:(c + 1) * CHUNK] * mixed
    bd = bd_ref[...]
    q = z[:, 1536:2048]
    q_ref[...] = q * lax.rsqrt(_dot2(q * q, bd) * (1.0 / HEAD_DIM) + 1e-6) * qg_ref[...]
    bd128 = bd[0:128, 0:128]
    cmp_ref[...] = z[:, 2048:2304]
    for j, (o_ref, off) in enumerate(((slc_ref, 2304), (win_ref, 2560))):
        k = z[:, off:off + 128]
        kn = k * lax.rsqrt(_dot2(k * k, bd128) * (1.0 / HEAD_DIM) + 1e-6) * kg_ref[j:j + 1, :]
        vv = z[:, off + 128:off + 256]
        o_ref[:, 0:128] = kn
        o_ref[:, 128:256] = vv
        kvb_ref[:, j * 256:j * 256 + 128] = kn.astype(BF16)
        kvb_ref[:, j * 256 + 128:j * 256 + 256] = vv.astype(BF16)
    gate_ref[...] = jax.nn.sigmoid(z[:, 2816:2944])


def _inproj(x, p, wm, brow, tm):
    n = x.shape[0]
    full = lambda shape: pl.BlockSpec(shape, lambda i: (0,) * len(shape))
    rows = lambda w: pl.BlockSpec((tm, w), lambda i: (i, 0))
    outs = [(A_WIDTH, F32), (A_WIDTH, F32), (B_PROJ, F32), (C_WIDTH, F32), (256, F32), (256, F32), (256, F32),
            (128, F32), (512, BF16)]
    return pl.pallas_call(
        functools.partial(_inproj_kernel, tm=tm),
        grid=(n // tm,),
        in_specs=[rows(1024), full((1, 1024)), full((1024, IN_PAD)), full((1, A_WIDTH)), full((4, CHUNK, CHUNK)),
                  full((CHUNK, A_WIDTH)), full((1, C_WIDTH)), full((2, 128)), full((512, 512))],
        out_specs=[rows(w) for w, _ in outs],
        out_shape=[jax.ShapeDtypeStruct((n, w), d) for w, d in outs],
        compiler_params=_cparams(("parallel",)),
        name="inproj",
    )(x, p["ln1_g"], p["w_in"], p["a_vnorm_g"], wm, brow, p["c_qnorm_g"], p["c_knorm_g12"], p["bd512"])


def _rwkv_prep_kernel(x_ref, xp_ref, mu_ref, w0_ref, w2a_ref, a0_ref, g2_ref, kkp_ref, ka_ref, rk_ref, bd_ref,
                      tri_ref, p2_ref, y0_ref, p3_ref, z_ref, g_ref, bonus_ref, *, c, t_valid):
    x = x_ref[...]
    xs = x + (xp_ref[...] - x) * mu_ref[...]
    r = xs[:, 0:256]
    k = xs[:, 256:512]
    v = xs[:, 512:768]
    wa = xs[:, 768:896]
    lane = lax.broadcasted_iota(jnp.int32, wa.shape, 1)
    pre = _hdot(jnp.where(lane < 64, jnp.tanh(wa), wa), w2a_ref[...])
    y = -(w0_ref[...] + pre[:, 0:256])
    softplus = jnp.maximum(y, 0.0) + jnp.log1p(jnp.exp(-jnp.abs(y)))
    ew = jnp.exp(-softplus - 0.5)
    a = jax.nn.sigmoid(a0_ref[...] + pre[:, 256:512])
    g_ref[...] = _hdot(jax.nn.sigmoid(xs[:, 896:960]), g2_ref[...])
    bd = bd_ref[...]
    kkr = k * kkp_ref[...]
    kk = kkr / jnp.maximum(jnp.sqrt(_dot2(kkr * kkr, bd)), 1e-12)
    k2 = k * (1.0 + (a - 1.0) * ka_ref[...])
    bonus_ref[...] = _dot2(r * k2 * rk_ref[...], bd) * v
    if t_valid < c:
        live = lax.broadcasted_iota(jnp.int32, ew.shape, 0) < t_valid
        ew = jnp.where(live, ew, 0.0)
        kk = jnp.where(live, kk, 0.0)
        k2 = jnp.where(live, k2, 0.0)
        v = jnp.where(live, v, 0.0)
    cum = jnp.dot(tri_ref[...], -ew, preferred_element_type=F32, precision=HIGHEST)
    gam = jnp.exp(cum)
    ginv = jnp.exp(-cum)
    g_end = gam[c - 1:c, :]
    alpha_t = -kk * jnp.exp(cum + ew)
    bhat = kk * a * ginv
    khat = k2 * ginv
    rt = r * gam
    kbar = khat * g_end
    bbar = bhat * g_end
    row = lax.broadcasted_iota(jnp.int32, (c, c), 0)
    col = lax.broadcasted_iota(jnp.int32, (c, c), 1)
    eye_c = (row == col).astype(F32)
    r64 = lax.broadcasted_iota(jnp.int32, (HEAD_DIM, HEAD_DIM), 0)
    c64 = lax.broadcasted_iota(jnp.int32, (HEAD_DIM, HEAD_DIM), 1)
    hs = range(B_HEADS)
    sls = [slice(h * HEAD_DIM, (h + 1) * HEAD_DIM) for h in hs]
    al = [alpha_t[:, s] for s in sls]
    bh = [bhat[:, s] for s in sls]
    kh = [khat[:, s] for s in sls]
    rh = [rt[:, s] for s in sls]
    vh = [v[:, s] for s in sls]
    mb = [jnp.where(row > col, _hdot_nt(al[h], bh[h]), 0.0) for h in hs]
    mk = [jnp.where(row > col, _hdot_nt(al[h], kh[h]), 0.0) for h in hs]
    qk = [jnp.where(row >= col, _hdot_nt(rh[h], kh[h]), 0.0) for h in hs]
    qb = [jnp.where(row >= col, _hdot_nt(rh[h], bh[h]), 0.0) for h in hs]
    mkv = [_hdot(mk[h], vh[h]) for h in hs]
    tinv = [eye_c + mb[h] for h in hs]
    pw = mb
    for _ in range(int(math.log2(c)) - 1):
        pw = [_hdot(pw[h], pw[h]) for h in hs]
        tinv = [tinv[h] + _hdot(tinv[h], pw[h]) for h in hs]
    p1 = [_hdot(tinv[h], al[h]) for h in hs]
    u = [_hdot(tinv[h], mkv[h]) for h in hs]
    for h in hs:
        sl = sls[h]
        p2_ref[:, sl] = rh[h] + _hdot(qb[h], p1[h])
        y0_ref[:, sl] = _hdot(qk[h], vh[h]) + _hdot(qb[h], u[h])
        p3_ref[:, sl] = jnp.where(r64 == c64, g_end[:, sl], 0.0) + _hdot_tn(bbar[:, sl], p1[h])
        z_ref[:, sl] = _hdot_tn(kbar[:, sl], vh[h]) + _hdot_tn(bbar[:, sl], u[h])


def _rwkv_seq_kernel(st0_ref, p2_ref, y0_ref, p3_ref, z_ref, g_ref, bonus_ref, lg_ref, lb_ref,
                     y_ref, st_ref, st_scr):
    @pl.when(pl.program_id(1) == 0)
    def _():
        st_scr[...] = st0_ref[...]

    for h in range(B_HEADS):
        sl = slice(h * HEAD_DIM, (h + 1) * HEAD_DIM)
        st = st_scr[:, sl]
        y = _hdot(p2_ref[:, sl], st) + y0_ref[:, sl]
        st_scr[:, sl] = _hdot(p3_ref[:, sl], st) + z_ref[:, sl]
        mu = jnp.mean(y, axis=-1, keepdims=True)
        var = jnp.mean(jnp.square(y - mu), axis=-1, keepdims=True)
        yn = (y - mu) * lax.rsqrt(var + RWKV_GN_EPS) * lg_ref[:, sl] + lb_ref[:, sl]
        y_ref[:, sl] = (yn + bonus_ref[:, sl]) * g_ref[:, sl]
    st_ref[...] = st_scr[...]


def _rwkv(proj, prev, st0, p, nb, nch, c, t_valid):
    n = proj.shape[0]
    full = lambda shape: pl.BlockSpec(shape, lambda b, i: (0,) * len(shape))
    rows = lambda w: pl.BlockSpec((c, w), lambda b, i: (b * nch + i, 0))
    mats = pl.BlockSpec((HEAD_DIM, 256), lambda b, i: (b * nch + i, 0))
    tri = (jnp.arange(c)[:, None] >= jnp.arange(c)[None, :]).astype(F32)
    p2, y0, p3, z, g, bonus = pl.pallas_call(
        functools.partial(_rwkv_prep_kernel, c=c, t_valid=t_valid),
        grid=(nb, nch),
        in_specs=[rows(B_PROJ), rows(B_PROJ), full((1, B_PROJ)), full((1, 256)), full((128, 512)), full((1, 256)),
                  full((64, 256)), full((1, 256)), full((1, 256)), full((1, 256)), full((256, 256)), full((c, c))],
        out_specs=[rows(256), rows(256), mats, mats, rows(256), rows(256)],
        out_shape=[jax.ShapeDtypeStruct((n, 256), F32), jax.ShapeDtypeStruct((n, 256), F32),
                   jax.ShapeDtypeStruct((nb * nch * HEAD_DIM, 256), F32),
                   jax.ShapeDtypeStruct((nb * nch * HEAD_DIM, 256), F32),
                   jax.ShapeDtypeStruct((n, 256), F32), jax.ShapeDtypeStruct((n, 256), F32)],
        compiler_params=_cparams(("parallel", "parallel")),
        name="rwkv_prep",
    )(proj, prev, p["b_mu"], p["b_w0"], p["b_w2a"], p["b_a0"], p["b_g2"], p["b_kk"], p["b_ka"], p["b_rk"],
      p["bd256"], tri)
    st_spec = pl.BlockSpec((None, HEAD_DIM, 256), lambda b, i: (b, 0, 0))
    y, st = pl.pallas_call(
        _rwkv_seq_kernel,
        grid=(nb, nch),
        in_specs=[st_spec, rows(256), rows(256), mats, mats, rows(256), rows(256), full((1, 256)), full((1, 256))],
        out_specs=[rows(256), st_spec],
        out_shape=[jax.ShapeDtypeStruct((n, 256), F32), jax.ShapeDtypeStruct((nb, HEAD_DIM, 256), F32)],
        scratch_shapes=[pltpu.VMEM((HEAD_DIM, 256), F32)],
        compiler_params=_cparams(("parallel", "arbitrary")),
        name="rwkv_seq",
    )(st0, p2, y0, p3, z, g, bonus, p["b_lnx_g"], p["b_lnx_b"])
    return y, st


def _compress_tail(a0, a1, n_out, b1_ref, w2_ref, kg_ref, bd_ref, o_ref):
    n = a1.shape[0]
    hid = jax.nn.gelu(a0 + pltpu.roll(a1, n - 1, 0) + b1_ref[...])[0:n_out]
    out = _dot(hid, w2_ref[...])
    k = out[:, 0:128]
    kn = k * lax.rsqrt(_dot2(k * k, bd_ref[...]) * (1.0 / HEAD_DIM) + 1e-6) * kg_ref[...]
    o_ref[:, 0:128] = kn.astype(BF16)
    o_ref[:, 128:256] = out[:, 128:256].astype(BF16)


def _compress_kernel(x_ref, w1_ref, b1_ref, w2_ref, kg_ref, bd_ref, o_ref):
    x = x_ref[...].astype(BF16)
    a0 = jnp.dot(x, w1_ref[0], preferred_element_type=F32)
    a1 = jnp.dot(x, w1_ref[1], preferred_element_type=F32)
    _compress_tail(a0, a1, x.shape[0], b1_ref, w2_ref, kg_ref, bd_ref, o_ref)


def _compress_paged_kernel(pt_ref, *refs, pg):
    pages = refs[:pg + 1]
    w1_ref, b1_ref, w2_ref, kg_ref, bd_ref, o_ref, rows_scr = refs[pg + 1:]
    for j, r in enumerate(pages):
        rows = r[...].reshape(4 * HEAD_DIM, PAGE).T
        rows_scr[0, j * PAGE:(j + 1) * PAGE, :] = rows[:, 0:128]
        rows_scr[1, j * PAGE:(j + 1) * PAGE, :] = rows[:, 128:256]
    n_piece = (pg + 1) * (PAGE // CMP_STRIDE)
    a0 = jnp.zeros((n_piece, 256), F32)
    a1 = jnp.zeros((n_piece, 256), F32)
    for s in range(CMP_STRIDE):
        for hf in range(2):
            xs = rows_scr[hf, pl.ds(s, n_piece, stride=CMP_STRIDE), :].astype(BF16)
            w0 = s * 256 + hf * 128
            a0 = a0 + jnp.dot(xs, w1_ref[0, w0:w0 + 128, :], preferred_element_type=F32)
            a1 = a1 + jnp.dot(xs, w1_ref[1, w0:w0 + 128, :], preferred_element_type=F32)
    _compress_tail(a0, a1, pg * 8, b1_ref, w2_ref, kg_ref, bd_ref, o_ref)


def _compress_weight_specs(index):
    full = lambda shape: pl.BlockSpec(shape, index(len(shape)))
    return [full((2, 4096, 256)), full((1, 256)), full((256, 256)), full((1, 128)), full((128, 128))]


def _compress(cmp_rows, p, nb):
    n_piece = cmp_rows.shape[0] // nb // CMP_STRIDE
    x = cmp_rows.reshape(nb, n_piece, 4096)
    return pl.pallas_call(
        _compress_kernel,
        grid=(nb,),
        in_specs=[pl.BlockSpec((None, n_piece, 4096), lambda b: (b, 0, 0))]
        + _compress_weight_specs(lambda r: (lambda b: (0,) * r)),
        out_specs=pl.BlockSpec((None, n_piece, 256), lambda b: (b, 0, 0)),
        out_shape=jax.ShapeDtypeStruct((nb, n_piece, 256), BF16),
        compiler_params=_cparams(("parallel",)),
        name="compress",
    )(x, p["c_w1e"], p["c_b1e"], p["c_w2e"], p["c_kg0"], p["bd128"])


CMP_PAGES = 16


def _compress_paged(cache_t, page_table, p, layer):
    ns, n_pages = page_table.shape
    pg = min(CMP_PAGES, n_pages)

    def page_spec(j):
        return pl.BlockSpec((None, None, 2, C_KV_HEADS, HEAD_DIM, PAGE),
                            lambda s, i, pt: (layer, pt[s, jnp.minimum(i * pg + j, n_pages - 1)], 0, 0, 0, 0))

    return pl.pallas_call(
        functools.partial(_compress_paged_kernel, pg=pg),
        grid_spec=pltpu.PrefetchScalarGridSpec(
            num_scalar_prefetch=1,
            grid=(ns, n_pages // pg),
            in_specs=[page_spec(j) for j in range(pg + 1)]
            + _compress_weight_specs(lambda r: (lambda s, i, pt: (0,) * r)),
            out_specs=pl.BlockSpec((None, pg * 8, 256), lambda s, i, pt: (s, i, 0)),
            scratch_shapes=[pltpu.VMEM((2, (pg + 1) * PAGE, 128), F32)],
        ),
        out_shape=jax.ShapeDtypeStruct((ns, n_pages * 8, 256), BF16),
        compiler_params=_cparams(("parallel", "arbitrary")),
        name="compress_paged",
    )(page_table, *([cache_t] * (pg + 1)), p["c_w1e"], p["c_b1e"], p["c_w2e"], p["c_kg0"], p["bd128"])


def _alibi_slope(head):
    return 2.0 ** (-8.0 * (head + 1.0) / C_HEADS)


def _topk_mask(x, k):
    nl = x.shape[-1]
    lane = lax.broadcasted_iota(jnp.int32, x.shape, x.ndim - 1).astype(F32)
    sel = jnp.zeros(x.shape, F32)
    for _ in range(k):
        m = jnp.max(x, axis=-1, keepdims=True)
        idx = jnp.min(jnp.where(x == m, lane, float(nl)), axis=-1, keepdims=True)
        hit = lane == idx
        sel = jnp.where(hit, 1.0, sel)
        x = jnp.where(hit, NEG_INF, x)
    return sel


def _masked_softmax(s, valid):
    s = jnp.where(valid, s, NEG_BIG)
    e = jnp.exp(s - jnp.max(s, axis=-1, keepdims=True))
    return jnp.where(valid, e / jnp.sum(e, axis=-1, keepdims=True), 0.0)


def _rows4(x):
    return jnp.concatenate([x, x, x, x], axis=0)


def _flash_step(carry, s, ok, v, v_is_t=False):
    m, l, acc = carry
    if ok is not None:
        s = jnp.where(ok, s, NEG_BIG)
    m_new = jnp.maximum(m, jnp.max(s, axis=-1, keepdims=True))
    alpha = jnp.exp(m - m_new)
    pr = jnp.exp(s - m_new)
    if ok is not None:
        pr = jnp.where(ok, pr, 0.0)
    l = alpha * l + jnp.sum(pr, axis=-1, keepdims=True)
    pv = _dot_nt(pr, v) if v_is_t else jnp.dot(pr.astype(BF16), v, preferred_element_type=F32)
    return m_new, l, alpha * acc + pv


def _flash_init(rows):
    return (jnp.full((rows, 1), NEG_BIG, F32), jnp.zeros((rows, 1), F32), jnp.zeros((rows, HEAD_DIM), F32))


def _nsa_prompt_kernel(q_ref, gate_ref, kc_ref, kvb_ref, pool_ref, o_ref, *, n_slc):
    i = pl.program_id(1)
    qb = Q_BLOCK
    n_cmp = kc_ref.shape[0]
    nbl = pool_ref.shape[1]
    tok = lax.broadcasted_iota(jnp.int32, (qb, 1), 0) + i * qb
    t4 = _rows4(tok)
    lane128 = lax.broadcasted_iota(jnp.int32, (1, 128), 1)
    for h in range(C_KV_HEADS):
        heads = [C_GROUP * h + g for g in range(C_GROUP)]
        qh = (jnp.concatenate([q_ref[:, hd * HEAD_DIM:(hd + 1) * HEAD_DIM] for hd in heads], axis=0)
              * (HEAD_DIM ** -0.5)).astype(BF16)
        slope = jnp.concatenate([jnp.full((qb, 1), _alibi_slope(hd), F32) for hd in heads], axis=0)
        kcol = slice(h * HEAD_DIM, (h + 1) * HEAD_DIM)
        vcol = slice(128 + h * HEAD_DIM, 128 + (h + 1) * HEAD_DIM)
        c_dist = t4 - (lax.broadcasted_iota(jnp.int32, (1, n_cmp), 1) * CMP_STRIDE + (2 * CMP_STRIDE - 1))
        s_c = _dot_nt(qh, kc_ref[:, kcol]) - slope * c_dist.astype(F32)
        p_c = _masked_softmax(s_c, c_dist >= 0)
        o_c = jnp.dot(p_c.astype(BF16), kc_ref[:, vcol], preferred_element_type=F32)
        psum = p_c[0:qb] + p_c[qb:2 * qb] + p_c[2 * qb:3 * qb] + p_c[3 * qb:4 * qb]
        imp = _dot2(psum, pool_ref[...])
        blk = lax.broadcasted_iota(jnp.int32, (1, nbl), 1)
        imp = jnp.where(blk * SLC_BLOCK <= tok, imp, NEG_BIG)
        imp = jnp.where((blk == 0) | (blk == tok // SLC_BLOCK), POS_BIG, imp)
        imp = jnp.where(blk < n_slc, imp, NEG_INF)
        sel_f = _topk_mask(imp, min(SLC_TOPK, n_slc))
        chunk_used = jnp.max(sel_f, axis=0, keepdims=True)
        sel_t = sel_f.T.astype(BF16)
        nq4 = C_GROUP * qb
        slopes = [_alibi_slope(hd) for hd in heads]
        q_t = (jnp.concatenate([q_ref[:, hd * HEAD_DIM:(hd + 1) * HEAD_DIM] for hd in heads], axis=0)
               * (HEAD_DIM ** -0.5)).T.astype(BF16)
        lane4 = lax.broadcasted_iota(jnp.int32, (1, nq4), 1)
        slope_row = jnp.full((1, nq4), slopes[0], F32)
        for g in range(1, C_GROUP):
            slope_row = jnp.where(lane4 // qb == g, slopes[g], slope_row)
        rel_t = (lax.broadcasted_iota(jnp.int32, (128, nq4), 0)
                 - lax.broadcasted_iota(jnp.int32, (128, nq4), 1) % qb).astype(F32)
        bias_t = slope_row * rel_t
        exp_row = lax.broadcasted_iota(jnp.int32, (128, nbl), 0) // SLC_BLOCK
        exp_col = lax.broadcasted_iota(jnp.int32, (128, nbl), 1)

        def tile_step(carry, c, kc, vc, ok):
            m, l, acc = carry
            s = (jnp.dot(kc, q_t, preferred_element_type=F32) + bias_t
                 + slope_row * ((c - i) * 128).astype(F32))
            if ok is not None:
                s = jnp.where(ok, s, NEG_BIG)
            m_new = jnp.maximum(m, jnp.max(s, axis=0, keepdims=True))
            alpha = jnp.exp(m - m_new)
            pr = jnp.exp(s - m_new)
            if ok is not None:
                pr = jnp.where(ok, pr, 0.0)
            l = alpha * l + jnp.sum(pr, axis=0, keepdims=True)
            pv = lax.dot_general(vc, pr.astype(BF16), (((0,), (0,)), ((), ())), preferred_element_type=F32)
            return m_new, l, alpha * acc + pv

        def sel_mask(c):
            expand = (exp_col == 2 * c + exp_row).astype(BF16)
            m1 = jnp.dot(expand, sel_t, preferred_element_type=F32) > 0.5
            return jnp.concatenate([m1] * C_GROUP, axis=1)

        def sel_body(c, carry):
            used = jnp.max(jnp.where(blk // 2 == c, chunk_used, 0.0))

            def visit(carry):
                r0 = pl.multiple_of(c * 128, 128)
                return tile_step(carry, c, kvb_ref[pl.ds(r0, 128), kcol], kvb_ref[pl.ds(r0, 128), vcol], sel_mask(c))

            return lax.cond(used > 0.0, visit, lambda carry: carry, carry)

        init = (jnp.full((1, nq4), NEG_BIG, F32), jnp.zeros((1, nq4), F32), jnp.zeros((HEAD_DIM, nq4), F32))
        carry = lax.fori_loop(0, i, sel_body, init)
        d0 = pl.multiple_of(i * 128, 128)
        causal = rel_t <= 0.0
        _, l_s, acc_s = tile_step(carry, i, kvb_ref[pl.ds(d0, 128), kcol], kvb_ref[pl.ds(d0, 128), vcol],
                                  sel_mask(i) & causal)

        wk = slice(256 + h * HEAD_DIM, 256 + (h + 1) * HEAD_DIM)
        wv = slice(384 + h * HEAD_DIM, 384 + (h + 1) * HEAD_DIM)
        n_wc = WINDOW // 128

        def win_edge(carry):
            r0 = pl.multiple_of((i - n_wc) * 128, 128)
            return tile_step(carry, i - n_wc, kvb_ref[pl.ds(r0, 128), wk], kvb_ref[pl.ds(r0, 128), wv], rel_t > 0.0)

        carry = lax.cond(i >= n_wc, win_edge, lambda carry: carry, init)

        def win_body(c, carry):
            r0 = pl.multiple_of(c * 128, 128)
            return tile_step(carry, c, kvb_ref[pl.ds(r0, 128), wk], kvb_ref[pl.ds(r0, 128), wv], None)

        carry = lax.fori_loop(jnp.maximum(i - n_wc + 1, 0), i, win_body, carry)
        _, l_w, acc_w = tile_step(carry, i, kvb_ref[pl.ds(d0, 128), wk], kvb_ref[pl.ds(d0, 128), wv], causal)
        o_s = (acc_s / l_s).T
        o_w = (acc_w / l_w).T
        for g, hd in enumerate(heads):
            rs = slice(g * qb, (g + 1) * qb)
            gt = gate_ref[:, 3 * hd:3 * hd + 3]
            o_ref[hd] = gt[:, 0:1] * o_c[rs] + gt[:, 1:2] * o_s[rs] + gt[:, 2:3] * o_w[rs]


def _topk_mask_rows(x, k):
    n = x.shape[0]
    ridx = lax.broadcasted_iota(jnp.int32, x.shape, 0).astype(F32)
    sel = jnp.zeros(x.shape, F32)
    for _ in range(k):
        m = jnp.max(x, axis=0, keepdims=True)
        idx = jnp.min(jnp.where(x == m, ridx, float(n)), axis=0, keepdims=True)
        hit = ridx == idx
        sel = jnp.where(hit, 1.0, sel)
        x = jnp.where(hit, NEG_INF, x)
    return sel


def _nsa_prompt_kernel_t(q_ref, gate_ref, kc_ref, kvb_ref, pool_ref, o_ref, *, n_slc):
    i = pl.program_id(1)
    qb = Q_BLOCK
    nq4 = C_GROUP * qb
    n_cmp = kc_ref.shape[0]
    nbl = pool_ref.shape[0]
    lane4 = lax.broadcasted_iota(jnp.int32, (1, nq4), 1)
    tq_row = i * qb + lane4 % qb
    tok_row = i * qb + lax.broadcasted_iota(jnp.int32, (1, qb), 1)
    gate_t = gate_ref[...].T
    blk_col = lax.broadcasted_iota(jnp.int32, (nbl, 1), 0)
    rel2 = (lax.broadcasted_iota(jnp.int32, (256, nq4), 0) - lane4 % qb).astype(F32)
    exp_row = lax.broadcasted_iota(jnp.int32, (256, nbl), 0) // SLC_BLOCK
    exp_col = lax.broadcasted_iota(jnp.int32, (256, nbl), 1)
    half = i // 2
    for h in range(C_KV_HEADS):
        heads = [C_GROUP * h + g for g in range(C_GROUP)]
        q_t = (jnp.concatenate([q_ref[:, hd * HEAD_DIM:(hd + 1) * HEAD_DIM] for hd in heads], axis=0)
               * (HEAD_DIM ** -0.5)).T.astype(BF16)
        slope_row = jnp.full((1, nq4), _alibi_slope(heads[0]), F32)
        for g in range(1, C_GROUP):
            slope_row = jnp.where(lane4 // qb == g, _alibi_slope(heads[g]), slope_row)
        kcol = slice(h * HEAD_DIM, (h + 1) * HEAD_DIM)
        vcol = slice(128 + h * HEAD_DIM, 128 + (h + 1) * HEAD_DIM)
        c_dist = tq_row - (lax.broadcasted_iota(jnp.int32, (n_cmp, 1), 0) * CMP_STRIDE + (2 * CMP_STRIDE - 1))
        valid = c_dist >= 0
        s_c = jnp.dot(kc_ref[:, kcol], q_t, preferred_element_type=F32) - slope_row * c_dist.astype(F32)
        s_c = jnp.where(valid, s_c, NEG_BIG)
        e_c = jnp.exp(s_c - jnp.max(s_c, axis=0, keepdims=True))
        p_c = jnp.where(valid, e_c / jnp.sum(e_c, axis=0, keepdims=True), 0.0)
        o_c = lax.dot_general(kc_ref[:, vcol], p_c.astype(BF16), (((0,), (0,)), ((), ())),
                              preferred_element_type=F32)
        psum = p_c[:, 0:qb] + p_c[:, qb:2 * qb] + p_c[:, 2 * qb:3 * qb] + p_c[:, 3 * qb:4 * qb]
        ps_hi = psum.astype(BF16)
        ps_lo = (psum - ps_hi.astype(F32)).astype(BF16)
        imp = (jnp.dot(pool_ref[...], ps_hi, preferred_element_type=F32)
               + jnp.dot(pool_ref[...], ps_lo, preferred_element_type=F32))
        imp = jnp.where(blk_col * SLC_BLOCK <= tok_row, imp, NEG_BIG)
        imp = jnp.where((blk_col == 0) | (blk_col == tok_row // SLC_BLOCK), POS_BIG, imp)
        imp = jnp.where(blk_col < n_slc, imp, NEG_INF)
        sel_f = _topk_mask_rows(imp, min(SLC_TOPK, n_slc))
        used_col = jnp.max(sel_f, axis=1, keepdims=True)
        sel_t = sel_f.astype(BF16)
        bias2 = slope_row * rel2

        def tile_step(carry, p, kcols, vcols, mask_fn):
            m, l, acc = carry
            r0 = pl.multiple_of(p * 256, 256)
            off = ((2 * p - i) * qb).astype(F32)
            s = jnp.dot(kvb_ref[pl.ds(r0, 256), kcols], q_t, preferred_element_type=F32) + bias2 + slope_row * off
            ok = mask_fn(p, off)
            if ok is not None:
                s = jnp.where(ok, s, NEG_BIG)
            m_new = jnp.maximum(m, jnp.max(s, axis=0, keepdims=True))
            alpha = jnp.exp(m - m_new)
            pr = jnp.exp(s - m_new)
            if ok is not None:
                pr = jnp.where(ok, pr, 0.0)
            l = alpha * l + jnp.sum(pr, axis=0, keepdims=True)
            pv = lax.dot_general(kvb_ref[pl.ds(r0, 256), vcols], pr.astype(BF16), (((0,), (0,)), ((), ())),
                                 preferred_element_type=F32)
            return m_new, l, alpha * acc + pv

        def sel_mask(p, off):
            expand = (exp_col == 4 * p + exp_row).astype(BF16)
            m1 = jnp.dot(expand, sel_t, preferred_element_type=F32) > 0.5
            return jnp.concatenate([m1] * C_GROUP, axis=1)

        def sel_causal_mask(p, off):
            return sel_mask(p, off) & (rel2 + off <= 0.0)

        def win_mask(p, off):
            d = rel2 + off
            return (d <= 0.0) & (d > -float(WINDOW))

        def sel_body(p, carry):
            used = jnp.max(jnp.where(blk_col // 4 == p, used_col, 0.0))
            return lax.cond(used > 0.0, lambda cr: tile_step(cr, p, kcol, vcol, sel_mask), lambda cr: cr, carry)

        init = (jnp.full((1, nq4), NEG_BIG, F32), jnp.zeros((1, nq4), F32), jnp.zeros((HEAD_DIM, nq4), F32))
        carry = lax.fori_loop(0, half, sel_body, init)
        _, l_s, acc_s = tile_step(carry, half, kcol, vcol, sel_causal_mask)

        wk = slice(256 + h * HEAD_DIM, 256 + (h + 1) * HEAD_DIM)
        wv = slice(384 + h * HEAD_DIM, 384 + (h + 1) * HEAD_DIM)
        n_wp = WINDOW // 256
        _, l_w, acc_w = lax.fori_loop(jnp.maximum(half - n_wp, 0), half + 1,
                                      lambda p, cr: tile_step(cr, p, wk, wv, win_mask), init)

        def gate_row(j):
            return jnp.concatenate([gate_t[3 * hd + j:3 * hd + j + 1, :] for hd in heads], axis=1)

        o_t = gate_row(0) * o_c + gate_row(1) * (acc_s / l_s) + gate_row(2) * (acc_w / l_w)
        o = o_t.T
        for g, hd in enumerate(heads):
            o_ref[hd] = o[g * qb:(g + 1) * qb]


def _nsa_prompt(q, gate, kcvc, kvb, nb, t_len):
    assert t_len % 256 == 0
    nq = t_len // Q_BLOCK
    n_cmp = kcvc.shape[1]
    n_slc = -(-t_len // SLC_BLOCK)
    nbl = -(-n_slc // 128) * 128
    pool = (jnp.arange(nbl)[:, None] == jnp.arange(n_cmp)[None, :] // (SLC_BLOCK // CMP_STRIDE)).astype(BF16)
    return pl.pallas_call(
        functools.partial(_nsa_prompt_kernel_t, n_slc=n_slc),
        grid=(nb, nq),
        in_specs=[pl.BlockSpec((Q_BLOCK, C_WIDTH), lambda b, i: (b * nq + i, 0)),
                  pl.BlockSpec((Q_BLOCK, 128), lambda b, i: (b * nq + i, 0)),
                  pl.BlockSpec((None, n_cmp, 256), lambda b, i: (b, 0, 0)),
                  pl.BlockSpec((t_len, 512), lambda b, i: (b, 0)),
                  pl.BlockSpec((nbl, n_cmp), lambda b, i: (0, 0))],
        out_specs=pl.BlockSpec((None, C_HEADS, Q_BLOCK, HEAD_DIM), lambda b, i: (b, 0, i, 0)),
        out_shape=jax.ShapeDtypeStruct((nb, C_HEADS, t_len, HEAD_DIM), F32),
        compiler_params=_cparams(("parallel", "arbitrary")),
        name="nsa_prompt",
    )(q, gate, kcvc, kvb, pool)


SLC_PAGES = 16


def _nsa_sample_kernel(pt_ref, *refs, pg, dec_seq, past_len, n_buf):
    pages = refs[:pg]
    (q_ref, gate_ref, kc_ref, new_ref, buf_ref, pool_ref, o_ref,
     sel_scr, ocw_scr, m_scr, l_scr, acc_scr) = refs[pg:]
    i = pl.program_id(1)
    n_parts = pl.num_programs(1)
    rows = C_GROUP * dec_seq
    n_cmp = kc_ref.shape[0]
    n_blk = pool_ref.shape[1]
    ridx = lax.broadcasted_iota(jnp.int32, (rows, 1), 0)
    tq = ridx % dec_seq
    t_abs = past_len + tq
    lane128 = lax.broadcasted_iota(jnp.int32, (1, 128), 1)

    def slope_of(h):
        s = jnp.full((rows, 1), _alibi_slope(C_GROUP * h), F32)
        for g in range(1, C_GROUP):
            s = jnp.where(ridx // dec_seq == g, _alibi_slope(C_GROUP * h + g), s)
        return s

    @pl.when(i == 0)
    def _():
        same_tok = (lax.broadcasted_iota(jnp.int32, (rows, rows), 0) % dec_seq
                    == lax.broadcasted_iota(jnp.int32, (rows, rows), 1) % dec_seq).astype(BF16)
        for h in range(C_KV_HEADS):
            slope = slope_of(h)
            qh = (q_ref[h] * (HEAD_DIM ** -0.5)).astype(BF16)
            kcol = slice(h * HEAD_DIM, (h + 1) * HEAD_DIM)
            vcol = slice(128 + h * HEAD_DIM, 128 + (h + 1) * HEAD_DIM)
            c_dist = t_abs - (lax.broadcasted_iota(jnp.int32, (1, n_cmp), 1) * CMP_STRIDE + (2 * CMP_STRIDE - 1))
            s_c = _dot_nt(qh, kc_ref[:, kcol]) - slope * c_dist.astype(F32)
            p_c = _masked_softmax(s_c, c_dist >= 0)
            o_c = jnp.dot(p_c.astype(BF16), kc_ref[:, vcol], preferred_element_type=F32)
            hi = p_c.astype(BF16)
            lo = (p_c - hi.astype(F32)).astype(BF16)
            psum = (jnp.dot(same_tok, hi, preferred_element_type=F32)
                    + jnp.dot(same_tok, lo, preferred_element_type=F32))
            imp = _dot2(psum, pool_ref[...])
            blk = lax.broadcasted_iota(jnp.int32, (1, n_blk), 1)
            imp = jnp.where(blk == 0, POS_BIG, imp)
            sel_scr[h] = _topk_mask(imp, min(SLC_TOPK, n_blk + 1) - 1)
            wk = slice(256 + h * HEAD_DIM, 256 + (h + 1) * HEAD_DIM)
            wv = slice(384 + h * HEAD_DIM, 384 + (h + 1) * HEAD_DIM)
            d_buf = (n_buf + tq) - lax.broadcasted_iota(jnp.int32, (1, n_buf), 1)
            s_b = _dot(qh, buf_ref[0, h]) - slope * d_buf.astype(F32)
            carry = _flash_step(_flash_init(rows), s_b, (d_buf >= 0) & (d_buf < WINDOW), buf_ref[1, h], True)
            d_new = tq - lane128
            s_n = _dot_nt(qh, new_ref[:, wk]) - slope * d_new.astype(F32)
            _, l_w, acc_w = _flash_step(carry, s_n, (d_new >= 0) & (lane128 < dec_seq),
                                        new_ref[:, wv].astype(BF16))
            gt = gate_ref[h]
            ocw_scr[h] = gt[:, 0:1] * o_c + gt[:, 2:3] * (acc_w / l_w)
            m0, l0, a0 = _flash_init(rows)
            m_scr[h] = m0
            l_scr[h] = l0
            acc_scr[h] = a0

    nk = pg * PAGE
    key_pos = i * nk + lax.broadcasted_iota(jnp.int32, (1, nk), 1)
    expand = (lax.broadcasted_iota(jnp.int32, (n_blk, nk), 0)
              == i * (nk // SLC_BLOCK) + lax.broadcasted_iota(jnp.int32, (n_blk, nk), 1) // SLC_BLOCK).astype(BF16)
    for h in range(C_KV_HEADS):
        slope = slope_of(h)
        qh = (q_ref[h] * (HEAD_DIM ** -0.5)).astype(BF16)
        k_all = jnp.concatenate([r[0, h] for r in pages], axis=1).astype(BF16)
        v_all = jnp.concatenate([r[1, h] for r in pages], axis=1).astype(BF16)
        dist = t_abs - key_pos
        selx = jnp.dot(sel_scr[h].astype(BF16), expand, preferred_element_type=F32)
        s = jnp.dot(qh, k_all, preferred_element_type=F32) - slope * dist.astype(F32)
        carry = _flash_step((m_scr[h], l_scr[h], acc_scr[h]), s, (dist >= 0) & (selx > 0.5), v_all, True)
        m_scr[h], l_scr[h], acc_scr[h] = carry

    @pl.when(i == n_parts - 1)
    def _():
        for h in range(C_KV_HEADS):
            slope = slope_of(h)
            qh = (q_ref[h] * (HEAD_DIM ** -0.5)).astype(BF16)
            kcol = slice(h * HEAD_DIM, (h + 1) * HEAD_DIM)
            vcol = slice(128 + h * HEAD_DIM, 128 + (h + 1) * HEAD_DIM)
            d_new = tq - lane128
            s_n = _dot_nt(qh, new_ref[:, kcol]) - slope * d_new.astype(F32)
            _, l_s, acc_s = _flash_step((m_scr[h], l_scr[h], acc_scr[h]), s_n,
                                        (d_new >= 0) & (lane128 < dec_seq), new_ref[:, vcol].astype(BF16))
            o_ref[h] = ocw_scr[h] + gate_ref[h][:, 1:2] * (acc_s / l_s)


def _nsa_sample(page_table, q16, gate16, kcvc, cache_slc_t, new_rows, win_buf_t, layer, dec_seq):
    ns, n_pages = page_table.shape
    pg = min(SLC_PAGES, n_pages)
    past_len = n_pages * PAGE
    n_buf = win_buf_t.shape[-1]
    n_cmp = kcvc.shape[1]
    n_blk = past_len // SLC_BLOCK
    rows = C_GROUP * dec_seq
    pool = (jnp.arange(n_cmp)[:, None] // (SLC_BLOCK // CMP_STRIDE) == jnp.arange(n_blk)[None, :]).astype(BF16)
    per_seq = lambda shape: pl.BlockSpec((None,) + shape, lambda s, i, pt: (s,) + (0,) * len(shape))

    kv_tile = (2, C_KV_HEADS, HEAD_DIM)

    def page_spec(j):
        return pl.BlockSpec((None, None) + kv_tile + (PAGE,),
                            lambda s, i, pt: (layer, pt[s, i * pg + j], 0, 0, 0, 0))

    return pl.pallas_call(
        functools.partial(_nsa_sample_kernel, pg=pg, dec_seq=dec_seq, past_len=past_len, n_buf=n_buf),
        grid_spec=pltpu.PrefetchScalarGridSpec(
            num_scalar_prefetch=1,
            grid=(ns, n_pages // pg),
            in_specs=[page_spec(j) for j in range(pg)]
            + [per_seq((C_KV_HEADS, rows, HEAD_DIM)), per_seq((C_KV_HEADS, rows, 3)), per_seq((n_cmp, 256)),
               per_seq((128, 512)),
               pl.BlockSpec((None, None) + kv_tile + (n_buf,), lambda s, i, pt: (layer, s, 0, 0, 0, 0)),
               pl.BlockSpec((n_cmp, n_blk), lambda s, i, pt: (0, 0))],
            out_specs=per_seq((C_KV_HEADS, rows, HEAD_DIM)),
            scratch_shapes=[pltpu.VMEM((C_KV_HEADS, rows, n_blk), F32), pltpu.VMEM((C_KV_HEADS, rows, HEAD_DIM), F32),
                            pltpu.VMEM((C_KV_HEADS, rows, 1), F32), pltpu.VMEM((C_KV_HEADS, rows, 1), F32),
                            pltpu.VMEM((C_KV_HEADS, rows, HEAD_DIM), F32)],
        ),
        out_shape=jax.ShapeDtypeStruct((ns, C_KV_HEADS, rows, HEAD_DIM), F32),
        compiler_params=_cparams(("parallel", "arbitrary")),
        name="nsa_sample",
    )(page_table, *([cache_slc_t] * pg), q16, gate16, kcvc, new_rows, win_buf_t, pool)


def _outproj_kernel(x_ref, ya_ref, yb_ref, yc_ref, w_ref, o_ref):
    acc = x_ref[...] + _dot(ya_ref[...], w_ref[0:256, :]) + _dot(yb_ref[...], w_ref[256:512, :])
    for hd in range(C_HEADS):
        r0 = 512 + hd * HEAD_DIM
        acc = acc + _dot(yc_ref[hd], w_ref[r0:r0 + HEAD_DIM, :])
    o_ref[...] = acc


def _outproj(x, ya, yb, yc, w_out, nb, t_len, tm):
    nt = t_len // tm
    rows = lambda w: pl.BlockSpec((tm, w), lambda b, i: (b * nt + i, 0))
    return pl.pallas_call(
        _outproj_kernel,
        grid=(nb, nt),
        in_specs=[rows(1024), rows(256), rows(256),
                  pl.BlockSpec((None, C_HEADS, tm, HEAD_DIM), lambda b, i: (b, 0, i, 0)),
                  pl.BlockSpec((1024, 1024), lambda b, i: (0, 0))],
        out_specs=rows(1024),
        out_shape=jax.ShapeDtypeStruct(x.shape, F32),
        compiler_params=_cparams(("parallel", "parallel")),
        name="outproj",
    )(x, ya, yb, yc, w_out)


def _top_vals_rows(x, k, rows_out):
    cols = x.shape[1]
    orow = lax.broadcasted_iota(jnp.int32, (rows_out, cols), 0)
    acc = jnp.zeros((rows_out, cols), F32)
    for j in range(k):
        m = jnp.max(x, axis=0, keepdims=True)
        x = jnp.where(x == m, NEG_INF, x)
        acc = jnp.where(orow == j, m, acc)
    return acc


def _peer_route_kernel(x_ref, g_ref, wq_ref, qg_ref, sk_ref, h2_ref, e1_ref, e2_ref, th_ref):
    x = x_ref[...]
    hb = (x * lax.rsqrt(jnp.mean(x * x, axis=-1, keepdims=True) + 1e-6) * g_ref[...]).astype(BF16)
    h2_ref[...] = hb
    qp = jnp.dot(hb, wq_ref[...], preferred_element_type=F32)
    k = PEER_TOPK
    for hd in range(PEER_HEADS):
        qn = []
        for c in range(2):
            qc = qp[:, (2 * hd + c) * N_KEYS:(2 * hd + c + 1) * N_KEYS]
            qn.append((qc * lax.rsqrt(jnp.mean(qc * qc, axis=-1, keepdims=True) + 1e-6)
                       * qg_ref[:, c * 128:(c + 1) * 128]).astype(BF16))
        for ts in range(x.shape[0] // 128):
            tsl = slice(ts * 128, (ts + 1) * 128)
            st = [_dot_nt(sk_ref[hd, c], qn[c][tsl]) for c in range(2)]
            top = [_top_vals_rows(s, k + 1, 24) for s in st]
            cand = jnp.concatenate([top[0][a:a + 1, :] + top[1][0:k, :] for a in range(k)], axis=0)
            best = _top_vals_rows(cand, k + 1, 24)
            nxt = jnp.maximum(best[k:k + 1, :], jnp.maximum(top[0][k:k + 1, :] + top[1][0:1, :],
                                                            top[0][0:1, :] + top[1][k:k + 1, :]))
            tau = 0.5 * (best[k - 1:k, :] + nxt)
            z = jnp.sum(jnp.where(cand >= tau, jnp.exp(cand - best[0:1, :]), 0.0), axis=0, keepdims=True)
            m2 = top[1][0:1, :]
            e1_ref[hd, :, tsl] = jnp.exp(st[0] - top[0][0:1, :]) / z
            e2_ref[hd, :, tsl] = jnp.exp(st[1] - m2)
            th_ref[hd, :, tsl] = jnp.exp((tau - m2) - st[0])


def _peer_dense_kernel(x_ref, h2_ref, u_ref, vt_ref, e1_ref, e2_ref, th_ref, o_ref, acc_ref, act_ref, w_ref, *, ea):
    j = pl.program_id(1)
    tm = h2_ref.shape[0]

    @pl.when(j == 0)
    def _():
        acc_ref[...] = jnp.zeros(acc_ref.shape, F32)

    h2 = h2_ref[...]
    for aa in range(ea):
        act_ref[aa * N_KEYS:(aa + 1) * N_KEYS, :] = jax.nn.gelu(
            _dot_nt(u_ref[aa * N_KEYS:(aa + 1) * N_KEYS, :], h2))
    hb = N_KEYS // 2
    th_rows = [[th_ref[hd, pl.ds(j * ea + aa, 1), :] for aa in range(ea)] for hd in range(PEER_HEADS)]
    e1_rows = [[e1_ref[hd, pl.ds(j * ea + aa, 1), :] for aa in range(ea)] for hd in range(PEER_HEADS)]
    for ts in range(tm // 128):
        tsl = slice(ts * 128, (ts + 1) * 128)
        for bh in range(2):
            bsl = slice(bh * hb, (bh + 1) * hb)
            gates = [jnp.zeros((hb, 128), F32) for _ in range(ea)]
            for hd in range(PEER_HEADS):
                e2 = e2_ref[hd, bsl, tsl]
                for aa in range(ea):
                    hit = e2 >= th_rows[hd][aa][:, tsl]
                    gates[aa] = gates[aa] + jnp.where(hit, e2, 0.0) * e1_rows[hd][aa][:, tsl]
            for aa in range(ea):
                rsl = slice(aa * N_KEYS + bh * hb, aa * N_KEYS + (bh + 1) * hb)
                w_ref[rsl, tsl] = (act_ref[rsl, tsl] * gates[aa]).astype(BF16)
    acc_ref[...] += jnp.dot(vt_ref[...], w_ref[...], preferred_element_type=F32)

    @pl.when(j == pl.num_programs(1) - 1)
    def _():
        o_ref[...] = x_ref[...] + acc_ref[...].T


PEER_EA = 4


def _peer(x1, p, tm):
    h2, e1, e2, th = _peer_route(x1, p, tm)
    return _peer_dense(x1, h2, e1, e2, th, p, tm)


def _peer_route(x1, p, tm):
    n = x1.shape[0]
    nt = n // tm
    full = lambda shape: pl.BlockSpec(shape, lambda i: (0,) * len(shape))
    tk = pl.BlockSpec((PEER_HEADS, N_KEYS, tm), lambda i: (0, 0, i))
    sd = jax.ShapeDtypeStruct((PEER_HEADS, N_KEYS, n), F32)
    return pl.pallas_call(
        _peer_route_kernel,
        grid=(nt,),
        in_specs=[pl.BlockSpec((tm, 1024), lambda i: (i, 0)), full((1, 1024)), full((1024, 2048)), full((1, 256)),
                  full((PEER_HEADS, 2, N_KEYS, N_KEYS))],
        out_specs=[pl.BlockSpec((tm, 1024), lambda i: (i, 0)), tk, tk, tk],
        out_shape=[jax.ShapeDtypeStruct((n, 1024), BF16), sd, sd, sd],
        compiler_params=_cparams(("parallel",)),
        name="peer_route",
    )(x1, p["ln2_g"], p["p_wq"], p["p_qnorm_g"], p["p_subkeys"])


def _peer_dense(x1, h2, e1, e2, th, p, tm):
    n = x1.shape[0]
    nt = n // tm
    et = PEER_EA * N_KEYS
    n_exp = p["p_u"].shape[0]
    tk2 = pl.BlockSpec((PEER_HEADS, N_KEYS, tm), lambda i, j: (0, 0, i))
    return pl.pallas_call(
        functools.partial(_peer_dense_kernel, ea=PEER_EA),
        grid=(nt, n_exp // et),
        in_specs=[pl.BlockSpec((tm, 1024), lambda i, j: (i, 0)), pl.BlockSpec((tm, 1024), lambda i, j: (i, 0)),
                  pl.BlockSpec((et, 1024), lambda i, j: (j, 0)), pl.BlockSpec((1024, et), lambda i, j: (0, j)),
                  tk2, tk2, tk2],
        out_specs=pl.BlockSpec((tm, 1024), lambda i, j: (i, 0)),
        out_shape=jax.ShapeDtypeStruct((n, 1024), F32),
        scratch_shapes=[pltpu.VMEM((1024, tm), F32), pltpu.VMEM((et, tm), F32), pltpu.VMEM((et, tm), BF16)],
        compiler_params=_cparams(("parallel", "arbitrary")),
        name="peer_dense",
    )(x1, h2, p["p_u"], p["p_vt"], e1, e2, th)


def _prep_layer(l, ln1_g, ln2_g, w_in, w_out, a_vnorm_g, a_ws, a_bs, b_mu, b_w0, b_w2, b_a0, b_a2, b_g2, b_kk,
                b_ka, b_rk, b_lnx_g, b_lnx_b, c_qnorm_g, c_knorm_g, c_cmp_w1, c_cmp_b1, c_cmp_w2, p_wq,
                p_qnorm_g, p_subkeys, p_u, p_v, dec_seq):
    p = {}
    p["ln1_g"] = ln1_g[l][None, :]
    p["ln2_g"] = ln2_g[l][None, :]
    w = w_in[l]
    z = lambda c: jnp.zeros((w.shape[0], c), w.dtype)
    p["w_in"] = jnp.concatenate([w[:, :1472], z(64), w[:, 1472:], z(104)], axis=1).astype(BF16)
    p["a_vnorm_g"] = a_vnorm_g[l][None, :]
    wm = jnp.where(jnp.tril(jnp.ones((CHUNK, CHUNK), bool)), a_ws[l], 0.0)
    p["wm_prompt"] = wm.astype(BF16)
    p["brow_prompt"] = jnp.repeat(a_bs[l].T, HEAD_DIM, axis=1)
    reps = CHUNK // dec_seq
    eye = jnp.eye(reps, dtype=F32)
    p["wm_sample"] = jnp.stack([jnp.kron(eye, wm[g, :dec_seq, :dec_seq]) for g in range(4)]).astype(BF16)
    p["brow_sample"] = jnp.tile(p["brow_prompt"][:dec_seq], (reps, 1))
    p["c_qnorm_g"] = jnp.tile(c_qnorm_g[l], C_HEADS)[None, :]
    p["c_knorm_g12"] = jnp.stack([jnp.tile(c_knorm_g[l, 1], 2), jnp.tile(c_knorm_g[l, 2], 2)])
    p["bd512"] = _block_ones(512, HEAD_DIM)
    p["bd256"] = _block_ones(256, HEAD_DIM)
    p["bd128"] = _block_ones(128, HEAD_DIM)
    eye2 = jnp.eye(2, dtype=F32)
    w1h = c_cmp_w1[l].reshape(2, 2, CMP_STRIDE, HEAD_DIM, HEAD_DIM)
    p["c_w1e"] = jnp.einsum("kjsdc,kK,hH->jsKHdkhc", w1h, eye2, eye2).reshape(2, 4096, 256).astype(BF16)
    p["c_b1e"] = jnp.broadcast_to(c_cmp_b1[l][:, None, :], (2, 2, HEAD_DIM)).reshape(1, 256)
    p["c_w2e"] = jnp.einsum("kcd,kK,hH->khcKHd", c_cmp_w2[l], eye2, eye2).reshape(256, 256).astype(BF16)
    p["c_kg0"] = jnp.tile(c_knorm_g[l, 0], 2)[None, :]
    row = lambda a: a.reshape(1, -1)
    p["b_mu"] = row(b_mu[l])
    p["b_w0"] = row(b_w0[l])
    p["b_a0"] = row(b_a0[l])
    zz = jnp.zeros((64, 256), F32)
    p["b_w2a"] = jnp.concatenate([jnp.concatenate([b_w2[l], zz], axis=1),
                                  jnp.concatenate([zz, b_a2[l]], axis=1)], axis=0)
    p["b_g2"] = b_g2[l]
    p["b_kk"] = row(b_kk[l])
    p["b_ka"] = row(b_ka[l])
    p["b_rk"] = row(b_rk[l])
    p["b_lnx_g"] = row(b_lnx_g[l])
    p["b_lnx_b"] = row(b_lnx_b[l])
    p["w_out"] = w_out[l].astype(BF16)
    p["p_wq"] = p_wq[l].astype(BF16)
    p["p_qnorm_g"] = row(p_qnorm_g[l])
    p["p_subkeys"] = p_subkeys[l].astype(BF16)
    p["p_u"] = p_u[l].astype(BF16)
    p["p_vt"] = p_v[l].T.astype(BF16)
    return p


def _st_in(wkv):
    b = wkv.shape[0]
    return wkv.transpose(0, 3, 1, 2).reshape(b, HEAD_DIM, 256)


def _st_out(st):
    b = st.shape[0]
    return st.reshape(b, HEAD_DIM, B_HEADS, HEAD_DIM).transpose(0, 2, 3, 1)


def kernel(x_prompt, x_sample, cache_cmp_kv, cache_slc_kv, cache_win_kv, state_wkv, state_shift, page_table, ln1_g, ln2_g, w_in, w_out, a_vnorm_g, a_ws, a_bs, b_mu, b_w0, b_w2, b_a0, b_a2, b_g2, b_kk, b_ka, b_rk, b_lnx_g, b_lnx_b, c_qnorm_g, c_knorm_g, c_cmp_w1, c_cmp_b1, c_cmp_w2, p_wq, p_qnorm_g, p_subkeys, p_u, p_v):
    weights = (ln1_g, ln2_g, w_in, w_out, a_vnorm_g, a_ws, a_bs, b_mu, b_w0, b_w2, b_a0, b_a2, b_g2, b_kk, b_ka,
               b_rk, b_lnx_g, b_lnx_b, c_qnorm_g, c_knorm_g, c_cmp_w1, c_cmp_b1, c_cmp_w2, p_wq, p_qnorm_g,
               p_subkeys, p_u, p_v)
    nb, t_len, d_model = x_prompt.shape
    ns, dec_seq, _ = x_sample.shape
    depth = ln1_g.shape[0]
    n_pool = cache_cmp_kv.shape[1]
    n_s = ns * dec_seq
    n_sp = -(-n_s // 128) * 128
    rwkv_c = 64
    dec_c = 8
    pad_rows = lambda a, n: jnp.pad(a, ((0, n - a.shape[0]),) + ((0, 0),) * (a.ndim - 1))
    xp = x_prompt.reshape(nb * t_len, d_model)
    xs = pad_rows(x_sample.reshape(n_s, d_model), n_sp)
    kv5 = lambda a, b, t: a.reshape(b, t, 2, C_KV_HEADS, HEAD_DIM)
    cache_cmp_t = cache_cmp_kv.transpose(0, 1, 3, 4, 5, 2)
    cache_slc_t = cache_slc_kv.transpose(0, 1, 3, 4, 5, 2)
    cache_win_t = cache_win_kv.transpose(0, 1, 3, 4, 5, 2)
    outs_p, outs_s = [], []
    for l in range(depth):
        p = _prep_layer(l, *weights, dec_seq=dec_seq)
        ya, _, bproj, q, cmp, slc, win, gate, kvb = _inproj(xp, p, p["wm_prompt"], p["brow_prompt"], 256)
        bp3 = bproj.reshape(nb, t_len, B_PROJ)
        prev = jnp.concatenate([jnp.zeros((nb, 1, B_PROJ), F32), bp3[:, :-1]], axis=1).reshape(nb * t_len, B_PROJ)
        yb, st = _rwkv(bproj, prev, jnp.zeros((nb, HEAD_DIM, 256), F32), p, nb, t_len // rwkv_c, rwkv_c, rwkv_c)
        kcvc = _compress(cmp, p, nb)
        yc = _nsa_prompt(q, gate, kcvc, kvb, nb, t_len)
        x1 = _outproj(xp, ya, yb, yc, p["w_out"], nb, t_len, 512)
        xp = _peer(x1, p, 512)
        n_win = min(WINDOW, t_len)
        outs_p.append((kv5(cmp, nb, t_len), kv5(slc, nb, t_len), kv5(win, nb, t_len)[:, t_len - n_win:],
                       _st_out(st), bp3[:, -1]))
        sya, sv, sbproj, sq, scmp, sslc, swin, sgate, _ = _inproj(xs, p, p["wm_sample"], p["brow_sample"], 128)
        sb3 = sbproj[:n_s].reshape(ns, dec_seq, B_PROJ)
        sprev = jnp.concatenate([state_shift[l][:, None], sb3[:, :-1]], axis=1)
        pad_c = lambda a: jnp.pad(a, ((0, 0), (0, dec_c - dec_seq), (0, 0))).reshape(ns * dec_c, B_PROJ)
        syb, sst = _rwkv(pad_c(sb3), pad_c(sprev), _st_in(state_wkv[l]), p, ns, 1, dec_c, dec_seq)
        syb = pad_rows(syb.reshape(ns, dec_c, 256)[:, :dec_seq].reshape(n_s, 256), n_sp)
        skc = _compress_paged(cache_cmp_t, page_table, p, l)
        q16 = (sq[:n_s].reshape(ns, dec_seq, C_KV_HEADS, C_GROUP, HEAD_DIM).transpose(0, 2, 3, 1, 4)
               .reshape(ns, C_KV_HEADS, C_GROUP * dec_seq, HEAD_DIM))
        g16 = (sgate[:n_s, :3 * C_HEADS].reshape(ns, dec_seq, C_KV_HEADS, C_GROUP, 3).transpose(0, 2, 3, 1, 4)
               .reshape(ns, C_KV_HEADS, C_GROUP * dec_seq, 3))
        new_rows = jnp.concatenate([sslc[:n_s], swin[:n_s]], axis=1).reshape(ns, dec_seq, 512)
        new_rows = jnp.pad(new_rows, ((0, 0), (0, 128 - dec_seq), (0, 0)))
        win_buf = cache_win_kv[l].reshape(ns, -1, 256)
        so = _nsa_sample(page_table, q16, g16, skc, cache_slc_t, new_rows, cache_win_t, l, dec_seq)
        syc = (so.reshape(ns, C_KV_HEADS, C_GROUP, dec_seq, HEAD_DIM).transpose(1, 2, 0, 3, 4)
               .reshape(C_HEADS, n_s, HEAD_DIM))
        syc = jnp.pad(syc, ((0, 0), (0, n_sp - n_s), (0, 0)))[None]
        sx1 = _outproj(xs, sya, syb, syc, p["w_out"], 1, n_sp, 128)
        xs = _peer(sx1, p, 128)
        swin3 = swin[:n_s].reshape(ns, dec_seq, 256)
        win_new = jnp.concatenate([win_buf, swin3], axis=1)[:, dec_seq:]
        outs_s.append((kv5(scmp[:n_s], ns, dec_seq), kv5(sslc[:n_s], ns, dec_seq),
                       kv5(win_new, ns, win_buf.shape[1]), _st_out(sst), sb3[:, -1],
                       sv[:n_s].reshape(ns, dec_seq, A_WIDTH)))
    stk = lambda lst, i: jnp.stack([s[i] for s in lst], axis=0)
    return (xp.reshape(nb, t_len, d_model), xs[:n_s].reshape(ns, dec_seq, d_model),
            stk(outs_p, 0), stk(outs_p, 1), stk(outs_p, 2), stk(outs_p, 3), stk(outs_p, 4),
            stk(outs_s, 0), stk(outs_s, 1), stk(outs_s, 2), stk(outs_s, 3), stk(outs_s, 4), stk(outs_s, 5))
```

```python
import functools
import math

import jax
import jax.numpy as jnp
from jax import lax
from jax.experimental import pallas as pl
from jax.experimental.pallas import tpu as pltpu

F32 = jnp.float32
BF16 = jnp.bfloat16
HIGHEST = lax.Precision.HIGHEST

HEAD_DIM = 64
CHUNK = 128
A_WIDTH = 256
B_WIDTH = 256
B_HEADS = 4
B_PROJ = 960
C_HEADS = 8
C_KV_HEADS = 2
C_GROUP = 4
C_WIDTH = 512
CMP_STRIDE = 16
SLC_BLOCK = 64
SLC_TOPK = 16
WINDOW = 512
Q_BLOCK = 128
PAGE = 128
PEER_HEADS = 8
N_KEYS = 128
PEER_TOPK = 16
RWKV_GN_EPS = 64e-5
NEG_BIG = -1e30
POS_BIG = 1e30
NEG_INF = float("-inf")
VMEM_LIMIT = 56 * 1024 * 1024


def _cparams(sem):
    return pltpu.CompilerParams(dimension_semantics=sem, vmem_limit_bytes=VMEM_LIMIT)


def _dot(a, b):
    return jnp.dot(a.astype(BF16), b.astype(BF16), preferred_element_type=F32)


def _dot_nt(a, b):
    return lax.dot_general(a.astype(BF16), b.astype(BF16), (((1,), (1,)), ((), ())),
                           preferred_element_type=F32)


def _dot2(x, m):
    hi = x.astype(BF16)
    lo = (x - hi.astype(F32)).astype(BF16)
    return (jnp.dot(hi, m, preferred_element_type=F32) + jnp.dot(lo, m, preferred_element_type=F32))


def _dot3(a, b, dims):
    a_hi = a.astype(BF16)
    b_hi = b.astype(BF16)
    a_lo = (a - a_hi.astype(F32)).astype(BF16)
    b_lo = (b - b_hi.astype(F32)).astype(BF16)
    dg = lambda x, y: lax.dot_general(x, y, (dims, ((), ())), preferred_element_type=F32)
    return dg(a_hi, b_hi) + (dg(a_hi, b_lo) + dg(a_lo, b_hi))


def _hdot(a, b):
    return _dot3(a, b, ((1,), (0,)))


def _hdot_nt(a, b):
    return _dot3(a, b, ((1,), (1,)))


def _hdot_tn(a, b):
    return _dot3(a, b, ((0,), (0,)))


def _block_ones(width, group):
    i = jnp.arange(width)
    return (i[:, None] // group == i[None, :] // group).astype(BF16)


IN_PAD = 2944


def _inproj_kernel(x_ref, g_ref, w_ref, avg_ref, wm_ref, brow_ref, qg_ref, kg_ref, bd_ref,
                   ya_ref, v_ref, b_ref, q_ref, cmp_ref, slc_ref, win_ref, gate_ref, kvb_ref, *, tm):
    x = x_ref[...]
    h = x * lax.rsqrt(jnp.mean(x * x, axis=-1, keepdims=True) + 1e-6) * g_ref[...]
    z = jnp.dot(h.astype(BF16), w_ref[...], preferred_element_type=F32)
    b_ref[...] = z[:, 512:512 + B_PROJ]
    u = jax.nn.gelu(z[:, 0:256])
    gv = jax.nn.gelu(z[:, 256:512])
    v = gv * lax.rsqrt(jnp.mean(gv * gv, axis=-1, keepdims=True) + 1e-6) * avg_ref[...]
    v_ref[...] = v
    lane_g = lax.broadcasted_iota(jnp.int32, (CHUNK, A_WIDTH), 1) // HEAD_DIM
    for c in range(tm // CHUNK):
        vc = v[c * CHUNK:(c + 1) * CHUNK].astype(BF16)
        mixed = brow_ref[...]
        for g in range(4):
            mg = jnp.dot(wm_ref[g], vc, preferred_element_type=F32)
            mixed = mixed + jnp.where(lane_g == g, mg, 0.0)
        ya_ref[c * CHUNK:(c + 1) * CHUNK, :] = u[c * CHUNK:(c + 1) * CHUNK] * mixed
    bd = bd_ref[...]
    q = z[:, 1536:2048]
    q_ref[...] = q * lax.rsqrt(_dot2(q * q, bd) * (1.0 / HEAD_DIM) + 1e-6) * qg_ref[...]
    bd128 = bd[0:128, 0:128]
    cmp_ref[...] = z[:, 2048:2304]
    for j, (o_ref, off) in enumerate(((slc_ref, 2304), (win_ref, 2560))):
        k = z[:, off:off + 128]
        kn = k * lax.rsqrt(_dot2(k * k, bd128) * (1.0 / HEAD_DIM) + 1e-6) * kg_ref[j:j + 1, :]
        vv = z[:, off + 128:off + 256]
        o_ref[:, 0:128] = kn
        o_ref[:, 128:256] = vv
        kvb_ref[:, j * 256:j * 256 + 128] = kn.astype(BF16)
        kvb_ref[:, j * 256 + 128:j * 256 + 256] = vv.astype(BF16)
    gate_ref[...] = jax.nn.sigmoid(z[:, 2816:2944])


def _inproj(x, p, wm, brow, tm):
    n = x.shape[0]
    full = lambda shape: pl.BlockSpec(shape, lambda i: (0,) * len(shape))
    rows = lambda w: pl.BlockSpec((tm, w), lambda i: (i, 0))
    outs = [(A_WIDTH, F32), (A_WIDTH, F32), (B_PROJ, F32), (C_WIDTH, F32), (256, F32), (256, F32), (256, F32),
            (128, F32), (512, BF16)]
    return pl.pallas_call(
        functools.partial(_inproj_kernel, tm=tm),
        grid=(n // tm,),
        in_specs=[rows(1024), full((1, 1024)), full((1024, IN_PAD)), full((1, A_WIDTH)), full((4, CHUNK, CHUNK)),
                  full((CHUNK, A_WIDTH)), full((1, C_WIDTH)), full((2, 128)), full((512, 512))],
        out_specs=[rows(w) for w, _ in outs],
        out_shape=[jax.ShapeDtypeStruct((n, w), d) for w, d in outs],
        compiler_params=_cparams(("parallel",)),
        name="inproj",
    )(x, p["ln1_g"], p["w_in"], p["a_vnorm_g"], wm, brow, p["c_qnorm_g"], p["c_knorm_g12"], p["bd512"])


def _rwkv_prep_kernel(x_ref, xp_ref, mu_ref, w0_ref, w2a_ref, a0_ref, g2_ref, kkp_ref, ka_ref, rk_ref, bd_ref,
                      tri_ref, p2_ref, y0_ref, p3_ref, z_ref, g_ref, bonus_ref, *, c, t_valid):
    x = x_ref[...]
    xs = x + (xp_ref[...] - x) * mu_ref[...]
    r = xs[:, 0:256]
    k = xs[:, 256:512]
    v = xs[:, 512:768]
    wa = xs[:, 768:896]
    lane = lax.broadcasted_iota(jnp.int32, wa.shape, 1)
    pre = _hdot(jnp.where(lane < 64, jnp.tanh(wa), wa), w2a_ref[...])
    y = -(w0_ref[...] + pre[:, 0:256])
    softplus = jnp.maximum(y, 0.0) + jnp.log1p(jnp.exp(-jnp.abs(y)))
    ew = jnp.exp(-softplus - 0.5)
    a = jax.nn.sigmoid(a0_ref[...] + pre[:, 256:512])
    g_ref[...] = _hdot(jax.nn.sigmoid(xs[:, 896:960]), g2_ref[...])
    bd = bd_ref[...]
    kkr = k * kkp_ref[...]
    kk = kkr / jnp.maximum(jnp.sqrt(_dot2(kkr * kkr, bd)), 1e-12)
    k2 = k * (1.0 + (a - 1.0) * ka_ref[...])
    bonus_ref[...] = _dot2(r * k2 * rk_ref[...], bd) * v
    if t_valid < c:
        live = lax.broadcasted_iota(jnp.int32, ew.shape, 0) < t_valid
        ew = jnp.where(live, ew, 0.0)
        kk = jnp.where(live, kk, 0.0)
        k2 = jnp.where(live, k2, 0.0)
        v = jnp.where(live, v, 0.0)
    cum = jnp.dot(tri_ref[...], -ew, preferred_element_type=F32, precision=HIGHEST)
    gam = jnp.exp(cum)
    ginv = jnp.exp(-cum)
    g_end = gam[c - 1:c, :]
    alpha_t = -kk * jnp.exp(cum + ew)
    bhat = kk * a * ginv
    khat = k2 * ginv
    rt = r * gam
    kbar = khat * g_end
    bbar = bhat * g_end
    row = lax.broadcasted_iota(jnp.int32, (c, c), 0)
    col = lax.broadcasted_iota(jnp.int32, (c, c), 1)
    eye_c = (row == col).astype(F32)
    r64 = lax.broadcasted_iota(jnp.int32, (HEAD_DIM, HEAD_DIM), 0)
    c64 = lax.broadcasted_iota(jnp.int32, (HEAD_DIM, HEAD_DIM), 1)
    hs = range(B_HEADS)
    sls = [slice(h * HEAD_DIM, (h + 1) * HEAD_DIM) for h in hs]
    al = [alpha_t[:, s] for s in sls]
    bh = [bhat[:, s] for s in sls]
    kh = [khat[:, s] for s in sls]
    rh = [rt[:, s] for s in sls]
    vh = [v[:, s] for s in sls]
    mb = [jnp.where(row > col, _hdot_nt(al[h], bh[h]), 0.0) for h in hs]
    mk = [jnp.where(row > col, _hdot_nt(al[h], kh[h]), 0.0) for h in hs]
    qk = [jnp.where(row >= col, _hdot_nt(rh[h], kh[h]), 0.0) for h in hs]
    qb = [jnp.where(row >= col, _hdot_nt(rh[h], bh[h]), 0.0) for h in hs]
    mkv = [_hdot(mk[h], vh[h]) for h in hs]
    tinv = [eye_c + mb[h] for h in hs]
    pw = mb
    for _ in range(int(math.log2(c)) - 1):
        pw = [_hdot(pw[h], pw[h]) for h in hs]
        tinv = [tinv[h] + _hdot(tinv[h], pw[h]) for h in hs]
    p1 = [_hdot(tinv[h], al[h]) for h in hs]
    u = [_hdot(tinv[h], mkv[h]) for h in hs]
    for h in hs:
        sl = sls[h]
        p2_ref[:, sl] = rh[h] + _hdot(qb[h], p1[h])
        y0_ref[:, sl] = _hdot(qk[h], vh[h]) + _hdot(qb[h], u[h])
        p3_ref[:, sl] = jnp.where(r64 == c64, g_end[:, sl], 0.0) + _hdot_tn(bbar[:, sl], p1[h])
        z_ref[:, sl] = _hdot_tn(kbar[:, sl], vh[h]) + _hdot_tn(bbar[:, sl], u[h])


def _rwkv_seq_kernel(st0_ref, p2_ref, y0_ref, p3_ref, z_ref, g_ref, bonus_ref, lg_ref, lb_ref,
                     y_ref, st_ref, st_scr):
    @pl.when(pl.program_id(1) == 0)
    def _():
        st_scr[...] = st0_ref[...]

    for h in range(B_HEADS):
        sl = slice(h * HEAD_DIM, (h + 1) * HEAD_DIM)
        st = st_scr[:, sl]
        y = _hdot(p2_ref[:, sl], st) + y0_ref[:, sl]
        st_scr[:, sl] = _hdot(p3_ref[:, sl], st) + z_ref[:, sl]
        mu = jnp.mean(y, axis=-1, keepdims=True)
        var = jnp.mean(jnp.square(y - mu), axis=-1, keepdims=True)
        yn = (y - mu) * lax.rsqrt(var + RWKV_GN_EPS) * lg_ref[:, sl] + lb_ref[:, sl]
        y_ref[:, sl] = (yn + bonus_ref[:, sl]) * g_ref[:, sl]
    st_ref[...] = st_scr[...]


def _rwkv(proj, prev, st0, p, nb, nch, c, t_valid):
    n = proj.shape[0]
    full = lambda shape: pl.BlockSpec(shape, lambda b, i: (0,) * len(shape))
    rows = lambda w: pl.BlockSpec((c, w), lambda b, i: (b * nch + i, 0))
    mats = pl.BlockSpec((HEAD_DIM, 256), lambda b, i: (b * nch + i, 0))
    tri = (jnp.arange(c)[:, None] >= jnp.arange(c)[None, :]).astype(F32)
    p2, y0, p3, z, g, bonus = pl.pallas_call(
        functools.partial(_rwkv_prep_kernel, c=c, t_valid=t_valid),
        grid=(nb, nch),
        in_specs=[rows(B_PROJ), rows(B_PROJ), full((1, B_PROJ)), full((1, 256)), full((128, 512)), full((1, 256)),
                  full((64, 256)), full((1, 256)), full((1, 256)), full((1, 256)), full((256, 256)), full((c, c))],
        out_specs=[rows(256), rows(256), mats, mats, rows(256), rows(256)],
        out_shape=[jax.ShapeDtypeStruct((n, 256), F32), jax.ShapeDtypeStruct((n, 256), F32),
                   jax.ShapeDtypeStruct((nb * nch * HEAD_DIM, 256), F32),
                   jax.ShapeDtypeStruct((nb * nch * HEAD_DIM, 256), F32),
                   jax.ShapeDtypeStruct((n, 256), F32), jax.ShapeDtypeStruct((n, 256), F32)],
        compiler_params=_cparams(("parallel", "parallel")),
        name="rwkv_prep",
    )(proj, prev, p["b_mu"], p["b_w0"], p["b_w2a"], p["b_a0"], p["b_g2"], p["b_kk"], p["b_ka"], p["b_rk"],
      p["bd256"], tri)
    st_spec = pl.BlockSpec((None, HEAD_DIM, 256), lambda b, i: (b, 0, 0))
    y, st = pl.pallas_call(
        _rwkv_seq_kernel,
        grid=(nb, nch),
        in_specs=[st_spec, rows(256), rows(256), mats, mats, rows(256), rows(256), full((1, 256)), full((1, 256))],
        out_specs=[rows(256), st_spec],
        out_shape=[jax.ShapeDtypeStruct((n, 256), F32), jax.ShapeDtypeStruct((nb, HEAD_DIM, 256), F32)],
        scratch_shapes=[pltpu.VMEM((HEAD_DIM, 256), F32)],
        compiler_params=_cparams(("parallel", "arbitrary")),
        name="rwkv_seq",
    )(st0, p2, y0, p3, z, g, bonus, p["b_lnx_g"], p["b_lnx_b"])
    return y, st


def _compress_tail(a0, a1, n_out, b1_ref, w2_ref, kg_ref, bd_ref, o_ref):
    n = a1.shape[0]
    hid = jax.nn.gelu(a0 + pltpu.roll(a1, n - 1, 0) + b1_ref[...])[0:n_out]
    out = _dot(hid, w2_ref[...])
    k = out[:, 0:128]
    kn = k * lax.rsqrt(_dot2(k * k, bd_ref[...]) * (1.0 / HEAD_DIM) + 1e-6) * kg_ref[...]
    o_ref[:, 0:128] = kn.astype(BF16)
    o_ref[:, 128:256] = out[:, 128:256].astype(BF16)


def _compress_kernel(x_ref, w1_ref, b1_ref, w2_ref, kg_ref, bd_ref, o_ref):
    x = x_ref[...].astype(BF16)
    a0 = jnp.dot(x, w1_ref[0], preferred_element_type=F32)
    a1 = jnp.dot(x, w1_ref[1], preferred_element_type=F32)
    _compress_tail(a0, a1, x.shape[0], b1_ref, w2_ref, kg_ref, bd_ref, o_ref)


def _compress_paged_kernel(pt_ref, *refs, pg):
    pages = refs[:pg + 1]
    w1_ref, b1_ref, w2_ref, kg_ref, bd_ref, o_ref, rows_scr = refs[pg + 1:]
    for j, r in enumerate(pages):
        rows = r[...].reshape(4 * HEAD_DIM, PAGE).T
        rows_scr[0, j * PAGE:(j + 1) * PAGE, :] = rows[:, 0:128]
        rows_scr[1, j * PAGE:(j + 1) * PAGE, :] = rows[:, 128:256]
    n_piece = (pg + 1) * (PAGE // CMP_STRIDE)
    a0 = jnp.zeros((n_piece, 256), F32)
    a1 = jnp.zeros((n_piece, 256), F32)
    for s in range(CMP_STRIDE):
        xs = jnp.concatenate([rows_scr[hf, pl.ds(s, n_piece, stride=CMP_STRIDE), :] for hf in range(2)],
                             axis=1).astype(BF16)
        a0 = a0 + jnp.dot(xs, w1_ref[0, s * 256:(s + 1) * 256, :], preferred_element_type=F32)
        a1 = a1 + jnp.dot(xs, w1_ref[1, s * 256:(s + 1) * 256, :], preferred_element_type=F32)
    _compress_tail(a0, a1, pg * 8, b1_ref, w2_ref, kg_ref, bd_ref, o_ref)


def _compress_weight_specs(index):
    full = lambda shape: pl.BlockSpec(shape, index(len(shape)))
    return [full((2, 4096, 256)), full((1, 256)), full((256, 256)), full((1, 128)), full((128, 128))]


def _compress(cmp_rows, p, nb):
    n_piece = cmp_rows.shape[0] // nb // CMP_STRIDE
    x = cmp_rows.reshape(nb, n_piece, 4096)
    return pl.pallas_call(
        _compress_kernel,
        grid=(nb,),
        in_specs=[pl.BlockSpec((None, n_piece, 4096), lambda b: (b, 0, 0))]
        + _compress_weight_specs(lambda r: (lambda b: (0,) * r)),
        out_specs=pl.BlockSpec((None, n_piece, 256), lambda b: (b, 0, 0)),
        out_shape=jax.ShapeDtypeStruct((nb, n_piece, 256), BF16),
        compiler_params=_cparams(("parallel",)),
        name="compress",
    )(x, p["c_w1e"], p["c_b1e"], p["c_w2e"], p["c_kg0"], p["bd128"])


CMP_PAGES = 16


def _compress_paged(cache_t, page_table, p, layer):
    ns, n_pages = page_table.shape
    pg = min(CMP_PAGES, n_pages)

    def page_spec(j):
        return pl.BlockSpec((None, None, 2, C_KV_HEADS, HEAD_DIM, PAGE),
                            lambda s, i, pt: (layer, pt[s, jnp.minimum(i * pg + j, n_pages - 1)], 0, 0, 0, 0))

    return pl.pallas_call(
        functools.partial(_compress_paged_kernel, pg=pg),
        grid_spec=pltpu.PrefetchScalarGridSpec(
            num_scalar_prefetch=1,
            grid=(ns, n_pages // pg),
            in_specs=[page_spec(j) for j in range(pg + 1)]
            + _compress_weight_specs(lambda r: (lambda s, i, pt: (0,) * r)),
            out_specs=pl.BlockSpec((None, pg * 8, 256), lambda s, i, pt: (s, i, 0)),
            scratch_shapes=[pltpu.VMEM((2, (pg + 1) * PAGE, 128), F32)],
        ),
        out_shape=jax.ShapeDtypeStruct((ns, n_pages * 8, 256), BF16),
        compiler_params=_cparams(("parallel", "arbitrary")),
        name="compress_paged",
    )(page_table, *([cache_t] * (pg + 1)), p["c_w1e"], p["c_b1e"], p["c_w2e"], p["c_kg0"], p["bd128"])


def _alibi_slope(head):
    return 2.0 ** (-8.0 * (head + 1.0) / C_HEADS)


def _topk_mask(x, k):
    nl = x.shape[-1]
    lane = lax.broadcasted_iota(jnp.int32, x.shape, x.ndim - 1).astype(F32)
    sel = jnp.zeros(x.shape, F32)
    for _ in range(k):
        m = jnp.max(x, axis=-1, keepdims=True)
        idx = jnp.min(jnp.where(x == m, lane, float(nl)), axis=-1, keepdims=True)
        hit = lane == idx
        sel = jnp.where(hit, 1.0, sel)
        x = jnp.where(hit, NEG_INF, x)
    return sel


def _masked_softmax(s, valid):
    s = jnp.where(valid, s, NEG_BIG)
    e = jnp.exp(s - jnp.max(s, axis=-1, keepdims=True))
    return jnp.where(valid, e / jnp.sum(e, axis=-1, keepdims=True), 0.0)


def _flash_step(carry, s, ok, v, v_is_t=False):
    m, l, acc = carry
    if ok is not None:
        s = jnp.where(ok, s, NEG_BIG)
    m_new = jnp.maximum(m, jnp.max(s, axis=-1, keepdims=True))
    alpha = jnp.exp(m - m_new)
    pr = jnp.exp(s - m_new)
    if ok is not None:
        pr = jnp.where(ok, pr, 0.0)
    l = alpha * l + jnp.sum(pr, axis=-1, keepdims=True)
    pv = _dot_nt(pr, v) if v_is_t else jnp.dot(pr.astype(BF16), v, preferred_element_type=F32)
    return m_new, l, alpha * acc + pv


def _flash_init(rows):
    return (jnp.full((rows, 1), NEG_BIG, F32), jnp.zeros((rows, 1), F32), jnp.zeros((rows, HEAD_DIM), F32))


def _topk_mask_rows(x, k):
    n = x.shape[0]
    ridx = lax.broadcasted_iota(jnp.int32, x.shape, 0).astype(F32)
    sel = jnp.zeros(x.shape, F32)
    for _ in range(k):
        m = jnp.max(x, axis=0, keepdims=True)
        idx = jnp.min(jnp.where(x == m, ridx, float(n)), axis=0, keepdims=True)
        hit = ridx == idx
        sel = jnp.where(hit, 1.0, sel)
        x = jnp.where(hit, NEG_INF, x)
    return sel


def _nsa_prompt_kernel_t(q_ref, gate_ref, kc_ref, kvb_ref, pool_ref, o_ref, *, n_slc):
    i = pl.program_id(1)
    qb = Q_BLOCK
    nq4 = C_GROUP * qb
    n_cmp = kc_ref.shape[0]
    nbl = pool_ref.shape[0]
    lane4 = lax.broadcasted_iota(jnp.int32, (1, nq4), 1)
    tq_row = i * qb + lane4 % qb
    tok_row = i * qb + lax.broadcasted_iota(jnp.int32, (1, qb), 1)
    gate_t = gate_ref[...].T
    blk_col = lax.broadcasted_iota(jnp.int32, (nbl, 1), 0)
    rel2 = (lax.broadcasted_iota(jnp.int32, (256, nq4), 0) - lane4 % qb).astype(F32)
    exp_row = lax.broadcasted_iota(jnp.int32, (256, nbl), 0) // SLC_BLOCK
    exp_col = lax.broadcasted_iota(jnp.int32, (256, nbl), 1)
    half = i // 2
    for h in range(C_KV_HEADS):
        heads = [C_GROUP * h + g for g in range(C_GROUP)]
        q_t = (jnp.concatenate([q_ref[:, hd * HEAD_DIM:(hd + 1) * HEAD_DIM] for hd in heads], axis=0)
               * (HEAD_DIM ** -0.5)).T.astype(BF16)
        slope_row = jnp.full((1, nq4), _alibi_slope(heads[0]), F32)
        for g in range(1, C_GROUP):
            slope_row = jnp.where(lane4 // qb == g, _alibi_slope(heads[g]), slope_row)
        kcol = slice(h * HEAD_DIM, (h + 1) * HEAD_DIM)
        vcol = slice(128 + h * HEAD_DIM, 128 + (h + 1) * HEAD_DIM)
        c_dist = tq_row - (lax.broadcasted_iota(jnp.int32, (n_cmp, 1), 0) * CMP_STRIDE + (2 * CMP_STRIDE - 1))
        valid = c_dist >= 0
        s_c = jnp.dot(kc_ref[:, kcol], q_t, preferred_element_type=F32) - slope_row * c_dist.astype(F32)
        s_c = jnp.where(valid, s_c, NEG_BIG)
        e_c = jnp.exp(s_c - jnp.max(s_c, axis=0, keepdims=True))
        p_c = jnp.where(valid, e_c / jnp.sum(e_c, axis=0, keepdims=True), 0.0)
        o_c = lax.dot_general(kc_ref[:, vcol], p_c.astype(BF16), (((0,), (0,)), ((), ())),
                              preferred_element_type=F32)
        psum = p_c[:, 0:qb] + p_c[:, qb:2 * qb] + p_c[:, 2 * qb:3 * qb] + p_c[:, 3 * qb:4 * qb]
        ps_hi = psum.astype(BF16)
        ps_lo = (psum - ps_hi.astype(F32)).astype(BF16)
        imp = (jnp.dot(pool_ref[...], ps_hi, preferred_element_type=F32)
               + jnp.dot(pool_ref[...], ps_lo, preferred_element_type=F32))
        imp = jnp.where(blk_col * SLC_BLOCK <= tok_row, imp, NEG_BIG)
        imp = jnp.where((blk_col == 0) | (blk_col == tok_row // SLC_BLOCK), POS_BIG, imp)
        imp = jnp.where(blk_col < n_slc, imp, NEG_INF)
        sel_f = _topk_mask_rows(imp, min(SLC_TOPK, n_slc))
        used_col = jnp.max(sel_f, axis=1, keepdims=True)
        sel_t = sel_f.astype(BF16)
        bias2 = slope_row * rel2

        def tiles_step(carry, ps, kcols, vcols, mask_fn):
            m, l, acc = carry
            rows, scores, oks = [], [], []
            for p in ps:
                r0 = pl.multiple_of(jnp.maximum(p, 0) * 256, 256)
                off = ((2 * p - i) * qb).astype(F32)
                s = (jnp.dot(kvb_ref[pl.ds(r0, 256), kcols], q_t, preferred_element_type=F32) + bias2
                     + slope_row * off)
                ok = mask_fn(p, off)
                if ok is not None:
                    s = jnp.where(ok, s, NEG_BIG)
                rows.append(r0)
                scores.append(s)
                oks.append(ok)
            m_new = m
            for s in scores:
                m_new = jnp.maximum(m_new, jnp.max(s, axis=0, keepdims=True))
            alpha = jnp.exp(m - m_new)
            l = alpha * l
            acc = alpha * acc
            for r0, s, ok in zip(rows, scores, oks):
                pr = jnp.exp(s - m_new)
                if ok is not None:
                    pr = jnp.where(ok, pr, 0.0)
                l = l + jnp.sum(pr, axis=0, keepdims=True)
                acc = acc + lax.dot_general(kvb_ref[pl.ds(r0, 256), vcols], pr.astype(BF16),
                                            (((0,), (0,)), ((), ())), preferred_element_type=F32)
            return m_new, l, acc

        def sel_mask(p, off):
            expand = (exp_col == 4 * p + exp_row).astype(BF16)
            m1 = jnp.dot(expand, sel_t, preferred_element_type=F32) > 0.5
            return jnp.concatenate([m1] * C_GROUP, axis=1)

        def sel_causal_mask(p, off):
            return sel_mask(p, off) & (rel2 + off <= 0.0)

        def win_mask(p, off):
            d = rel2 + jnp.where(p >= 0, off, float(WINDOW))
            return (d <= 0.0) & (d > -float(WINDOW))

        def sel_body(pp, carry):
            used = jnp.max(jnp.where(blk_col // 8 == pp, used_col, 0.0))
            return lax.cond(used > 0.0, lambda cr: tiles_step(cr, (2 * pp, 2 * pp + 1), kcol, vcol, sel_mask),
                            lambda cr: cr, carry)

        init = (jnp.full((1, nq4), NEG_BIG, F32), jnp.zeros((1, nq4), F32), jnp.zeros((HEAD_DIM, nq4), F32))
        carry = lax.fori_loop(0, half // 2, sel_body, init)
        _, l_s, acc_s = lax.cond(
            half % 2 == 1,
            lambda cr: tiles_step(cr, (half - 1, half), kcol, vcol, sel_causal_mask),
            lambda cr: tiles_step(cr, (half,), kcol, vcol, sel_causal_mask), carry)

        wk = slice(256 + h * HEAD_DIM, 256 + (h + 1) * HEAD_DIM)
        wv = slice(384 + h * HEAD_DIM, 384 + (h + 1) * HEAD_DIM)
        n_wp = WINDOW // 256
        _, l_w, acc_w = tiles_step(init, tuple(half - d for d in range(n_wp, -1, -1)), wk, wv, win_mask)

        def gate_row(j):
            return jnp.concatenate([gate_t[3 * hd + j:3 * hd + j + 1, :] for hd in heads], axis=1)

        o_t = gate_row(0) * o_c + gate_row(1) * (acc_s / l_s) + gate_row(2) * (acc_w / l_w)
        o = o_t.T
        for g, hd in enumerate(heads):
            o_ref[hd] = o[g * qb:(g + 1) * qb]


def _nsa_prompt(q, gate, kcvc, kvb, nb, t_len):
    assert t_len % 256 == 0
    nq = t_len // Q_BLOCK
    n_cmp = kcvc.shape[1]
    n_slc = -(-t_len // SLC_BLOCK)
    nbl = -(-n_slc // 128) * 128
    pool = (jnp.arange(nbl)[:, None] == jnp.arange(n_cmp)[None, :] // (SLC_BLOCK // CMP_STRIDE)).astype(BF16)
    return pl.pallas_call(
        functools.partial(_nsa_prompt_kernel_t, n_slc=n_slc),
        grid=(nb, nq),
        in_specs=[pl.BlockSpec((Q_BLOCK, C_WIDTH), lambda b, i: (b * nq + i, 0)),
                  pl.BlockSpec((Q_BLOCK, 128), lambda b, i: (b * nq + i, 0)),
                  pl.BlockSpec((None, n_cmp, 256), lambda b, i: (b, 0, 0)),
                  pl.BlockSpec((t_len, 512), lambda b, i: (b, 0)),
                  pl.BlockSpec((nbl, n_cmp), lambda b, i: (0, 0))],
        out_specs=pl.BlockSpec((None, C_HEADS, Q_BLOCK, HEAD_DIM), lambda b, i: (b, 0, i, 0)),
        out_shape=jax.ShapeDtypeStruct((nb, C_HEADS, t_len, HEAD_DIM), F32),
        compiler_params=_cparams(("parallel", "arbitrary")),
        name="nsa_prompt",
    )(q, gate, kcvc, kvb, pool)


SLC_PAGES = 16


def _nsa_sample_kernel(pt_ref, *refs, pg, dec_seq, past_len, n_buf):
    pages = refs[:pg]
    (q_ref, gate_ref, kc_ref, new_ref, buf_ref, pool_ref, o_ref,
     sel_scr, ocw_scr, m_scr, l_scr, acc_scr) = refs[pg:]
    i = pl.program_id(1)
    n_parts = pl.num_programs(1)
    rows = C_GROUP * dec_seq
    n_cmp = kc_ref.shape[0]
    n_blk = pool_ref.shape[1]
    ridx = lax.broadcasted_iota(jnp.int32, (rows, 1), 0)
    tq = ridx % dec_seq
    t_abs = past_len + tq
    lane128 = lax.broadcasted_iota(jnp.int32, (1, 128), 1)

    def slope_of(h):
        s = jnp.full((rows, 1), _alibi_slope(C_GROUP * h), F32)
        for g in range(1, C_GROUP):
            s = jnp.where(ridx // dec_seq == g, _alibi_slope(C_GROUP * h + g), s)
        return s

    @pl.when(i == 0)
    def _():
        same_tok = (lax.broadcasted_iota(jnp.int32, (rows, rows), 0) % dec_seq
                    == lax.broadcasted_iota(jnp.int32, (rows, rows), 1) % dec_seq).astype(BF16)
        for h in range(C_KV_HEADS):
            slope = slope_of(h)
            qh = (q_ref[h] * (HEAD_DIM ** -0.5)).astype(BF16)
            kcol = slice(h * HEAD_DIM, (h + 1) * HEAD_DIM)
            vcol = slice(128 + h * HEAD_DIM, 128 + (h + 1) * HEAD_DIM)
            c_dist = t_abs - (lax.broadcasted_iota(jnp.int32, (1, n_cmp), 1) * CMP_STRIDE + (2 * CMP_STRIDE - 1))
            s_c = _dot_nt(qh, kc_ref[:, kcol]) - slope * c_dist.astype(F32)
            p_c = _masked_softmax(s_c, c_dist >= 0)
            o_c = jnp.dot(p_c.astype(BF16), kc_ref[:, vcol], preferred_element_type=F32)
            hi = p_c.astype(BF16)
            lo = (p_c - hi.astype(F32)).astype(BF16)
            psum = (jnp.dot(same_tok, hi, preferred_element_type=F32)
                    + jnp.dot(same_tok, lo, preferred_element_type=F32))
            imp = _dot2(psum, pool_ref[...])
            blk = lax.broadcasted_iota(jnp.int32, (1, n_blk), 1)
            imp = jnp.where(blk == 0, POS_BIG, imp)
            sel_scr[h] = _topk_mask(imp, min(SLC_TOPK, n_blk + 1) - 1)
            wk = slice(256 + h * HEAD_DIM, 256 + (h + 1) * HEAD_DIM)
            wv = slice(384 + h * HEAD_DIM, 384 + (h + 1) * HEAD_DIM)
            d_buf = (n_buf + tq) - lax.broadcasted_iota(jnp.int32, (1, n_buf), 1)
            s_b = _dot(qh, buf_ref[0, h]) - slope * d_buf.astype(F32)
            carry = _flash_step(_flash_init(rows), s_b, (d_buf >= 0) & (d_buf < WINDOW), buf_ref[1, h], True)
            d_new = tq - lane128
            s_n = _dot_nt(qh, new_ref[:, wk]) - slope * d_new.astype(F32)
            _, l_w, acc_w = _flash_step(carry, s_n, (d_new >= 0) & (lane128 < dec_seq),
                                        new_ref[:, wv].astype(BF16))
            gt = gate_ref[h]
            ocw_scr[h] = gt[:, 0:1] * o_c + gt[:, 2:3] * (acc_w / l_w)
            m0, l0, a0 = _flash_init(rows)
            m_scr[h] = m0
            l_scr[h] = l0
            acc_scr[h] = a0

    nk = pg * PAGE
    key_pos = i * nk + lax.broadcasted_iota(jnp.int32, (1, nk), 1)
    expand = (lax.broadcasted_iota(jnp.int32, (n_blk, nk), 0)
              == i * (nk // SLC_BLOCK) + lax.broadcasted_iota(jnp.int32, (n_blk, nk), 1) // SLC_BLOCK).astype(BF16)
    for h in range(C_KV_HEADS):
        slope = slope_of(h)
        qh = (q_ref[h] * (HEAD_DIM ** -0.5)).astype(BF16)
        k_all = jnp.concatenate([r[0, h] for r in pages], axis=1).astype(BF16)
        v_all = jnp.concatenate([r[1, h] for r in pages], axis=1).astype(BF16)
        dist = t_abs - key_pos
        selx = jnp.dot(sel_scr[h].astype(BF16), expand, preferred_element_type=F32)
        s = jnp.dot(qh, k_all, preferred_element_type=F32) - slope * dist.astype(F32)
        carry = _flash_step((m_scr[h], l_scr[h], acc_scr[h]), s, (dist >= 0) & (selx > 0.5), v_all, True)
        m_scr[h], l_scr[h], acc_scr[h] = carry

    @pl.when(i == n_parts - 1)
    def _():
        for h in range(C_KV_HEADS):
            slope = slope_of(h)
            qh = (q_ref[h] * (HEAD_DIM ** -0.5)).astype(BF16)
            kcol = slice(h * HEAD_DIM, (h + 1) * HEAD_DIM)
            vcol = slice(128 + h * HEAD_DIM, 128 + (h + 1) * HEAD_DIM)
            d_new = tq - lane128
            s_n = _dot_nt(qh, new_ref[:, kcol]) - slope * d_new.astype(F32)
            _, l_s, acc_s = _flash_step((m_scr[h], l_scr[h], acc_scr[h]), s_n,
                                        (d_new >= 0) & (lane128 < dec_seq), new_ref[:, vcol].astype(BF16))
            o_ref[h] = ocw_scr[h] + gate_ref[h][:, 1:2] * (acc_s / l_s)


def _nsa_sample(page_table, q16, gate16, kcvc, cache_slc_t, new_rows, win_buf_t, layer, dec_seq):
    ns, n_pages = page_table.shape
    pg = min(SLC_PAGES, n_pages)
    past_len = n_pages * PAGE
    n_buf = win_buf_t.shape[-1]
    n_cmp = kcvc.shape[1]
    n_blk = past_len // SLC_BLOCK
    rows = C_GROUP * dec_seq
    pool = (jnp.arange(n_cmp)[:, None] // (SLC_BLOCK // CMP_STRIDE) == jnp.arange(n_blk)[None, :]).astype(BF16)
    per_seq = lambda shape: pl.BlockSpec((None,) + shape, lambda s, i, pt: (s,) + (0,) * len(shape))

    kv_tile = (2, C_KV_HEADS, HEAD_DIM)

    def page_spec(j):
        return pl.BlockSpec((None, None) + kv_tile + (PAGE,),
                            lambda s, i, pt: (layer, pt[s, i * pg + j], 0, 0, 0, 0))

    return pl.pallas_call(
        functools.partial(_nsa_sample_kernel, pg=pg, dec_seq=dec_seq, past_len=past_len, n_buf=n_buf),
        grid_spec=pltpu.PrefetchScalarGridSpec(
            num_scalar_prefetch=1,
            grid=(ns, n_pages // pg),
            in_specs=[page_spec(j) for j in range(pg)]
            + [per_seq((C_KV_HEADS, rows, HEAD_DIM)), per_seq((C_KV_HEADS, rows, 3)), per_seq((n_cmp, 256)),
               per_seq((128, 512)),
               pl.BlockSpec((None, None) + kv_tile + (n_buf,), lambda s, i, pt: (layer, s, 0, 0, 0, 0)),
               pl.BlockSpec((n_cmp, n_blk), lambda s, i, pt: (0, 0))],
            out_specs=per_seq((C_KV_HEADS, rows, HEAD_DIM)),
            scratch_shapes=[pltpu.VMEM((C_KV_HEADS, rows, n_blk), F32), pltpu.VMEM((C_KV_HEADS, rows, HEAD_DIM), F32),
                            pltpu.VMEM((C_KV_HEADS, rows, 1), F32), pltpu.VMEM((C_KV_HEADS, rows, 1), F32),
                            pltpu.VMEM((C_KV_HEADS, rows, HEAD_DIM), F32)],
        ),
        out_shape=jax.ShapeDtypeStruct((ns, C_KV_HEADS, rows, HEAD_DIM), F32),
        compiler_params=_cparams(("parallel", "arbitrary")),
        name="nsa_sample",
    )(page_table, *([cache_slc_t] * pg), q16, gate16, kcvc, new_rows, win_buf_t, pool)


def _outproj_kernel(x_ref, ya_ref, yb_ref, yc_ref, w_ref, o_ref):
    acc = x_ref[...] + _dot(ya_ref[...], w_ref[0:256, :]) + _dot(yb_ref[...], w_ref[256:512, :])
    for hd in range(C_HEADS):
        r0 = 512 + hd * HEAD_DIM
        acc = acc + _dot(yc_ref[hd], w_ref[r0:r0 + HEAD_DIM, :])
    o_ref[...] = acc


def _outproj(x, ya, yb, yc, w_out, nb, t_len, tm):
    nt = t_len // tm
    rows = lambda w: pl.BlockSpec((tm, w), lambda b, i: (b * nt + i, 0))
    return pl.pallas_call(
        _outproj_kernel,
        grid=(nb, nt),
        in_specs=[rows(1024), rows(256), rows(256),
                  pl.BlockSpec((None, C_HEADS, tm, HEAD_DIM), lambda b, i: (b, 0, i, 0)),
                  pl.BlockSpec((1024, 1024), lambda b, i: (0, 0))],
        out_specs=rows(1024),
        out_shape=jax.ShapeDtypeStruct(x.shape, F32),
        compiler_params=_cparams(("parallel", "parallel")),
        name="outproj",
    )(x, ya, yb, yc, w_out)


def _top_vals_rows(x, k, rows_out):
    cols = x.shape[1]
    orow = lax.broadcasted_iota(jnp.int32, (rows_out, cols), 0)
    acc = jnp.zeros((rows_out, cols), F32)
    for j in range(k):
        m = jnp.max(x, axis=0, keepdims=True)
        x = jnp.where(x == m, NEG_INF, x)
        acc = jnp.where(orow == j, m, acc)
    return acc


def _peer_route_kernel(x_ref, g_ref, wq_ref, qg_ref, sk_ref, h2_ref, e1_ref, e2_ref, th_ref):
    x = x_ref[...]
    hb = (x * lax.rsqrt(jnp.mean(x * x, axis=-1, keepdims=True) + 1e-6) * g_ref[...]).astype(BF16)
    h2_ref[...] = hb
    qp = jnp.dot(hb, wq_ref[...], preferred_element_type=F32)
    k = PEER_TOPK
    for hd in range(PEER_HEADS):
        qn = []
        for c in range(2):
            qc = qp[:, (2 * hd + c) * N_KEYS:(2 * hd + c + 1) * N_KEYS]
            qn.append((qc * lax.rsqrt(jnp.mean(qc * qc, axis=-1, keepdims=True) + 1e-6)
                       * qg_ref[:, c * 128:(c + 1) * 128]).astype(BF16))
        for ts in range(x.shape[0] // 128):
            tsl = slice(ts * 128, (ts + 1) * 128)
            st = [_dot_nt(sk_ref[hd, c], qn[c][tsl]) for c in range(2)]
            top = [_top_vals_rows(s, k + 1, 24) for s in st]
            cand = jnp.concatenate([top[0][a:a + 1, :] + top[1][0:k, :] for a in range(k)], axis=0)
            best = _top_vals_rows(cand, k + 1, 24)
            nxt = jnp.maximum(best[k:k + 1, :], jnp.maximum(top[0][k:k + 1, :] + top[1][0:1, :],
                                                            top[0][0:1, :] + top[1][k:k + 1, :]))
            tau = 0.5 * (best[k - 1:k, :] + nxt)
            z = jnp.sum(jnp.where(cand >= tau, jnp.exp(cand - best[0:1, :]), 0.0), axis=0, keepdims=True)
            m2 = top[1][0:1, :]
            e1_ref[hd, :, tsl] = jnp.exp(st[0] - top[0][0:1, :]) / z
            e2_ref[hd, :, tsl] = jnp.exp(st[1] - m2)
            th_ref[hd, :, tsl] = jnp.exp((tau - m2) - st[0])


def _peer_dense_kernel(x_ref, h2_ref, u_ref, vt_ref, e1_ref, e2_ref, th_ref, o_ref, acc_ref, act_ref, w_ref, *, ea):
    j = pl.program_id(1)
    tm = h2_ref.shape[0]

    @pl.when(j == 0)
    def _():
        acc_ref[...] = jnp.zeros(acc_ref.shape, F32)

    h2 = h2_ref[...]
    for aa in range(ea):
        act_ref[aa * N_KEYS:(aa + 1) * N_KEYS, :] = jax.nn.gelu(
            _dot_nt(u_ref[aa * N_KEYS:(aa + 1) * N_KEYS, :], h2))
    hb = N_KEYS // 2
    ga = PEER_GA
    for grp in range(ea // ga):
        firsts = [grp * ga + k for k in range(ga)]
        th_rows = [[th_ref[hd, pl.ds(j * ea + aa, 1), :] for aa in firsts] for hd in range(PEER_HEADS)]
        e1_rows = [[e1_ref[hd, pl.ds(j * ea + aa, 1), :] for aa in firsts] for hd in range(PEER_HEADS)]
        for ts in range(tm // 128):
            tsl = slice(ts * 128, (ts + 1) * 128)
            for bh in range(2):
                bsl = slice(bh * hb, (bh + 1) * hb)
                gates = [jnp.zeros((hb, 128), F32) for _ in range(ga)]
                for hd in range(PEER_HEADS):
                    e2 = e2_ref[hd, bsl, tsl]
                    for k in range(ga):
                        hit = e2 >= th_rows[hd][k][:, tsl]
                        gates[k] = gates[k] + jnp.where(hit, e2, 0.0) * e1_rows[hd][k][:, tsl]
                for k, aa in enumerate(firsts):
                    rsl = slice(aa * N_KEYS + bh * hb, aa * N_KEYS + (bh + 1) * hb)
                    w_ref[rsl, tsl] = (act_ref[rsl, tsl] * gates[k]).astype(BF16)
    acc_ref[...] += jnp.dot(vt_ref[...], w_ref[...], preferred_element_type=F32)

    @pl.when(j == pl.num_programs(1) - 1)
    def _():
        o_ref[...] = x_ref[...] + acc_ref[...].T


PEER_EA = 16
PEER_GA = 4


def _peer(x1, p, tm):
    h2, e1, e2, th = _peer_route(x1, p, tm)
    return _peer_dense(x1, h2, e1, e2, th, p, tm)


def _peer_route(x1, p, tm):
    n = x1.shape[0]
    nt = n // tm
    full = lambda shape: pl.BlockSpec(shape, lambda i: (0,) * len(shape))
    tk = pl.BlockSpec((PEER_HEADS, N_KEYS, tm), lambda i: (0, 0, i))
    sd = jax.ShapeDtypeStruct((PEER_HEADS, N_KEYS, n), F32)
    return pl.pallas_call(
        _peer_route_kernel,
        grid=(nt,),
        in_specs=[pl.BlockSpec((tm, 1024), lambda i: (i, 0)), full((1, 1024)), full((1024, 2048)), full((1, 256)),
                  full((PEER_HEADS, 2, N_KEYS, N_KEYS))],
        out_specs=[pl.BlockSpec((tm, 1024), lambda i: (i, 0)), tk, tk, tk],
        out_shape=[jax.ShapeDtypeStruct((n, 1024), BF16), sd, sd, sd],
        compiler_params=_cparams(("parallel",)),
        name="peer_route",
    )(x1, p["ln2_g"], p["p_wq"], p["p_qnorm_g"], p["p_subkeys"])


def _peer_dense(x1, h2, e1, e2, th, p, tm):
    n = x1.shape[0]
    nt = n // tm
    et = PEER_EA * N_KEYS
    n_exp = p["p_u"].shape[0]
    tk2 = pl.BlockSpec((PEER_HEADS, N_KEYS, tm), lambda i, j: (0, 0, i))
    return pl.pallas_call(
        functools.partial(_peer_dense_kernel, ea=PEER_EA),
        grid=(nt, n_exp // et),
        in_specs=[pl.BlockSpec((tm, 1024), lambda i, j: (i, 0)), pl.BlockSpec((tm, 1024), lambda i, j: (i, 0)),
                  pl.BlockSpec((et, 1024), lambda i, j: (j, 0)), pl.BlockSpec((1024, et), lambda i, j: (0, j)),
                  tk2, tk2, tk2],
        out_specs=pl.BlockSpec((tm, 1024), lambda i, j: (i, 0)),
        out_shape=jax.ShapeDtypeStruct((n, 1024), F32),
        scratch_shapes=[pltpu.VMEM((1024, tm), F32), pltpu.VMEM((et, tm), F32), pltpu.VMEM((et, tm), BF16)],
        compiler_params=_cparams(("parallel", "arbitrary")),
        name="peer_dense",
    )(x1, h2, p["p_u"], p["p_vt"], e1, e2, th)


def _prep_layer(l, ln1_g, ln2_g, w_in, w_out, a_vnorm_g, a_ws, a_bs, b_mu, b_w0, b_w2, b_a0, b_a2, b_g2, b_kk,
                b_ka, b_rk, b_lnx_g, b_lnx_b, c_qnorm_g, c_knorm_g, c_cmp_w1, c_cmp_b1, c_cmp_w2, p_wq,
                p_qnorm_g, p_subkeys, p_u, p_v, dec_seq):
    p = {}
    p["ln1_g"] = ln1_g[l][None, :]
    p["ln2_g"] = ln2_g[l][None, :]
    w = w_in[l]
    z = lambda c: jnp.zeros((w.shape[0], c), w.dtype)
    p["w_in"] = jnp.concatenate([w[:, :1472], z(64), w[:, 1472:], z(104)], axis=1).astype(BF16)
    p["a_vnorm_g"] = a_vnorm_g[l][None, :]
    wm = jnp.where(jnp.tril(jnp.ones((CHUNK, CHUNK), bool)), a_ws[l], 0.0)
    p["wm_prompt"] = wm.astype(BF16)
    p["brow_prompt"] = jnp.repeat(a_bs[l].T, HEAD_DIM, axis=1)
    reps = CHUNK // dec_seq
    eye = jnp.eye(reps, dtype=F32)
    p["wm_sample"] = jnp.stack([jnp.kron(eye, wm[g, :dec_seq, :dec_seq]) for g in range(4)]).astype(BF16)
    p["brow_sample"] = jnp.tile(p["brow_prompt"][:dec_seq], (reps, 1))
    p["c_qnorm_g"] = jnp.tile(c_qnorm_g[l], C_HEADS)[None, :]
    p["c_knorm_g12"] = jnp.stack([jnp.tile(c_knorm_g[l, 1], 2), jnp.tile(c_knorm_g[l, 2], 2)])
    p["bd512"] = _block_ones(512, HEAD_DIM)
    p["bd256"] = _block_ones(256, HEAD_DIM)
    p["bd128"] = _block_ones(128, HEAD_DIM)
    eye2 = jnp.eye(2, dtype=F32)
    w1h = c_cmp_w1[l].reshape(2, 2, CMP_STRIDE, HEAD_DIM, HEAD_DIM)
    p["c_w1e"] = jnp.einsum("kjsdc,kK,hH->jsKHdkhc", w1h, eye2, eye2).reshape(2, 4096, 256).astype(BF16)
    p["c_b1e"] = jnp.broadcast_to(c_cmp_b1[l][:, None, :], (2, 2, HEAD_DIM)).reshape(1, 256)
    p["c_w2e"] = jnp.einsum("kcd,kK,hH->khcKHd", c_cmp_w2[l], eye2, eye2).reshape(256, 256).astype(BF16)
    p["c_kg0"] = jnp.tile(c_knorm_g[l, 0], 2)[None, :]
    row = lambda a: a.reshape(1, -1)
    p["b_mu"] = row(b_mu[l])
    p["b_w0"] = row(b_w0[l])
    p["b_a0"] = row(b_a0[l])
    zz = jnp.zeros((64, 256), F32)
    p["b_w2a"] = jnp.concatenate([jnp.concatenate([b_w2[l], zz], axis=1),
                                  jnp.concatenate([zz, b_a2[l]], axis=1)], axis=0)
    p["b_g2"] = b_g2[l]
    p["b_kk"] = row(b_kk[l])
    p["b_ka"] = row(b_ka[l])
    p["b_rk"] = row(b_rk[l])
    p["b_lnx_g"] = row(b_lnx_g[l])
    p["b_lnx_b"] = row(b_lnx_b[l])
    p["w_out"] = w_out[l].astype(BF16)
    p["p_wq"] = p_wq[l].astype(BF16)
    p["p_qnorm_g"] = row(p_qnorm_g[l])
    p["p_subkeys"] = p_subkeys[l].astype(BF16)
    p["p_u"] = p_u[l].astype(BF16)
    p["p_vt"] = p_v[l].T.astype(BF16)
    return p


def _st_in(wkv):
    b = wkv.shape[0]
    return wkv.transpose(0, 3, 1, 2).reshape(b, HEAD_DIM, 256)


def _st_out(st):
    b = st.shape[0]
    return st.reshape(b, HEAD_DIM, B_HEADS, HEAD_DIM).transpose(0, 2, 3, 1)


def kernel(x_prompt, x_sample, cache_cmp_kv, cache_slc_kv, cache_win_kv, state_wkv, state_shift, page_table, ln1_g, ln2_g, w_in, w_out, a_vnorm_g, a_ws, a_bs, b_mu, b_w0, b_w2, b_a0, b_a2, b_g2, b_kk, b_ka, b_rk, b_lnx_g, b_lnx_b, c_qnorm_g, c_knorm_g, c_cmp_w1, c_cmp_b1, c_cmp_w2, p_wq, p_qnorm_g, p_subkeys, p_u, p_v):
    weights = (ln1_g, ln2_g, w_in, w_out, a_vnorm_g, a_ws, a_bs, b_mu, b_w0, b_w2, b_a0, b_a2, b_g2, b_kk, b_ka,
               b_rk, b_lnx_g, b_lnx_b, c_qnorm_g, c_knorm_g, c_cmp_w1, c_cmp_b1, c_cmp_w2, p_wq, p_qnorm_g,
               p_subkeys, p_u, p_v)
    nb, t_len, d_model = x_prompt.shape
    ns, dec_seq, _ = x_sample.shape
    depth = ln1_g.shape[0]
    n_pool = cache_cmp_kv.shape[1]
    n_s = ns * dec_seq
    n_sp = -(-n_s // 128) * 128
    rwkv_c = 64
    dec_c = 8
    pad_rows = lambda a, n: jnp.pad(a, ((0, n - a.shape[0]),) + ((0, 0),) * (a.ndim - 1))
    xp = x_prompt.reshape(nb * t_len, d_model)
    xs = pad_rows(x_sample.reshape(n_s, d_model), n_sp)
    kv5 = lambda a, b, t: a.reshape(b, t, 2, C_KV_HEADS, HEAD_DIM)
    cache_cmp_t = cache_cmp_kv.transpose(0, 1, 3, 4, 5, 2)
    cache_slc_t = cache_slc_kv.transpose(0, 1, 3, 4, 5, 2)
    cache_win_t = cache_win_kv.transpose(0, 1, 3, 4, 5, 2)
    outs_p, outs_s = [], []
    for l in range(depth):
        p = _prep_layer(l, *weights, dec_seq=dec_seq)
        ya, _, bproj, q, cmp, slc, win, gate, kvb = _inproj(xp, p, p["wm_prompt"], p["brow_prompt"], 256)
        bp3 = bproj.reshape(nb, t_len, B_PROJ)
        prev = jnp.concatenate([jnp.zeros((nb, 1, B_PROJ), F32), bp3[:, :-1]], axis=1).reshape(nb * t_len, B_PROJ)
        yb, st = _rwkv(bproj, prev, jnp.zeros((nb, HEAD_DIM, 256), F32), p, nb, t_len // rwkv_c, rwkv_c, rwkv_c)
        kcvc = _compress(cmp, p, nb)
        yc = _nsa_prompt(q, gate, kcvc, kvb, nb, t_len)
        x1 = _outproj(xp, ya, yb, yc, p["w_out"], nb, t_len, 512)
        xp = _peer(x1, p, 512)
        n_win = min(WINDOW, t_len)
        outs_p.append((kv5(cmp, nb, t_len), kv5(slc, nb, t_len), kv5(win, nb, t_len)[:, t_len - n_win:],
                       _st_out(st), bp3[:, -1]))
        sya, sv, sbproj, sq, scmp, sslc, swin, sgate, _ = _inproj(xs, p, p["wm_sample"], p["brow_sample"], 128)
        sb3 = sbproj[:n_s].reshape(ns, dec_seq, B_PROJ)
        sprev = jnp.concatenate([state_shift[l][:, None], sb3[:, :-1]], axis=1)
        pad_c = lambda a: jnp.pad(a, ((0, 0), (0, dec_c - dec_seq), (0, 0))).reshape(ns * dec_c, B_PROJ)
        syb, sst = _rwkv(pad_c(sb3), pad_c(sprev), _st_in(state_wkv[l]), p, ns, 1, dec_c, dec_seq)
        syb = pad_rows(syb.reshape(ns, dec_c, 256)[:, :dec_seq].reshape(n_s, 256), n_sp)
        skc = _compress_paged(cache_cmp_t, page_table, p, l)
        q16 = (sq[:n_s].reshape(ns, dec_seq, C_KV_HEADS, C_GROUP, HEAD_DIM).transpose(0, 2, 3, 1, 4)
               .reshape(ns, C_KV_HEADS, C_GROUP * dec_seq, HEAD_DIM))
        g16 = (sgate[:n_s, :3 * C_HEADS].reshape(ns, dec_seq, C_KV_HEADS, C_GROUP, 3).transpose(0, 2, 3, 1, 4)
               .reshape(ns, C_KV_HEADS, C_GROUP * dec_seq, 3))
        new_rows = jnp.concatenate([sslc[:n_s], swin[:n_s]], axis=1).reshape(ns, dec_seq, 512)
        new_rows = jnp.pad(new_rows, ((0, 0), (0, 128 - dec_seq), (0, 0)))
        win_buf = cache_win_kv[l].reshape(ns, -1, 256)
        so = _nsa_sample(page_table, q16, g16, skc, cache_slc_t, new_rows, cache_win_t, l, dec_seq)
        syc = (so.reshape(ns, C_KV_HEADS, C_GROUP, dec_seq, HEAD_DIM).transpose(1, 2, 0, 3, 4)
               .reshape(C_HEADS, n_s, HEAD_DIM))
        syc = jnp.pad(syc, ((0, 0), (0, n_sp - n_s), (0, 0)))[None]
        sx1 = _outproj(xs, sya, syb, syc, p["w_out"], 1, n_sp, 128)
        xs = _peer(sx1, p, 128)
        swin3 = swin[:n_s].reshape(ns, dec_seq, 256)
        win_new = jnp.concatenate([win_buf, swin3], axis=1)[:, dec_seq:]
        outs_s.append((kv5(scmp[:n_s], ns, dec_seq), kv5(sslc[:n_s], ns, dec_seq),
                       kv5(win_new, ns, win_buf.shape[1]), _st_out(sst), sb3[:, -1],
                       sv[:n_s].reshape(ns, dec_seq, A_WIDTH)))
    stk = lambda lst, i: jnp.stack([s[i] for s in lst], axis=0)
    return (xp.reshape(nb, t_len, d_model), xs[:n_s].reshape(ns, dec_seq, d_model),
            stk(outs_p, 0), stk(outs_p, 1), stk(outs_p, 2), stk(outs_p, 3), stk(outs_p, 4),
            stk(outs_s, 0), stk(outs_s, 1), stk(outs_s, 2), stk(outs_s, 3), stk(outs_s, 4), stk(outs_s, 5))
```

```python
import functools
import math

import jax
import jax.numpy as jnp
from jax import lax
from jax.experimental import pallas as pl
from jax.experimental.pallas import tpu as pltpu

F32 = jnp.float32
BF16 = jnp.bfloat16
HIGHEST = lax.Precision.HIGHEST

HEAD_DIM = 64
CHUNK = 128
A_WIDTH = 256
B_WIDTH = 256
B_HEADS = 4
B_PROJ = 960
C_HEADS = 8
C_KV_HEADS = 2
C_GROUP = 4
C_WIDTH = 512
CMP_STRIDE = 16
SLC_BLOCK = 64
SLC_TOPK = 16
WINDOW = 512
Q_BLOCK = 128
PAGE = 128
PEER_HEADS = 8
N_KEYS = 128
PEER_TOPK = 16
RWKV_GN_EPS = 64e-5
NEG_BIG = -1e30
POS_BIG = 1e30
NEG_INF = float("-inf")
VMEM_LIMIT = 56 * 1024 * 1024


def _cparams(sem):
    return pltpu.CompilerParams(dimension_semantics=sem, vmem_limit_bytes=VMEM_LIMIT)


def _dot(a, b):
    return jnp.dot(a.astype(BF16), b.astype(BF16), preferred_element_type=F32)


def _dot_nt(a, b):
    return lax.dot_general(a.astype(BF16), b.astype(BF16), (((1,), (1,)), ((), ())),
                           preferred_element_type=F32)


def _dot2(x, m):
    hi = x.astype(BF16)
    lo = (x - hi.astype(F32)).astype(BF16)
    return (jnp.dot(hi, m, preferred_element_type=F32) + jnp.dot(lo, m, preferred_element_type=F32))


def _dot3(a, b, dims):
    a_hi = a.astype(BF16)
    b_hi = b.astype(BF16)
    a_lo = (a - a_hi.astype(F32)).astype(BF16)
    b_lo = (b - b_hi.astype(F32)).astype(BF16)
    dg = lambda x, y: lax.dot_general(x, y, (dims, ((), ())), preferred_element_type=F32)
    return dg(a_hi, b_hi) + (dg(a_hi, b_lo) + dg(a_lo, b_hi))


def _hdot(a, b):
    return _dot3(a, b, ((1,), (0,)))


def _hdot_nt(a, b):
    return _dot3(a, b, ((1,), (1,)))


def _hdot_tn(a, b):
    return _dot3(a, b, ((0,), (0,)))


def _block_ones(width, group):
    i = jnp.arange(width)
    return (i[:, None] // group == i[None, :] // group).astype(BF16)


IN_PAD = 2944


def _inproj_kernel(x_ref, g_ref, w_ref, avg_ref, wm_ref, brow_ref, qg_ref, kg_ref, bd_ref,
                   ya_ref, v_ref, b_ref, q_ref, cmp_ref, slc_ref, win_ref, gate_ref, kvb_ref, *, tm):
    x = x_ref[...]
    h = x * lax.rsqrt(jnp.mean(x * x, axis=-1, keepdims=True) + 1e-6) * g_ref[...]
    z = jnp.dot(h.astype(BF16), w_ref[...], preferred_element_type=F32)
    b_ref[...] = z[:, 512:512 + B_PROJ]
    u = jax.nn.gelu(z[:, 0:256])
    gv = jax.nn.gelu(z[:, 256:512])
    v = gv * lax.rsqrt(jnp.mean(gv * gv, axis=-1, keepdims=True) + 1e-6) * avg_ref[...]
    v_ref[...] = v
    lane_g = lax.broadcasted_iota(jnp.int32, (CHUNK, A_WIDTH), 1) // HEAD_DIM
    for c in range(tm // CHUNK):
        vc = v[c * CHUNK:(c + 1) * CHUNK].astype(BF16)
        mixed = brow_ref[...]
        for g in range(4):
            mg = jnp.dot(wm_ref[g], vc, preferred_element_type=F32)
            mixed = mixed + jnp.where(lane_g == g, mg, 0.0)
        ya_ref[c * CHUNK:(c + 1) * CHUNK, :] = u[c * CHUNK:(c + 1) * CHUNK] * mixed
    bd = bd_ref[...]
    q = z[:, 1536:2048]
    q_ref[...] = q * lax.rsqrt(_dot2(q * q, bd) * (1.0 / HEAD_DIM) + 1e-6) * qg_ref[...]
    bd128 = bd[0:128, 0:128]
    cmp_ref[...] = z[:, 2048:2304]
    for j, (o_ref, off) in enumerate(((slc_ref, 2304), (win_ref, 2560))):
        k = z[:, off:off + 128]
        kn = k * lax.rsqrt(_dot2(k * k, bd128) * (1.0 / HEAD_DIM) + 1e-6) * kg_ref[j:j + 1, :]
        vv = z[:, off + 128:off + 256]
        o_ref[:, 0:128] = kn
        o_ref[:, 128:256] = vv
        kvb_ref[:, j * 256:j * 256 + 128] = kn.astype(BF16)
        kvb_ref[:, j * 256 + 128:j * 256 + 256] = vv.astype(BF16)
    gate_ref[...] = jax.nn.sigmoid(z[:, 2816:2944])


def _inproj(x, p, wm, brow, tm):
    n = x.shape[0]
    full = lambda shape: pl.BlockSpec(shape, lambda i: (0,) * len(shape))
    rows = lambda w: pl.BlockSpec((tm, w), lambda i: (i, 0))
    outs = [(A_WIDTH, F32), (A_WIDTH, F32), (B_PROJ, F32), (C_WIDTH, F32), (256, F32), (256, F32), (256, F32),
            (128, F32), (512, BF16)]
    return pl.pallas_call(
        functools.partial(_inproj_kernel, tm=tm),
        grid=(n // tm,),
        in_specs=[rows(1024), full((1, 1024)), full((1024, IN_PAD)), full((1, A_WIDTH)), full((4, CHUNK, CHUNK)),
                  full((CHUNK, A_WIDTH)), full((1, C_WIDTH)), full((2, 128)), full((512, 512))],
        out_specs=[rows(w) for w, _ in outs],
        out_shape=[jax.ShapeDtypeStruct((n, w), d) for w, d in outs],
        compiler_params=_cparams(("parallel",)),
        name="inproj",
    )(x, p["ln1_g"], p["w_in"], p["a_vnorm_g"], wm, brow, p["c_qnorm_g"], p["c_knorm_g12"], p["bd512"])


def _rwkv_prep_kernel(x_ref, xp_ref, mu_ref, w0_ref, w2a_ref, a0_ref, g2_ref, kkp_ref, ka_ref, rk_ref, bd_ref,
                      tri_ref, p2_ref, y0_ref, p3_ref, z_ref, g_ref, bonus_ref, *, c, t_valid):
    x = x_ref[...]
    xs = x + (xp_ref[...] - x) * mu_ref[...]
    r = xs[:, 0:256]
    k = xs[:, 256:512]
    v = xs[:, 512:768]
    wa = xs[:, 768:896]
    lane = lax.broadcasted_iota(jnp.int32, wa.shape, 1)
    pre = _hdot(jnp.where(lane < 64, jnp.tanh(wa), wa), w2a_ref[...])
    y = -(w0_ref[...] + pre[:, 0:256])
    softplus = jnp.maximum(y, 0.0) + jnp.log1p(jnp.exp(-jnp.abs(y)))
    ew = jnp.exp(-softplus - 0.5)
    a = jax.nn.sigmoid(a0_ref[...] + pre[:, 256:512])
    g_ref[...] = _hdot(jax.nn.sigmoid(xs[:, 896:960]), g2_ref[...])
    bd = bd_ref[...]
    kkr = k * kkp_ref[...]
    kk = kkr / jnp.maximum(jnp.sqrt(_dot2(kkr * kkr, bd)), 1e-12)
    k2 = k * (1.0 + (a - 1.0) * ka_ref[...])
    bonus_ref[...] = _dot2(r * k2 * rk_ref[...], bd) * v
    if t_valid < c:
        live = lax.broadcasted_iota(jnp.int32, ew.shape, 0) < t_valid
        ew = jnp.where(live, ew, 0.0)
        kk = jnp.where(live, kk, 0.0)
        k2 = jnp.where(live, k2, 0.0)
        v = jnp.where(live, v, 0.0)
    cum = jnp.dot(tri_ref[...], -ew, preferred_element_type=F32, precision=HIGHEST)
    gam = jnp.exp(cum)
    ginv = jnp.exp(-cum)
    g_end = gam[c - 1:c, :]
    alpha_t = -kk * jnp.exp(cum + ew)
    bhat = kk * a * ginv
    khat = k2 * ginv
    rt = r * gam
    kbar = khat * g_end
    bbar = bhat * g_end
    row = lax.broadcasted_iota(jnp.int32, (c, c), 0)
    col = lax.broadcasted_iota(jnp.int32, (c, c), 1)
    eye_c = (row == col).astype(F32)
    r64 = lax.broadcasted_iota(jnp.int32, (HEAD_DIM, HEAD_DIM), 0)
    c64 = lax.broadcasted_iota(jnp.int32, (HEAD_DIM, HEAD_DIM), 1)
    hs = range(B_HEADS)
    sls = [slice(h * HEAD_DIM, (h + 1) * HEAD_DIM) for h in hs]
    al = [alpha_t[:, s] for s in sls]
    bh = [bhat[:, s] for s in sls]
    kh = [khat[:, s] for s in sls]
    rh = [rt[:, s] for s in sls]
    vh = [v[:, s] for s in sls]
    mb = [jnp.where(row > col, _hdot_nt(al[h], bh[h]), 0.0) for h in hs]
    mk = [jnp.where(row > col, _hdot_nt(al[h], kh[h]), 0.0) for h in hs]
    qk = [jnp.where(row >= col, _hdot_nt(rh[h], kh[h]), 0.0) for h in hs]
    qb = [jnp.where(row >= col, _hdot_nt(rh[h], bh[h]), 0.0) for h in hs]
    mkv = [_hdot(mk[h], vh[h]) for h in hs]
    tinv = [eye_c + mb[h] for h in hs]
    pw = mb
    for _ in range(int(math.log2(c)) - 1):
        pw = [_hdot(pw[h], pw[h]) for h in hs]
        tinv = [tinv[h] + _hdot(tinv[h], pw[h]) for h in hs]
    p1 = [_hdot(tinv[h], al[h]) for h in hs]
    u = [_hdot(tinv[h], mkv[h]) for h in hs]
    for h in hs:
        sl = sls[h]
        p2_ref[:, sl] = rh[h] + _hdot(qb[h], p1[h])
        y0_ref[:, sl] = _hdot(qk[h], vh[h]) + _hdot(qb[h], u[h])
        p3_ref[:, sl] = jnp.where(r64 == c64, g_end[:, sl], 0.0) + _hdot_tn(bbar[:, sl], p1[h])
        z_ref[:, sl] = _hdot_tn(kbar[:, sl], vh[h]) + _hdot_tn(bbar[:, sl], u[h])


def _rwkv_seq_kernel(st0_ref, p2_ref, y0_ref, p3_ref, z_ref, g_ref, bonus_ref, lg_ref, lb_ref,
                     y_ref, st_ref, st_scr):
    @pl.when(pl.program_id(1) == 0)
    def _():
        st_scr[...] = st0_ref[...]

    for h in range(B_HEADS):
        sl = slice(h * HEAD_DIM, (h + 1) * HEAD_DIM)
        st = st_scr[:, sl]
        y = _hdot(p2_ref[:, sl], st) + y0_ref[:, sl]
        st_scr[:, sl] = _hdot(p3_ref[:, sl], st) + z_ref[:, sl]
        mu = jnp.mean(y, axis=-1, keepdims=True)
        var = jnp.mean(jnp.square(y - mu), axis=-1, keepdims=True)
        yn = (y - mu) * lax.rsqrt(var + RWKV_GN_EPS) * lg_ref[:, sl] + lb_ref[:, sl]
        y_ref[:, sl] = (yn + bonus_ref[:, sl]) * g_ref[:, sl]
    st_ref[...] = st_scr[...]


def _rwkv(proj, prev, st0, p, nb, nch, c, t_valid):
    n = proj.shape[0]
    full = lambda shape: pl.BlockSpec(shape, lambda b, i: (0,) * len(shape))
    rows = lambda w: pl.BlockSpec((c, w), lambda b, i: (b * nch + i, 0))
    mats = pl.BlockSpec((HEAD_DIM, 256), lambda b, i: (b * nch + i, 0))
    tri = (jnp.arange(c)[:, None] >= jnp.arange(c)[None, :]).astype(F32)
    p2, y0, p3, z, g, bonus = pl.pallas_call(
        functools.partial(_rwkv_prep_kernel, c=c, t_valid=t_valid),
        grid=(nb, nch),
        in_specs=[rows(B_PROJ), rows(B_PROJ), full((1, B_PROJ)), full((1, 256)), full((128, 512)), full((1, 256)),
                  full((64, 256)), full((1, 256)), full((1, 256)), full((1, 256)), full((256, 256)), full((c, c))],
        out_specs=[rows(256), rows(256), mats, mats, rows(256), rows(256)],
        out_shape=[jax.ShapeDtypeStruct((n, 256), F32), jax.ShapeDtypeStruct((n, 256), F32),
                   jax.ShapeDtypeStruct((nb * nch * HEAD_DIM, 256), F32),
                   jax.ShapeDtypeStruct((nb * nch * HEAD_DIM, 256), F32),
                   jax.ShapeDtypeStruct((n, 256), F32), jax.ShapeDtypeStruct((n, 256), F32)],
        compiler_params=_cparams(("parallel", "parallel")),
        name="rwkv_prep",
    )(proj, prev, p["b_mu"], p["b_w0"], p["b_w2a"], p["b_a0"], p["b_g2"], p["b_kk"], p["b_ka"], p["b_rk"],
      p["bd256"], tri)
    st_spec = pl.BlockSpec((None, HEAD_DIM, 256), lambda b, i: (b, 0, 0))
    y, st = pl.pallas_call(
        _rwkv_seq_kernel,
        grid=(nb, nch),
        in_specs=[st_spec, rows(256), rows(256), mats, mats, rows(256), rows(256), full((1, 256)), full((1, 256))],
        out_specs=[rows(256), st_spec],
        out_shape=[jax.ShapeDtypeStruct((n, 256), F32), jax.ShapeDtypeStruct((nb, HEAD_DIM, 256), F32)],
        scratch_shapes=[pltpu.VMEM((HEAD_DIM, 256), F32)],
        compiler_params=_cparams(("parallel", "arbitrary")),
        name="rwkv_seq",
    )(st0, p2, y0, p3, z, g, bonus, p["b_lnx_g"], p["b_lnx_b"])
    return y, st


def _compress_tail(a0, a1, n_out, b1_ref, w2_ref, kg_ref, bd_ref, o_ref):
    n = a1.shape[0]
    hid = jax.nn.gelu(a0 + pltpu.roll(a1, n - 1, 0) + b1_ref[...])[0:n_out]
    out = _dot(hid, w2_ref[...])
    k = out[:, 0:128]
    kn = k * lax.rsqrt(_dot2(k * k, bd_ref[...]) * (1.0 / HEAD_DIM) + 1e-6) * kg_ref[...]
    o_ref[:, 0:128] = kn.astype(BF16)
    o_ref[:, 128:256] = out[:, 128:256].astype(BF16)


def _compress_kernel(x_ref, w1_ref, b1_ref, w2_ref, kg_ref, bd_ref, o_ref):
    x = x_ref[...].astype(BF16)
    a0 = jnp.dot(x, w1_ref[0], preferred_element_type=F32)
    a1 = jnp.dot(x, w1_ref[1], preferred_element_type=F32)
    _compress_tail(a0, a1, x.shape[0], b1_ref, w2_ref, kg_ref, bd_ref, o_ref)


def _compress_paged_kernel(pt_ref, *refs, pg):
    pages = refs[:pg + 1]
    w1_ref, b1_ref, w2_ref, kg_ref, bd_ref, o_ref, rows_scr = refs[pg + 1:]
    for j, r in enumerate(pages):
        rows = r[...].reshape(4 * HEAD_DIM, PAGE).T
        rows_scr[0, j * PAGE:(j + 1) * PAGE, :] = rows[:, 0:128]
        rows_scr[1, j * PAGE:(j + 1) * PAGE, :] = rows[:, 128:256]
    n_piece = (pg + 1) * (PAGE // CMP_STRIDE)
    a0 = jnp.zeros((n_piece, 256), F32)
    a1 = jnp.zeros((n_piece, 256), F32)
    for s in range(CMP_STRIDE):
        xs = jnp.concatenate([rows_scr[hf, pl.ds(s, n_piece, stride=CMP_STRIDE), :] for hf in range(2)],
                             axis=1).astype(BF16)
        a0 = a0 + jnp.dot(xs, w1_ref[0, s * 256:(s + 1) * 256, :], preferred_element_type=F32)
        a1 = a1 + jnp.dot(xs, w1_ref[1, s * 256:(s + 1) * 256, :], preferred_element_type=F32)
    _compress_tail(a0, a1, pg * 8, b1_ref, w2_ref, kg_ref, bd_ref, o_ref)


def _compress_weight_specs(index):
    full = lambda shape: pl.BlockSpec(shape, index(len(shape)))
    return [full((2, 4096, 256)), full((1, 256)), full((256, 256)), full((1, 128)), full((128, 128))]


def _compress(cmp_rows, p, nb):
    n_piece = cmp_rows.shape[0] // nb // CMP_STRIDE
    x = cmp_rows.reshape(nb, n_piece, 4096)
    return pl.pallas_call(
        _compress_kernel,
        grid=(nb,),
        in_specs=[pl.BlockSpec((None, n_piece, 4096), lambda b: (b, 0, 0))]
        + _compress_weight_specs(lambda r: (lambda b: (0,) * r)),
        out_specs=pl.BlockSpec((None, n_piece, 256), lambda b: (b, 0, 0)),
        out_shape=jax.ShapeDtypeStruct((nb, n_piece, 256), BF16),
        compiler_params=_cparams(("parallel",)),
        name="compress",
    )(x, p["c_w1e"], p["c_b1e"], p["c_w2e"], p["c_kg0"], p["bd128"])


CMP_PAGES = 16


def _compress_paged(cache_t, page_table, p, layer):
    ns, n_pages = page_table.shape
    pg = min(CMP_PAGES, n_pages)

    def page_spec(j):
        return pl.BlockSpec((None, None, 2, C_KV_HEADS, HEAD_DIM, PAGE),
                            lambda s, i, pt: (layer, pt[s, jnp.minimum(i * pg + j, n_pages - 1)], 0, 0, 0, 0))

    return pl.pallas_call(
        functools.partial(_compress_paged_kernel, pg=pg),
        grid_spec=pltpu.PrefetchScalarGridSpec(
            num_scalar_prefetch=1,
            grid=(ns, n_pages // pg),
            in_specs=[page_spec(j) for j in range(pg + 1)]
            + _compress_weight_specs(lambda r: (lambda s, i, pt: (0,) * r)),
            out_specs=pl.BlockSpec((None, pg * 8, 256), lambda s, i, pt: (s, i, 0)),
            scratch_shapes=[pltpu.VMEM((2, (pg + 1) * PAGE, 128), F32)],
        ),
        out_shape=jax.ShapeDtypeStruct((ns, n_pages * 8, 256), BF16),
        compiler_params=_cparams(("parallel", "arbitrary")),
        name="compress_paged",
    )(page_table, *([cache_t] * (pg + 1)), p["c_w1e"], p["c_b1e"], p["c_w2e"], p["c_kg0"], p["bd128"])


def _alibi_slope(head):
    return 2.0 ** (-8.0 * (head + 1.0) / C_HEADS)


def _topk_mask(x, k):
    nl = x.shape[-1]
    lane = lax.broadcasted_iota(jnp.int32, x.shape, x.ndim - 1).astype(F32)
    sel = jnp.zeros(x.shape, F32)
    for _ in range(k):
        m = jnp.max(x, axis=-1, keepdims=True)
        idx = jnp.min(jnp.where(x == m, lane, float(nl)), axis=-1, keepdims=True)
        hit = lane == idx
        sel = jnp.where(hit, 1.0, sel)
        x = jnp.where(hit, NEG_INF, x)
    return sel


def _masked_softmax(s, valid):
    s = jnp.where(valid, s, NEG_BIG)
    e = jnp.exp(s - jnp.max(s, axis=-1, keepdims=True))
    return jnp.where(valid, e / jnp.sum(e, axis=-1, keepdims=True), 0.0)


def _flash_step(carry, s, ok, v, v_is_t=False):
    m, l, acc = carry
    if ok is not None:
        s = jnp.where(ok, s, NEG_BIG)
    m_new = jnp.maximum(m, jnp.max(s, axis=-1, keepdims=True))
    alpha = jnp.exp(m - m_new)
    pr = jnp.exp(s - m_new)
    if ok is not None:
        pr = jnp.where(ok, pr, 0.0)
    l = alpha * l + jnp.sum(pr, axis=-1, keepdims=True)
    pv = _dot_nt(pr, v) if v_is_t else jnp.dot(pr.astype(BF16), v, preferred_element_type=F32)
    return m_new, l, alpha * acc + pv


def _flash_init(rows):
    return (jnp.full((rows, 1), NEG_BIG, F32), jnp.zeros((rows, 1), F32), jnp.zeros((rows, HEAD_DIM), F32))


def _topk_mask_rows(x, k):
    n = x.shape[0]
    ridx = lax.broadcasted_iota(jnp.int32, x.shape, 0).astype(F32)
    sel = jnp.zeros(x.shape, F32)
    for _ in range(k):
        m = jnp.max(x, axis=0, keepdims=True)
        idx = jnp.min(jnp.where(x == m, ridx, float(n)), axis=0, keepdims=True)
        hit = ridx == idx
        sel = jnp.where(hit, 1.0, sel)
        x = jnp.where(hit, NEG_INF, x)
    return sel


def _nsa_prompt_kernel_t(q_ref, gate_ref, kc_ref, kvb_ref, pool_ref, o_ref, *, n_slc):
    i = pl.program_id(1)
    qb = Q_BLOCK
    nq4 = C_GROUP * qb
    n_cmp = kc_ref.shape[0]
    nbl = pool_ref.shape[0]
    lane4 = lax.broadcasted_iota(jnp.int32, (1, nq4), 1)
    tq_row = i * qb + lane4 % qb
    tok_row = i * qb + lax.broadcasted_iota(jnp.int32, (1, qb), 1)
    gate_t = gate_ref[...].T
    blk_col = lax.broadcasted_iota(jnp.int32, (nbl, 1), 0)
    rel2 = (lax.broadcasted_iota(jnp.int32, (256, nq4), 0) - lane4 % qb).astype(F32)
    exp_row = lax.broadcasted_iota(jnp.int32, (256, nbl), 0) // SLC_BLOCK
    exp_col = lax.broadcasted_iota(jnp.int32, (256, nbl), 1)
    half = i // 2
    for h in range(C_KV_HEADS):
        heads = [C_GROUP * h + g for g in range(C_GROUP)]
        q_t = (jnp.concatenate([q_ref[:, hd * HEAD_DIM:(hd + 1) * HEAD_DIM] for hd in heads], axis=0)
               * (HEAD_DIM ** -0.5)).T.astype(BF16)
        slope_row = jnp.full((1, nq4), _alibi_slope(heads[0]), F32)
        for g in range(1, C_GROUP):
            slope_row = jnp.where(lane4 // qb == g, _alibi_slope(heads[g]), slope_row)
        kcol = slice(h * HEAD_DIM, (h + 1) * HEAD_DIM)
        vcol = slice(128 + h * HEAD_DIM, 128 + (h + 1) * HEAD_DIM)
        c_dist = tq_row - (lax.broadcasted_iota(jnp.int32, (n_cmp, 1), 0) * CMP_STRIDE + (2 * CMP_STRIDE - 1))
        valid = c_dist >= 0
        s_c = jnp.dot(kc_ref[:, kcol], q_t, preferred_element_type=F32) - slope_row * c_dist.astype(F32)
        s_c = jnp.where(valid, s_c, NEG_BIG)
        e_c = jnp.exp(s_c - jnp.max(s_c, axis=0, keepdims=True))
        p_c = jnp.where(valid, e_c / jnp.sum(e_c, axis=0, keepdims=True), 0.0)
        o_c = lax.dot_general(kc_ref[:, vcol], p_c.astype(BF16), (((0,), (0,)), ((), ())),
                              preferred_element_type=F32)
        psum = p_c[:, 0:qb] + p_c[:, qb:2 * qb] + p_c[:, 2 * qb:3 * qb] + p_c[:, 3 * qb:4 * qb]
        ps_hi = psum.astype(BF16)
        ps_lo = (psum - ps_hi.astype(F32)).astype(BF16)
        imp = (jnp.dot(pool_ref[...], ps_hi, preferred_element_type=F32)
               + jnp.dot(pool_ref[...], ps_lo, preferred_element_type=F32))
        imp = jnp.where(blk_col * SLC_BLOCK <= tok_row, imp, NEG_BIG)
        imp = jnp.where((blk_col == 0) | (blk_col == tok_row // SLC_BLOCK), POS_BIG, imp)
        imp = jnp.where(blk_col < n_slc, imp, NEG_INF)
        sel_f = _topk_mask_rows(imp, min(SLC_TOPK, n_slc))
        used_col = jnp.max(sel_f, axis=1, keepdims=True)
        sel_t = sel_f.astype(BF16)
        bias2 = slope_row * rel2

        def tiles_step(carry, ps, kcols, vcols, mask_fn):
            m, l, acc = carry
            rows, scores, shifts = [], [], []
            m_new = m
            for p in ps:
                r0 = pl.multiple_of(jnp.maximum(p, 0) * 256, 256)
                off = ((2 * p - i) * qb).astype(F32)
                s = jnp.dot(kvb_ref[pl.ds(r0, 256), kcols], q_t, preferred_element_type=F32) + bias2
                s = jnp.where(mask_fn(p, off), s, NEG_BIG)
                shift = slope_row * off
                m_new = jnp.maximum(m_new, jnp.max(s, axis=0, keepdims=True) + shift)
                rows.append(r0)
                scores.append(s)
                shifts.append(shift)
            alpha = jnp.exp(m - m_new)
            l = alpha * l
            acc = alpha * acc
            for r0, s, shift in zip(rows, scores, shifts):
                pr = jnp.exp(s - (m_new - shift))
                l = l + jnp.sum(pr, axis=0, keepdims=True)
                acc = acc + lax.dot_general(kvb_ref[pl.ds(r0, 256), vcols], pr.astype(BF16),
                                            (((0,), (0,)), ((), ())), preferred_element_type=F32)
            return m_new, l, acc

        def sel_mask(p, off):
            expand = (exp_col == 4 * p + exp_row).astype(BF16)
            m1 = jnp.dot(expand, sel_t, preferred_element_type=F32) > 0.5
            return jnp.concatenate([m1] * C_GROUP, axis=1)

        def sel_causal_mask(p, off):
            return sel_mask(p, off) & (rel2 + off <= 0.0)

        def win_mask(p, off):
            d = rel2 + jnp.where(p >= 0, off, float(WINDOW))
            return (d <= 0.0) & (d > -float(WINDOW))

        def sel_body(pp, carry):
            used = jnp.max(jnp.where(blk_col // 8 == pp, used_col, 0.0))
            return lax.cond(used > 0.0, lambda cr: tiles_step(cr, (2 * pp, 2 * pp + 1), kcol, vcol, sel_mask),
                            lambda cr: cr, carry)

        init = (jnp.full((1, nq4), 0.5 * NEG_BIG, F32), jnp.zeros((1, nq4), F32), jnp.zeros((HEAD_DIM, nq4), F32))
        carry = lax.fori_loop(0, half // 2, sel_body, init)
        _, l_s, acc_s = lax.cond(
            half % 2 == 1,
            lambda cr: tiles_step(cr, (half - 1, half), kcol, vcol, sel_causal_mask),
            lambda cr: tiles_step(cr, (half,), kcol, vcol, sel_causal_mask), carry)

        wk = slice(256 + h * HEAD_DIM, 256 + (h + 1) * HEAD_DIM)
        wv = slice(384 + h * HEAD_DIM, 384 + (h + 1) * HEAD_DIM)
        n_wp = WINDOW // 256
        _, l_w, acc_w = tiles_step(init, tuple(half - d for d in range(n_wp, -1, -1)), wk, wv, win_mask)

        def gate_row(j):
            return jnp.concatenate([gate_t[3 * hd + j:3 * hd + j + 1, :] for hd in heads], axis=1)

        o_t = gate_row(0) * o_c + gate_row(1) * (acc_s / l_s) + gate_row(2) * (acc_w / l_w)
        o = o_t.T
        for g, hd in enumerate(heads):
            o_ref[hd] = o[g * qb:(g + 1) * qb]


def _nsa_prompt(q, gate, kcvc, kvb, nb, t_len):
    assert t_len % 256 == 0
    nq = t_len // Q_BLOCK
    n_cmp = kcvc.shape[1]
    n_slc = -(-t_len // SLC_BLOCK)
    nbl = -(-n_slc // 128) * 128
    pool = (jnp.arange(nbl)[:, None] == jnp.arange(n_cmp)[None, :] // (SLC_BLOCK // CMP_STRIDE)).astype(BF16)
    return pl.pallas_call(
        functools.partial(_nsa_prompt_kernel_t, n_slc=n_slc),
        grid=(nb, nq),
        in_specs=[pl.BlockSpec((Q_BLOCK, C_WIDTH), lambda b, i: (b * nq + i, 0)),
                  pl.BlockSpec((Q_BLOCK, 128), lambda b, i: (b * nq + i, 0)),
                  pl.BlockSpec((None, n_cmp, 256), lambda b, i: (b, 0, 0)),
                  pl.BlockSpec((t_len, 512), lambda b, i: (b, 0)),
                  pl.BlockSpec((nbl, n_cmp), lambda b, i: (0, 0))],
        out_specs=pl.BlockSpec((None, C_HEADS, Q_BLOCK, HEAD_DIM), lambda b, i: (b, 0, i, 0)),
        out_shape=jax.ShapeDtypeStruct((nb, C_HEADS, t_len, HEAD_DIM), F32),
        compiler_params=_cparams(("parallel", "arbitrary")),
        name="nsa_prompt",
    )(q, gate, kcvc, kvb, pool)


SLC_PAGES = 16


def _nsa_sample_kernel(pt_ref, *refs, pg, dec_seq, past_len, n_buf):
    pages = refs[:pg]
    (q_ref, gate_ref, kc_ref, new_ref, buf_ref, pool_ref, o_ref,
     sel_scr, ocw_scr, m_scr, l_scr, acc_scr) = refs[pg:]
    i = pl.program_id(1)
    n_parts = pl.num_programs(1)
    rows = C_GROUP * dec_seq
    n_cmp = kc_ref.shape[0]
    n_blk = pool_ref.shape[1]
    ridx = lax.broadcasted_iota(jnp.int32, (rows, 1), 0)
    tq = ridx % dec_seq
    t_abs = past_len + tq
    lane128 = lax.broadcasted_iota(jnp.int32, (1, 128), 1)

    def slope_of(h):
        s = jnp.full((rows, 1), _alibi_slope(C_GROUP * h), F32)
        for g in range(1, C_GROUP):
            s = jnp.where(ridx // dec_seq == g, _alibi_slope(C_GROUP * h + g), s)
        return s

    @pl.when(i == 0)
    def _():
        same_tok = (lax.broadcasted_iota(jnp.int32, (rows, rows), 0) % dec_seq
                    == lax.broadcasted_iota(jnp.int32, (rows, rows), 1) % dec_seq).astype(BF16)
        for h in range(C_KV_HEADS):
            slope = slope_of(h)
            qh = (q_ref[h] * (HEAD_DIM ** -0.5)).astype(BF16)
            kcol = slice(h * HEAD_DIM, (h + 1) * HEAD_DIM)
            vcol = slice(128 + h * HEAD_DIM, 128 + (h + 1) * HEAD_DIM)
            c_dist = t_abs - (lax.broadcasted_iota(jnp.int32, (1, n_cmp), 1) * CMP_STRIDE + (2 * CMP_STRIDE - 1))
            s_c = _dot_nt(qh, kc_ref[:, kcol]) - slope * c_dist.astype(F32)
            p_c = _masked_softmax(s_c, c_dist >= 0)
            o_c = jnp.dot(p_c.astype(BF16), kc_ref[:, vcol], preferred_element_type=F32)
            hi = p_c.astype(BF16)
            lo = (p_c - hi.astype(F32)).astype(BF16)
            psum = (jnp.dot(same_tok, hi, preferred_element_type=F32)
                    + jnp.dot(same_tok, lo, preferred_element_type=F32))
            imp = _dot2(psum, pool_ref[...])
            blk = lax.broadcasted_iota(jnp.int32, (1, n_blk), 1)
            imp = jnp.where(blk == 0, POS_BIG, imp)
            sel_scr[h] = _topk_mask(imp, min(SLC_TOPK, n_blk + 1) - 1)
            wk = slice(256 + h * HEAD_DIM, 256 + (h + 1) * HEAD_DIM)
            wv = slice(384 + h * HEAD_DIM, 384 + (h + 1) * HEAD_DIM)
            d_buf = (n_buf + tq) - lax.broadcasted_iota(jnp.int32, (1, n_buf), 1)
            s_b = _dot(qh, buf_ref[0, h]) - slope * d_buf.astype(F32)
            carry = _flash_step(_flash_init(rows), s_b, (d_buf >= 0) & (d_buf < WINDOW), buf_ref[1, h], True)
            d_new = tq - lane128
            s_n = _dot_nt(qh, new_ref[:, wk]) - slope * d_new.astype(F32)
            _, l_w, acc_w = _flash_step(carry, s_n, (d_new >= 0) & (lane128 < dec_seq),
                                        new_ref[:, wv].astype(BF16))
            gt = gate_ref[h]
            ocw_scr[h] = gt[:, 0:1] * o_c + gt[:, 2:3] * (acc_w / l_w)
            m0, l0, a0 = _flash_init(rows)
            m_scr[h] = m0
            l_scr[h] = l0
            acc_scr[h] = a0

    nk = pg * PAGE
    key_pos = i * nk + lax.broadcasted_iota(jnp.int32, (1, nk), 1)
    expand = (lax.broadcasted_iota(jnp.int32, (n_blk, nk), 0)
              == i * (nk // SLC_BLOCK) + lax.broadcasted_iota(jnp.int32, (n_blk, nk), 1) // SLC_BLOCK).astype(BF16)
    for h in range(C_KV_HEADS):
        slope = slope_of(h)
        qh = (q_ref[h] * (HEAD_DIM ** -0.5)).astype(BF16)
        k_all = jnp.concatenate([r[0, h] for r in pages], axis=1).astype(BF16)
        v_all = jnp.concatenate([r[1, h] for r in pages], axis=1).astype(BF16)
        dist = t_abs - key_pos
        selx = jnp.dot(sel_scr[h].astype(BF16), expand, preferred_element_type=F32)
        s = jnp.dot(qh, k_all, preferred_element_type=F32) - slope * dist.astype(F32)
        carry = _flash_step((m_scr[h], l_scr[h], acc_scr[h]), s, (dist >= 0) & (selx > 0.5), v_all, True)
        m_scr[h], l_scr[h], acc_scr[h] = carry

    @pl.when(i == n_parts - 1)
    def _():
        for h in range(C_KV_HEADS):
            slope = slope_of(h)
            qh = (q_ref[h] * (HEAD_DIM ** -0.5)).astype(BF16)
            kcol = slice(h * HEAD_DIM, (h + 1) * HEAD_DIM)
            vcol = slice(128 + h * HEAD_DIM, 128 + (h + 1) * HEAD_DIM)
            d_new = tq - lane128
            s_n = _dot_nt(qh, new_ref[:, kcol]) - slope * d_new.astype(F32)
            _, l_s, acc_s = _flash_step((m_scr[h], l_scr[h], acc_scr[h]), s_n,
                                        (d_new >= 0) & (lane128 < dec_seq), new_ref[:, vcol].astype(BF16))
            o_ref[h] = ocw_scr[h] + gate_ref[h][:, 1:2] * (acc_s / l_s)


def _nsa_sample(page_table, q16, gate16, kcvc, cache_slc_t, new_rows, win_buf_t, layer, dec_seq):
    ns, n_pages = page_table.shape
    pg = min(SLC_PAGES, n_pages)
    past_len = n_pages * PAGE
    n_buf = win_buf_t.shape[-1]
    n_cmp = kcvc.shape[1]
    n_blk = past_len // SLC_BLOCK
    rows = C_GROUP * dec_seq
    pool = (jnp.arange(n_cmp)[:, None] // (SLC_BLOCK // CMP_STRIDE) == jnp.arange(n_blk)[None, :]).astype(BF16)
    per_seq = lambda shape: pl.BlockSpec((None,) + shape, lambda s, i, pt: (s,) + (0,) * len(shape))

    kv_tile = (2, C_KV_HEADS, HEAD_DIM)

    def page_spec(j):
        return pl.BlockSpec((None, None) + kv_tile + (PAGE,),
                            lambda s, i, pt: (layer, pt[s, i * pg + j], 0, 0, 0, 0))

    return pl.pallas_call(
        functools.partial(_nsa_sample_kernel, pg=pg, dec_seq=dec_seq, past_len=past_len, n_buf=n_buf),
        grid_spec=pltpu.PrefetchScalarGridSpec(
            num_scalar_prefetch=1,
            grid=(ns, n_pages // pg),
            in_specs=[page_spec(j) for j in range(pg)]
            + [per_seq((C_KV_HEADS, rows, HEAD_DIM)), per_seq((C_KV_HEADS, rows, 3)), per_seq((n_cmp, 256)),
               per_seq((128, 512)),
               pl.BlockSpec((None, None) + kv_tile + (n_buf,), lambda s, i, pt: (layer, s, 0, 0, 0, 0)),
               pl.BlockSpec((n_cmp, n_blk), lambda s, i, pt: (0, 0))],
            out_specs=per_seq((C_KV_HEADS, rows, HEAD_DIM)),
            scratch_shapes=[pltpu.VMEM((C_KV_HEADS, rows, n_blk), F32), pltpu.VMEM((C_KV_HEADS, rows, HEAD_DIM), F32),
                            pltpu.VMEM((C_KV_HEADS, rows, 1), F32), pltpu.VMEM((C_KV_HEADS, rows, 1), F32),
                            pltpu.VMEM((C_KV_HEADS, rows, HEAD_DIM), F32)],
        ),
        out_shape=jax.ShapeDtypeStruct((ns, C_KV_HEADS, rows, HEAD_DIM), F32),
        compiler_params=_cparams(("parallel", "arbitrary")),
        name="nsa_sample",
    )(page_table, *([cache_slc_t] * pg), q16, gate16, kcvc, new_rows, win_buf_t, pool)


def _outproj_kernel(x_ref, ya_ref, yb_ref, yc_ref, w_ref, o_ref):
    acc = x_ref[...] + _dot(ya_ref[...], w_ref[0:256, :]) + _dot(yb_ref[...], w_ref[256:512, :])
    for hd in range(C_HEADS):
        r0 = 512 + hd * HEAD_DIM
        acc = acc + _dot(yc_ref[hd], w_ref[r0:r0 + HEAD_DIM, :])
    o_ref[...] = acc


def _outproj(x, ya, yb, yc, w_out, nb, t_len, tm):
    nt = t_len // tm
    rows = lambda w: pl.BlockSpec((tm, w), lambda b, i: (b * nt + i, 0))
    return pl.pallas_call(
        _outproj_kernel,
        grid=(nb, nt),
        in_specs=[rows(1024), rows(256), rows(256),
                  pl.BlockSpec((None, C_HEADS, tm, HEAD_DIM), lambda b, i: (b, 0, i, 0)),
                  pl.BlockSpec((1024, 1024), lambda b, i: (0, 0))],
        out_specs=rows(1024),
        out_shape=jax.ShapeDtypeStruct(x.shape, F32),
        compiler_params=_cparams(("parallel", "parallel")),
        name="outproj",
    )(x, ya, yb, yc, w_out)


def _top_vals_rows(x, k, rows_out):
    cols = x.shape[1]
    orow = lax.broadcasted_iota(jnp.int32, (rows_out, cols), 0)
    acc = jnp.full((rows_out, cols), NEG_INF, F32)
    for j in range(k):
        m = jnp.max(x, axis=0, keepdims=True)
        x = jnp.where(x == m, NEG_INF, x)
        acc = jnp.where(orow == j, m, acc)
    return acc


def _peer_route_kernel(x_ref, g_ref, wq_ref, qg_ref, sk_ref, h2_ref, e1_ref, e2_ref, th_ref):
    x = x_ref[...]
    hb = (x * lax.rsqrt(jnp.mean(x * x, axis=-1, keepdims=True) + 1e-6) * g_ref[...]).astype(BF16)
    h2_ref[...] = hb
    qp = jnp.dot(hb, wq_ref[...], preferred_element_type=F32)
    k = PEER_TOPK
    for hd in range(PEER_HEADS):
        qn = []
        for c in range(2):
            qc = qp[:, (2 * hd + c) * N_KEYS:(2 * hd + c + 1) * N_KEYS]
            qn.append((qc * lax.rsqrt(jnp.mean(qc * qc, axis=-1, keepdims=True) + 1e-6)
                       * qg_ref[:, c * 128:(c + 1) * 128]).astype(BF16))
        for ts in range(x.shape[0] // 128):
            tsl = slice(ts * 128, (ts + 1) * 128)
            st = [_dot_nt(sk_ref[hd, c], qn[c][tsl]) for c in range(2)]
            top = [_top_vals_rows(s, k + 1, 24) for s in st]
            cand = jnp.concatenate([top[0][0:1, :] + top[1]]
                                   + [top[0][a:a + 1, :] + top[1][0:8, :] for a in range(1, k + 1)], axis=0)
            best = _top_vals_rows(cand, k + 1, 24)
            tau = 0.5 * (best[k - 1:k, :] + best[k:k + 1, :])
            z = jnp.sum(jnp.where(cand >= tau, jnp.exp(cand - best[0:1, :]), 0.0), axis=0, keepdims=True)
            m2 = top[1][0:1, :]
            e1_ref[hd, :, tsl] = jnp.exp(st[0] - top[0][0:1, :]) / z
            e2_ref[hd, :, tsl] = jnp.exp(st[1] - m2)
            th_ref[hd, :, tsl] = jnp.exp((tau - m2) - st[0])


def _peer_dense_kernel(x_ref, h2_ref, u_ref, vt_ref, e1_ref, e2_ref, th_ref, o_ref, acc_ref, act_ref, w_ref, *, ea):
    j = pl.program_id(1)
    tm = h2_ref.shape[0]

    @pl.when(j == 0)
    def _():
        acc_ref[...] = jnp.zeros(acc_ref.shape, F32)

    h2 = h2_ref[...]
    hb = N_KEYS // 2
    ga = PEER_GA
    n_grp = ea // ga
    n_ts = tm // 128

    def experts(grp):
        gsl = slice(grp * ga * N_KEYS, (grp + 1) * ga * N_KEYS)
        act_ref[gsl, :] = jax.nn.gelu(_dot_nt(u_ref[gsl, :], h2))

    experts(0)
    for grp in range(n_grp):
        firsts = [grp * ga + k for k in range(ga)]
        th_rows = [[th_ref[hd, pl.ds(j * ea + aa, 1), :] for aa in firsts] for hd in range(PEER_HEADS)]
        e1_rows = [[e1_ref[hd, pl.ds(j * ea + aa, 1), :] for aa in firsts] for hd in range(PEER_HEADS)]
        for ts in range(n_ts):
            if ts == n_ts // 2 and grp + 1 < n_grp:
                experts(grp + 1)
            tsl = slice(ts * 128, (ts + 1) * 128)
            for bh in range(2):
                bsl = slice(bh * hb, (bh + 1) * hb)
                gates = [jnp.zeros((hb, 128), F32) for _ in range(ga)]
                for hd in range(PEER_HEADS):
                    e2 = e2_ref[hd, bsl, tsl]
                    for k in range(ga):
                        hit = e2 >= th_rows[hd][k][:, tsl]
                        gates[k] = gates[k] + jnp.where(hit, e2, 0.0) * e1_rows[hd][k][:, tsl]
                for k, aa in enumerate(firsts):
                    rsl = slice(aa * N_KEYS + bh * hb, aa * N_KEYS + (bh + 1) * hb)
                    w_ref[rsl, tsl] = (act_ref[rsl, tsl] * gates[k]).astype(BF16)
        gsl = slice(grp * ga * N_KEYS, (grp + 1) * ga * N_KEYS)
        acc_ref[...] += jnp.dot(vt_ref[:, gsl], w_ref[gsl, :], preferred_element_type=F32)

    @pl.when(j == pl.num_programs(1) - 1)
    def _():
        o_ref[...] = x_ref[...] + acc_ref[...].T


PEER_EA = 16
PEER_GA = 4


def _peer(x1, p, tm):
    h2, e1, e2, th = _peer_route(x1, p, tm)
    return _peer_dense(x1, h2, e1, e2, th, p, tm)


def _peer_route(x1, p, tm):
    n = x1.shape[0]
    nt = n // tm
    full = lambda shape: pl.BlockSpec(shape, lambda i: (0,) * len(shape))
    tk = pl.BlockSpec((PEER_HEADS, N_KEYS, tm), lambda i: (0, 0, i))
    sd = jax.ShapeDtypeStruct((PEER_HEADS, N_KEYS, n), F32)
    return pl.pallas_call(
        _peer_route_kernel,
        grid=(nt,),
        in_specs=[pl.BlockSpec((tm, 1024), lambda i: (i, 0)), full((1, 1024)), full((1024, 2048)), full((1, 256)),
                  full((PEER_HEADS, 2, N_KEYS, N_KEYS))],
        out_specs=[pl.BlockSpec((tm, 1024), lambda i: (i, 0)), tk, tk, tk],
        out_shape=[jax.ShapeDtypeStruct((n, 1024), BF16), sd, sd, sd],
        compiler_params=_cparams(("parallel",)),
        name="peer_route",
    )(x1, p["ln2_g"], p["p_wq"], p["p_qnorm_g"], p["p_subkeys"])


def _peer_dense(x1, h2, e1, e2, th, p, tm):
    n = x1.shape[0]
    nt = n // tm
    et = PEER_EA * N_KEYS
    n_exp = p["p_u"].shape[0]
    tk2 = pl.BlockSpec((PEER_HEADS, N_KEYS, tm), lambda i, j: (0, 0, i))
    return pl.pallas_call(
        functools.partial(_peer_dense_kernel, ea=PEER_EA),
        grid=(nt, n_exp // et),
        in_specs=[pl.BlockSpec((tm, 1024), lambda i, j: (i, 0)), pl.BlockSpec((tm, 1024), lambda i, j: (i, 0)),
                  pl.BlockSpec((et, 1024), lambda i, j: (j, 0)), pl.BlockSpec((1024, et), lambda i, j: (0, j)),
                  tk2, tk2, tk2],
        out_specs=pl.BlockSpec((tm, 1024), lambda i, j: (i, 0)),
        out_shape=jax.ShapeDtypeStruct((n, 1024), F32),
        scratch_shapes=[pltpu.VMEM((1024, tm), F32), pltpu.VMEM((et, tm), F32), pltpu.VMEM((et, tm), BF16)],
        compiler_params=_cparams(("parallel", "arbitrary")),
        name="peer_dense",
    )(x1, h2, p["p_u"], p["p_vt"], e1, e2, th)


def _prep_layer(l, ln1_g, ln2_g, w_in, w_out, a_vnorm_g, a_ws, a_bs, b_mu, b_w0, b_w2, b_a0, b_a2, b_g2, b_kk,
                b_ka, b_rk, b_lnx_g, b_lnx_b, c_qnorm_g, c_knorm_g, c_cmp_w1, c_cmp_b1, c_cmp_w2, p_wq,
                p_qnorm_g, p_subkeys, p_u, p_v, dec_seq):
    p = {}
    p["ln1_g"] = ln1_g[l][None, :]
    p["ln2_g"] = ln2_g[l][None, :]
    w = w_in[l]
    z = lambda c: jnp.zeros((w.shape[0], c), w.dtype)
    p["w_in"] = jnp.concatenate([w[:, :1472], z(64), w[:, 1472:], z(104)], axis=1).astype(BF16)
    p["a_vnorm_g"] = a_vnorm_g[l][None, :]
    wm = jnp.where(jnp.tril(jnp.ones((CHUNK, CHUNK), bool)), a_ws[l], 0.0)
    p["wm_prompt"] = wm.astype(BF16)
    p["brow_prompt"] = jnp.repeat(a_bs[l].T, HEAD_DIM, axis=1)
    reps = CHUNK // dec_seq
    eye = jnp.eye(reps, dtype=F32)
    p["wm_sample"] = jnp.stack([jnp.kron(eye, wm[g, :dec_seq, :dec_seq]) for g in range(4)]).astype(BF16)
    p["brow_sample"] = jnp.tile(p["brow_prompt"][:dec_seq], (reps, 1))
    p["c_qnorm_g"] = jnp.tile(c_qnorm_g[l], C_HEADS)[None, :]
    p["c_knorm_g12"] = jnp.stack([jnp.tile(c_knorm_g[l, 1], 2), jnp.tile(c_knorm_g[l, 2], 2)])
    p["bd512"] = _block_ones(512, HEAD_DIM)
    p["bd256"] = _block_ones(256, HEAD_DIM)
    p["bd128"] = _block_ones(128, HEAD_DIM)
    eye2 = jnp.eye(2, dtype=F32)
    w1h = c_cmp_w1[l].reshape(2, 2, CMP_STRIDE, HEAD_DIM, HEAD_DIM)
    p["c_w1e"] = jnp.einsum("kjsdc,kK,hH->jsKHdkhc", w1h, eye2, eye2).reshape(2, 4096, 256).astype(BF16)
    p["c_b1e"] = jnp.broadcast_to(c_cmp_b1[l][:, None, :], (2, 2, HEAD_DIM)).reshape(1, 256)
    p["c_w2e"] = jnp.einsum("kcd,kK,hH->khcKHd", c_cmp_w2[l], eye2, eye2).reshape(256, 256).astype(BF16)
    p["c_kg0"] = jnp.tile(c_knorm_g[l, 0], 2)[None, :]
    row = lambda a: a.reshape(1, -1)
    p["b_mu"] = row(b_mu[l])
    p["b_w0"] = row(b_w0[l])
    p["b_a0"] = row(b_a0[l])
    zz = jnp.zeros((64, 256), F32)
    p["b_w2a"] = jnp.concatenate([jnp.concatenate([b_w2[l], zz], axis=1),
                                  jnp.concatenate([zz, b_a2[l]], axis=1)], axis=0)
    p["b_g2"] = b_g2[l]
    p["b_kk"] = row(b_kk[l])
    p["b_ka"] = row(b_ka[l])
    p["b_rk"] = row(b_rk[l])
    p["b_lnx_g"] = row(b_lnx_g[l])
    p["b_lnx_b"] = row(b_lnx_b[l])
    p["w_out"] = w_out[l].astype(BF16)
    p["p_wq"] = p_wq[l].astype(BF16)
    p["p_qnorm_g"] = row(p_qnorm_g[l])
    p["p_subkeys"] = p_subkeys[l].astype(BF16)
    p["p_u"] = p_u[l].astype(BF16)
    p["p_vt"] = p_v[l].T.astype(BF16)
    return p


def _st_in(wkv):
    b = wkv.shape[0]
    return wkv.transpose(0, 3, 1, 2).reshape(b, HEAD_DIM, 256)


def _st_out(st):
    b = st.shape[0]
    return st.reshape(b, HEAD_DIM, B_HEADS, HEAD_DIM).transpose(0, 2, 3, 1)


def kernel(x_prompt, x_sample, cache_cmp_kv, cache_slc_kv, cache_win_kv, state_wkv, state_shift, page_table, ln1_g, ln2_g, w_in, w_out, a_vnorm_g, a_ws, a_bs, b_mu, b_w0, b_w2, b_a0, b_a2, b_g2, b_kk, b_ka, b_rk, b_lnx_g, b_lnx_b, c_qnorm_g, c_knorm_g, c_cmp_w1, c_cmp_b1, c_cmp_w2, p_wq, p_qnorm_g, p_subkeys, p_u, p_v):
    weights = (ln1_g, ln2_g, w_in, w_out, a_vnorm_g, a_ws, a_bs, b_mu, b_w0, b_w2, b_a0, b_a2, b_g2, b_kk, b_ka,
               b_rk, b_lnx_g, b_lnx_b, c_qnorm_g, c_knorm_g, c_cmp_w1, c_cmp_b1, c_cmp_w2, p_wq, p_qnorm_g,
               p_subkeys, p_u, p_v)
    nb, t_len, d_model = x_prompt.shape
    ns, dec_seq, _ = x_sample.shape
    depth = ln1_g.shape[0]
    n_pool = cache_cmp_kv.shape[1]
    n_s = ns * dec_seq
    n_sp = -(-n_s // 128) * 128
    rwkv_c = 64
    dec_c = 8
    pad_rows = lambda a, n: jnp.pad(a, ((0, n - a.shape[0]),) + ((0, 0),) * (a.ndim - 1))
    xp = x_prompt.reshape(nb * t_len, d_model)
    xs = pad_rows(x_sample.reshape(n_s, d_model), n_sp)
    kv5 = lambda a, b, t: a.reshape(b, t, 2, C_KV_HEADS, HEAD_DIM)
    cache_cmp_t = cache_cmp_kv.transpose(0, 1, 3, 4, 5, 2)
    cache_slc_t = cache_slc_kv.transpose(0, 1, 3, 4, 5, 2)
    cache_win_t = cache_win_kv.transpose(0, 1, 3, 4, 5, 2)
    outs_p, outs_s = [], []
    for l in range(depth):
        p = _prep_layer(l, *weights, dec_seq=dec_seq)
        ya, _, bproj, q, cmp, slc, win, gate, kvb = _inproj(xp, p, p["wm_prompt"], p["brow_prompt"], 256)
        bp3 = bproj.reshape(nb, t_len, B_PROJ)
        prev = jnp.concatenate([jnp.zeros((nb, 1, B_PROJ), F32), bp3[:, :-1]], axis=1).reshape(nb * t_len, B_PROJ)
        yb, st = _rwkv(bproj, prev, jnp.zeros((nb, HEAD_DIM, 256), F32), p, nb, t_len // rwkv_c, rwkv_c, rwkv_c)
        kcvc = _compress(cmp, p, nb)
        yc = _nsa_prompt(q, gate, kcvc, kvb, nb, t_len)
        x1 = _outproj(xp, ya, yb, yc, p["w_out"], nb, t_len, 512)
        xp = _peer(x1, p, 512)
        n_win = min(WINDOW, t_len)
        outs_p.append((kv5(cmp, nb, t_len), kv5(slc, nb, t_len), kv5(win, nb, t_len)[:, t_len - n_win:],
                       _st_out(st), bp3[:, -1]))
        sya, sv, sbproj, sq, scmp, sslc, swin, sgate, _ = _inproj(xs, p, p["wm_sample"], p["brow_sample"], 128)
        sb3 = sbproj[:n_s].reshape(ns, dec_seq, B_PROJ)
        sprev = jnp.concatenate([state_shift[l][:, None], sb3[:, :-1]], axis=1)
        pad_c = lambda a: jnp.pad(a, ((0, 0), (0, dec_c - dec_seq), (0, 0))).reshape(ns * dec_c, B_PROJ)
        syb, sst = _rwkv(pad_c(sb3), pad_c(sprev), _st_in(state_wkv[l]), p, ns, 1, dec_c, dec_seq)
        syb = pad_rows(syb.reshape(ns, dec_c, 256)[:, :dec_seq].reshape(n_s, 256), n_sp)
        skc = _compress_paged(cache_cmp_t, page_table, p, l)
        q16 = (sq[:n_s].reshape(ns, dec_seq, C_KV_HEADS, C_GROUP, HEAD_DIM).transpose(0, 2, 3, 1, 4)
               .reshape(ns, C_KV_HEADS, C_GROUP * dec_seq, HEAD_DIM))
        g16 = (sgate[:n_s, :3 * C_HEADS].reshape(ns, dec_seq, C_KV_HEADS, C_GROUP, 3).transpose(0, 2, 3, 1, 4)
               .reshape(ns, C_KV_HEADS, C_GROUP * dec_seq, 3))
        new_rows = jnp.concatenate([sslc[:n_s], swin[:n_s]], axis=1).reshape(ns, dec_seq, 512)
        new_rows = jnp.pad(new_rows, ((0, 0), (0, 128 - dec_seq), (0, 0)))
        win_buf = cache_win_kv[l].reshape(ns, -1, 256)
        so = _nsa_sample(page_table, q16, g16, skc, cache_slc_t, new_rows, cache_win_t, l, dec_seq)
        syc = (so.reshape(ns, C_KV_HEADS, C_GROUP, dec_seq, HEAD_DIM).transpose(1, 2, 0, 3, 4)
               .reshape(C_HEADS, n_s, HEAD_DIM))
        syc = jnp.pad(syc, ((0, 0), (0, n_sp - n_s), (0, 0)))[None]
        sx1 = _outproj(xs, sya, syb, syc, p["w_out"], 1, n_sp, 128)
        xs = _peer(sx1, p, 128)
        swin3 = swin[:n_s].reshape(ns, dec_seq, 256)
        win_new = jnp.concatenate([win_buf, swin3], axis=1)[:, dec_seq:]
        outs_s.append((kv5(scmp[:n_s], ns, dec_seq), kv5(sslc[:n_s], ns, dec_seq),
                       kv5(win_new, ns, win_buf.shape[1]), _st_out(sst), sb3[:, -1],
                       sv[:n_s].reshape(ns, dec_seq, A_WIDTH)))
    stk = lambda lst, i: jnp.stack([s[i] for s in lst], axis=0)
    return (xp.reshape(nb, t_len, d_model), xs[:n_s].reshape(ns, dec_seq, d_model),
            stk(outs_p, 0), stk(outs_p, 1), stk(outs_p, 2), stk(outs_p, 3), stk(outs_p, 4),
            stk(outs_s, 0), stk(outs_s, 1), stk(outs_s, 2), stk(outs_s, 3), stk(outs_s, 4), stk(outs_s, 5))
```

```python
import functools
import math

import jax
import jax.numpy as jnp
from jax import lax
from jax.experimental import pallas as pl
from jax.experimental.pallas import tpu as pltpu

F32 = jnp.float32
BF16 = jnp.bfloat16
HIGHEST = lax.Precision.HIGHEST

HEAD_DIM = 64
CHUNK = 128
A_WIDTH = 256
B_WIDTH = 256
B_HEADS = 4
B_PROJ = 960
C_HEADS = 8
C_KV_HEADS = 2
C_GROUP = 4
C_WIDTH = 512
CMP_STRIDE = 16
SLC_BLOCK = 64
SLC_TOPK = 16
WINDOW = 512
Q_BLOCK = 128
PAGE = 128
PEER_HEADS = 8
N_KEYS = 128
PEER_TOPK = 16
RWKV_GN_EPS = 64e-5
NEG_BIG = -1e30
POS_BIG = 1e30
NEG_INF = float("-inf")
VMEM_LIMIT = 56 * 1024 * 1024


def _cparams(sem):
    return pltpu.CompilerParams(dimension_semantics=sem, vmem_limit_bytes=VMEM_LIMIT)


def _dot(a, b):
    return jnp.dot(a.astype(BF16), b.astype(BF16), preferred_element_type=F32)


def _dot_nt(a, b):
    return lax.dot_general(a.astype(BF16), b.astype(BF16), (((1,), (1,)), ((), ())),
                           preferred_element_type=F32)


def _dot2(x, m):
    hi = x.astype(BF16)
    lo = (x - hi.astype(F32)).astype(BF16)
    return (jnp.dot(hi, m, preferred_element_type=F32) + jnp.dot(lo, m, preferred_element_type=F32))


def _dot3(a, b, dims):
    a_hi = a.astype(BF16)
    b_hi = b.astype(BF16)
    a_lo = (a - a_hi.astype(F32)).astype(BF16)
    b_lo = (b - b_hi.astype(F32)).astype(BF16)
    dg = lambda x, y: lax.dot_general(x, y, (dims, ((), ())), preferred_element_type=F32)
    return dg(a_hi, b_hi) + (dg(a_hi, b_lo) + dg(a_lo, b_hi))


def _hdot(a, b):
    return _dot3(a, b, ((1,), (0,)))


def _hdot_nt(a, b):
    return _dot3(a, b, ((1,), (1,)))


def _hdot_tn(a, b):
    return _dot3(a, b, ((0,), (0,)))


def _block_ones(width, group):
    i = jnp.arange(width)
    return (i[:, None] // group == i[None, :] // group).astype(BF16)


IN_PAD = 2944


def _inproj_kernel(x_ref, g_ref, w_ref, avg_ref, wm_ref, brow_ref, qg_ref, kg_ref, bd_ref,
                   ya_ref, v_ref, b_ref, q_ref, cmp_ref, slc_ref, win_ref, gate_ref, kvb_ref, *, tm):
    x = x_ref[...]
    h = x * lax.rsqrt(jnp.mean(x * x, axis=-1, keepdims=True) + 1e-6) * g_ref[...]
    z = jnp.dot(h.astype(BF16), w_ref[...], preferred_element_type=F32)
    b_ref[...] = z[:, 512:512 + B_PROJ]
    u = jax.nn.gelu(z[:, 0:256])
    gv = jax.nn.gelu(z[:, 256:512])
    v = gv * lax.rsqrt(jnp.mean(gv * gv, axis=-1, keepdims=True) + 1e-6) * avg_ref[...]
    v_ref[...] = v
    lane_g = lax.broadcasted_iota(jnp.int32, (CHUNK, A_WIDTH), 1) // HEAD_DIM
    for c in range(tm // CHUNK):
        vc = v[c * CHUNK:(c + 1) * CHUNK].astype(BF16)
        mixed = brow_ref[...]
        for g in range(4):
            mg = jnp.dot(wm_ref[g], vc, preferred_element_type=F32)
            mixed = mixed + jnp.where(lane_g == g, mg, 0.0)
        ya_ref[c * CHUNK:(c + 1) * CHUNK, :] = u[c * CHUNK:(c + 1) * CHUNK] * mixed
    bd = bd_ref[...]
    q = z[:, 1536:2048]
    q_ref[...] = q * lax.rsqrt(_dot2(q * q, bd) * (1.0 / HEAD_DIM) + 1e-6) * qg_ref[...]
    bd128 = bd[0:128, 0:128]
    cmp_ref[...] = z[:, 2048:2304]
    for j, (o_ref, off) in enumerate(((slc_ref, 2304), (win_ref, 2560))):
        k = z[:, off:off + 128]
        kn = k * lax.rsqrt(_dot2(k * k, bd128) * (1.0 / HEAD_DIM) + 1e-6) * kg_ref[j:j + 1, :]
        vv = z[:, off + 128:off + 256]
        o_ref[:, 0:128] = kn
        o_ref[:, 128:256] = vv
        kvb_ref[:, j * 256:j * 256 + 128] = kn.astype(BF16)
        kvb_ref[:, j * 256 + 128:j * 256 + 256] = vv.astype(BF16)
    gate_ref[...] = jax.nn.sigmoid(z[:, 2816:2944])


def _inproj(x, p, wm, brow, tm):
    n = x.shape[0]
    full = lambda shape: pl.BlockSpec(shape, lambda i: (0,) * len(shape))
    rows = lambda w: pl.BlockSpec((tm, w), lambda i: (i, 0))
    outs = [(A_WIDTH, F32), (A_WIDTH, F32), (B_PROJ, F32), (C_WIDTH, F32), (256, F32), (256, F32), (256, F32),
            (128, F32), (512, BF16)]
    return pl.pallas_call(
        functools.partial(_inproj_kernel, tm=tm),
        grid=(n // tm,),
        in_specs=[rows(1024), full((1, 1024)), full((1024, IN_PAD)), full((1, A_WIDTH)), full((4, CHUNK, CHUNK)),
                  full((CHUNK, A_WIDTH)), full((1, C_WIDTH)), full((2, 128)), full((512, 512))],
        out_specs=[rows(w) for w, _ in outs],
        out_shape=[jax.ShapeDtypeStruct((n, w), d) for w, d in outs],
        compiler_params=_cparams(("parallel",)),
        name="inproj",
    )(x, p["ln1_g"], p["w_in"], p["a_vnorm_g"], wm, brow, p["c_qnorm_g"], p["c_knorm_g12"], p["bd512"])


def _rwkv_prep_kernel(x_ref, xp_ref, mu_ref, w0_ref, w2a_ref, a0_ref, g2_ref, kkp_ref, ka_ref, rk_ref, bd_ref,
                      tri_ref, p2_ref, y0_ref, p3_ref, z_ref, g_ref, bonus_ref, *, c, t_valid):
    x = x_ref[...]
    xs = x + (xp_ref[...] - x) * mu_ref[...]
    r = xs[:, 0:256]
    k = xs[:, 256:512]
    v = xs[:, 512:768]
    wa = xs[:, 768:896]
    lane = lax.broadcasted_iota(jnp.int32, wa.shape, 1)
    pre = _hdot(jnp.where(lane < 64, jnp.tanh(wa), wa), w2a_ref[...])
    y = -(w0_ref[...] + pre[:, 0:256])
    softplus = jnp.maximum(y, 0.0) + jnp.log1p(jnp.exp(-jnp.abs(y)))
    ew = jnp.exp(-softplus - 0.5)
    a = jax.nn.sigmoid(a0_ref[...] + pre[:, 256:512])
    g_ref[...] = _hdot(jax.nn.sigmoid(xs[:, 896:960]), g2_ref[...])
    bd = bd_ref[...]
    kkr = k * kkp_ref[...]
    kk = kkr / jnp.maximum(jnp.sqrt(_dot2(kkr * kkr, bd)), 1e-12)
    k2 = k * (1.0 + (a - 1.0) * ka_ref[...])
    bonus_ref[...] = _dot2(r * k2 * rk_ref[...], bd) * v
    n_sub = x.shape[0] // c
    if t_valid < c:
        live = lax.broadcasted_iota(jnp.int32, ew.shape, 0) % c < t_valid
        ew = jnp.where(live, ew, 0.0)
        kk = jnp.where(live, kk, 0.0)
        k2 = jnp.where(live, k2, 0.0)
        v = jnp.where(live, v, 0.0)
    cum = jnp.concatenate([jnp.dot(tri_ref[...], -ew[s * c:(s + 1) * c], preferred_element_type=F32,
                                   precision=HIGHEST) for s in range(n_sub)], axis=0)
    gam = jnp.exp(cum)
    ginv = jnp.exp(-cum)
    g_ends = [gam[(s + 1) * c - 1:(s + 1) * c, :] for s in range(n_sub)]
    g_end_rows = jnp.concatenate([jnp.broadcast_to(ge, (c, ge.shape[1])) for ge in g_ends], axis=0)
    alpha_t = -kk * jnp.exp(cum + ew)
    bhat = kk * a * ginv
    khat = k2 * ginv
    rt = r * gam
    kbar = khat * g_end_rows
    bbar = bhat * g_end_rows
    row = lax.broadcasted_iota(jnp.int32, (c, c), 0)
    col = lax.broadcasted_iota(jnp.int32, (c, c), 1)
    eye_c = (row == col).astype(F32)
    r64 = lax.broadcasted_iota(jnp.int32, (HEAD_DIM, HEAD_DIM), 0)
    c64 = lax.broadcasted_iota(jnp.int32, (HEAD_DIM, HEAD_DIM), 1)
    chains = [(s, h) for s in range(n_sub) for h in range(B_HEADS)]
    hs = range(len(chains))
    rws = [slice(s * c, (s + 1) * c) for s, _ in chains]
    sls = [slice(h * HEAD_DIM, (h + 1) * HEAD_DIM) for _, h in chains]
    al = [alpha_t[rws[i], sls[i]] for i in hs]
    bh = [bhat[rws[i], sls[i]] for i in hs]
    kh = [khat[rws[i], sls[i]] for i in hs]
    rh = [rt[rws[i], sls[i]] for i in hs]
    vh = [v[rws[i], sls[i]] for i in hs]
    mb = [jnp.where(row > col, _hdot_nt(al[h], bh[h]), 0.0) for h in hs]
    mk = [jnp.where(row > col, _hdot_nt(al[h], kh[h]), 0.0) for h in hs]
    qk = [jnp.where(row >= col, _hdot_nt(rh[h], kh[h]), 0.0) for h in hs]
    qb = [jnp.where(row >= col, _hdot_nt(rh[h], bh[h]), 0.0) for h in hs]
    mkv = [_hdot(mk[h], vh[h]) for h in hs]
    tinv = [eye_c + mb[h] for h in hs]
    pw = mb
    for _ in range(int(math.log2(c)) - 1):
        pw = [_hdot(pw[h], pw[h]) for h in hs]
        tinv = [tinv[h] + _hdot(tinv[h], pw[h]) for h in hs]
    p1 = [_hdot(tinv[h], al[h]) for h in hs]
    u = [_hdot(tinv[h], mkv[h]) for h in hs]
    for i in hs:
        s = chains[i][0]
        rw, sl = rws[i], sls[i]
        mat_rows = slice(s * HEAD_DIM, (s + 1) * HEAD_DIM)
        p2_ref[rw, sl] = rh[i] + _hdot(qb[i], p1[i])
        y0_ref[rw, sl] = _hdot(qk[i], vh[i]) + _hdot(qb[i], u[i])
        p3_ref[mat_rows, sl] = (jnp.where(r64 == c64, g_ends[s][:, sl], 0.0)
                                + _hdot_tn(bbar[rw, sl], p1[i]))
        z_ref[mat_rows, sl] = _hdot_tn(kbar[rw, sl], vh[i]) + _hdot_tn(bbar[rw, sl], u[i])


def _rwkv_seq_kernel(st0_ref, p2_ref, y0_ref, p3_ref, z_ref, g_ref, bonus_ref, lg_ref, lb_ref,
                     y_ref, st_ref, st_scr):
    @pl.when(pl.program_id(1) == 0)
    def _():
        st_scr[...] = st0_ref[...]

    for h in range(B_HEADS):
        sl = slice(h * HEAD_DIM, (h + 1) * HEAD_DIM)
        st = st_scr[:, sl]
        y = _hdot(p2_ref[:, sl], st) + y0_ref[:, sl]
        st_scr[:, sl] = _hdot(p3_ref[:, sl], st) + z_ref[:, sl]
        mu = jnp.mean(y, axis=-1, keepdims=True)
        var = jnp.mean(jnp.square(y - mu), axis=-1, keepdims=True)
        yn = (y - mu) * lax.rsqrt(var + RWKV_GN_EPS) * lg_ref[:, sl] + lb_ref[:, sl]
        y_ref[:, sl] = (yn + bonus_ref[:, sl]) * g_ref[:, sl]
    st_ref[...] = st_scr[...]


def _rwkv(proj, prev, st0, p, nb, nch, c, t_valid, n_sub=1):
    n = proj.shape[0]
    full = lambda shape: pl.BlockSpec(shape, lambda b, i: (0,) * len(shape))
    rows = lambda w: pl.BlockSpec((c, w), lambda b, i: (b * nch + i, 0))
    mats = pl.BlockSpec((HEAD_DIM, 256), lambda b, i: (b * nch + i, 0))
    nst = nch // n_sub
    rows_p = lambda w: pl.BlockSpec((n_sub * c, w), lambda b, i: (b * nst + i, 0))
    mats_p = pl.BlockSpec((n_sub * HEAD_DIM, 256), lambda b, i: (b * nst + i, 0))
    tri = (jnp.arange(c)[:, None] >= jnp.arange(c)[None, :]).astype(F32)
    p2, y0, p3, z, g, bonus = pl.pallas_call(
        functools.partial(_rwkv_prep_kernel, c=c, t_valid=t_valid),
        grid=(nb, nst),
        in_specs=[rows_p(B_PROJ), rows_p(B_PROJ), full((1, B_PROJ)), full((1, 256)), full((128, 512)),
                  full((1, 256)), full((64, 256)), full((1, 256)), full((1, 256)), full((1, 256)), full((256, 256)),
                  full((c, c))],
        out_specs=[rows_p(256), rows_p(256), mats_p, mats_p, rows_p(256), rows_p(256)],
        out_shape=[jax.ShapeDtypeStruct((n, 256), F32), jax.ShapeDtypeStruct((n, 256), F32),
                   jax.ShapeDtypeStruct((nb * nch * HEAD_DIM, 256), F32),
                   jax.ShapeDtypeStruct((nb * nch * HEAD_DIM, 256), F32),
                   jax.ShapeDtypeStruct((n, 256), F32), jax.ShapeDtypeStruct((n, 256), F32)],
        compiler_params=_cparams(("parallel", "parallel")),
        name="rwkv_prep",
    )(proj, prev, p["b_mu"], p["b_w0"], p["b_w2a"], p["b_a0"], p["b_g2"], p["b_kk"], p["b_ka"], p["b_rk"],
      p["bd256"], tri)
    st_spec = pl.BlockSpec((None, HEAD_DIM, 256), lambda b, i: (b, 0, 0))
    y, st = pl.pallas_call(
        _rwkv_seq_kernel,
        grid=(nb, nch),
        in_specs=[st_spec, rows(256), rows(256), mats, mats, rows(256), rows(256), full((1, 256)), full((1, 256))],
        out_specs=[rows(256), st_spec],
        out_shape=[jax.ShapeDtypeStruct((n, 256), F32), jax.ShapeDtypeStruct((nb, HEAD_DIM, 256), F32)],
        scratch_shapes=[pltpu.VMEM((HEAD_DIM, 256), F32)],
        compiler_params=_cparams(("parallel", "arbitrary")),
        name="rwkv_seq",
    )(st0, p2, y0, p3, z, g, bonus, p["b_lnx_g"], p["b_lnx_b"])
    return y, st


def _compress_tail(a0, a1, n_out, b1_ref, w2_ref, kg_ref, bd_ref, o_ref):
    n = a1.shape[0]
    hid = jax.nn.gelu(a0 + pltpu.roll(a1, n - 1, 0) + b1_ref[...])[0:n_out]
    out = _dot(hid, w2_ref[...])
    k = out[:, 0:128]
    kn = k * lax.rsqrt(_dot2(k * k, bd_ref[...]) * (1.0 / HEAD_DIM) + 1e-6) * kg_ref[...]
    o_ref[:, 0:128] = kn.astype(BF16)
    o_ref[:, 128:256] = out[:, 128:256].astype(BF16)


def _compress_kernel(x_ref, w1_ref, b1_ref, w2_ref, kg_ref, bd_ref, o_ref):
    x = x_ref[...].astype(BF16)
    a0 = jnp.dot(x, w1_ref[0], preferred_element_type=F32)
    a1 = jnp.dot(x, w1_ref[1], preferred_element_type=F32)
    _compress_tail(a0, a1, x.shape[0], b1_ref, w2_ref, kg_ref, bd_ref, o_ref)


def _compress_paged_kernel(pt_ref, *refs, pg):
    pages = refs[:pg + 1]
    w1_ref, b1_ref, w2_ref, kg_ref, bd_ref, perm_ref, o_ref, rows_scr = refs[pg + 1:]
    ppp = PAGE // CMP_STRIDE
    for j, r in enumerate(pages):
        rows = _dot_nt(perm_ref[...], r[...].reshape(4 * HEAD_DIM, PAGE))
        for s in range(CMP_STRIDE):
            rows_scr[s, j * ppp:(j + 1) * ppp, :] = rows[s * ppp:(s + 1) * ppp, :]
    n_piece = (pg + 1) * ppp
    a0 = jnp.zeros((n_piece, 256), F32)
    a1 = jnp.zeros((n_piece, 256), F32)
    for s in range(CMP_STRIDE):
        xs = rows_scr[s].astype(BF16)
        a0 = a0 + jnp.dot(xs, w1_ref[0, s * 256:(s + 1) * 256, :], preferred_element_type=F32)
        a1 = a1 + jnp.dot(xs, w1_ref[1, s * 256:(s + 1) * 256, :], preferred_element_type=F32)
    _compress_tail(a0, a1, pg * 8, b1_ref, w2_ref, kg_ref, bd_ref, o_ref)


def _compress_weight_specs(index):
    full = lambda shape: pl.BlockSpec(shape, index(len(shape)))
    return [full((2, 4096, 256)), full((1, 256)), full((256, 256)), full((1, 128)), full((128, 128))]


def _compress(cmp_rows, p, nb):
    n_piece = cmp_rows.shape[0] // nb // CMP_STRIDE
    x = cmp_rows.reshape(nb, n_piece, 4096)
    return pl.pallas_call(
        _compress_kernel,
        grid=(nb,),
        in_specs=[pl.BlockSpec((None, n_piece, 4096), lambda b: (b, 0, 0))]
        + _compress_weight_specs(lambda r: (lambda b: (0,) * r)),
        out_specs=pl.BlockSpec((None, n_piece, 256), lambda b: (b, 0, 0)),
        out_shape=jax.ShapeDtypeStruct((nb, n_piece, 256), BF16),
        compiler_params=_cparams(("parallel",)),
        name="compress",
    )(x, p["c_w1e"], p["c_b1e"], p["c_w2e"], p["c_kg0"], p["bd128"])


CMP_PAGES = 32


def _compress_paged(cache_t, page_table, p, layer):
    ns, n_pages = page_table.shape
    pg = min(CMP_PAGES, n_pages)
    ppp = PAGE // CMP_STRIDE
    r = jnp.arange(PAGE)
    perm = (r[None, :] == (r[:, None] % ppp) * CMP_STRIDE + r[:, None] // ppp).astype(BF16)

    def page_spec(j):
        return pl.BlockSpec((None, None, 2, C_KV_HEADS, HEAD_DIM, PAGE),
                            lambda s, i, pt: (layer, pt[s, jnp.minimum(i * pg + j, n_pages - 1)], 0, 0, 0, 0))

    return pl.pallas_call(
        functools.partial(_compress_paged_kernel, pg=pg),
        grid_spec=pltpu.PrefetchScalarGridSpec(
            num_scalar_prefetch=1,
            grid=(ns, n_pages // pg),
            in_specs=[page_spec(j) for j in range(pg + 1)]
            + _compress_weight_specs(lambda r: (lambda s, i, pt: (0,) * r))
            + [pl.BlockSpec((PAGE, PAGE), lambda s, i, pt: (0, 0))],
            out_specs=pl.BlockSpec((None, pg * ppp, 256), lambda s, i, pt: (s, i, 0)),
            scratch_shapes=[pltpu.VMEM((CMP_STRIDE, (pg + 1) * ppp, 256), F32)],
        ),
        out_shape=jax.ShapeDtypeStruct((ns, n_pages * ppp, 256), BF16),
        compiler_params=_cparams(("parallel", "arbitrary")),
        name="compress_paged",
    )(page_table, *([cache_t] * (pg + 1)), p["c_w1e"], p["c_b1e"], p["c_w2e"], p["c_kg0"], p["bd128"], perm)


def _alibi_slope(head):
    return 2.0 ** (-8.0 * (head + 1.0) / C_HEADS)


def _topk_mask(x, k):
    nl = x.shape[-1]
    lane = lax.broadcasted_iota(jnp.int32, x.shape, x.ndim - 1).astype(F32)
    sel = jnp.zeros(x.shape, F32)
    for _ in range(k):
        m = jnp.max(x, axis=-1, keepdims=True)
        idx = jnp.min(jnp.where(x == m, lane, float(nl)), axis=-1, keepdims=True)
        hit = lane == idx
        sel = jnp.where(hit, 1.0, sel)
        x = jnp.where(hit, NEG_INF, x)
    return sel


def _masked_softmax(s, valid):
    s = jnp.where(valid, s, NEG_BIG)
    e = jnp.exp(s - jnp.max(s, axis=-1, keepdims=True))
    return jnp.where(valid, e / jnp.sum(e, axis=-1, keepdims=True), 0.0)


def _flash_step(carry, s, ok, v, v_is_t=False):
    m, l, acc = carry
    if ok is not None:
        s = jnp.where(ok, s, NEG_BIG)
    m_new = jnp.maximum(m, jnp.max(s, axis=-1, keepdims=True))
    alpha = jnp.exp(m - m_new)
    pr = jnp.exp(s - m_new)
    if ok is not None:
        pr = jnp.where(ok, pr, 0.0)
    l = alpha * l + jnp.sum(pr, axis=-1, keepdims=True)
    pv = _dot_nt(pr, v) if v_is_t else jnp.dot(pr.astype(BF16), v, preferred_element_type=F32)
    return m_new, l, alpha * acc + pv


def _flash_init(rows):
    return (jnp.full((rows, 1), NEG_BIG, F32), jnp.zeros((rows, 1), F32), jnp.zeros((rows, HEAD_DIM), F32))


def _topk_mask_rows(x, k):
    n = x.shape[0]
    ridx = lax.broadcasted_iota(jnp.int32, x.shape, 0).astype(F32)
    sel = jnp.zeros(x.shape, F32)
    for _ in range(k):
        m = jnp.max(x, axis=0, keepdims=True)
        idx = jnp.min(jnp.where(x == m, ridx, float(n)), axis=0, keepdims=True)
        hit = ridx == idx
        sel = jnp.where(hit, 1.0, sel)
        x = jnp.where(hit, NEG_INF, x)
    return sel


def _nsa_prompt_kernel_t(q_ref, gate_ref, kc_ref, kvb_ref, pool_ref, o_ref, *, n_slc):
    i = pl.program_id(1)
    qb = Q_BLOCK
    nq4 = C_GROUP * qb
    n_cmp = kc_ref.shape[0]
    nbl = pool_ref.shape[0]
    lane4 = lax.broadcasted_iota(jnp.int32, (1, nq4), 1)
    tq_row = i * qb + lane4 % qb
    tok_row = i * qb + lax.broadcasted_iota(jnp.int32, (1, qb), 1)
    gate_t = gate_ref[...].T
    blk_col = lax.broadcasted_iota(jnp.int32, (nbl, 1), 0)
    rel2 = (lax.broadcasted_iota(jnp.int32, (256, nq4), 0) - lane4 % qb).astype(F32)
    exp_row = lax.broadcasted_iota(jnp.int32, (256, nbl), 0) // SLC_BLOCK
    exp_col = lax.broadcasted_iota(jnp.int32, (256, nbl), 1)
    half = i // 2
    for h in range(C_KV_HEADS):
        heads = [C_GROUP * h + g for g in range(C_GROUP)]
        q_t = (jnp.concatenate([q_ref[:, hd * HEAD_DIM:(hd + 1) * HEAD_DIM] for hd in heads], axis=0)
               * (HEAD_DIM ** -0.5)).T.astype(BF16)
        slope_row = jnp.full((1, nq4), _alibi_slope(heads[0]), F32)
        for g in range(1, C_GROUP):
            slope_row = jnp.where(lane4 // qb == g, _alibi_slope(heads[g]), slope_row)
        kcol = slice(h * HEAD_DIM, (h + 1) * HEAD_DIM)
        vcol = slice(128 + h * HEAD_DIM, 128 + (h + 1) * HEAD_DIM)
        c_dist = tq_row - (lax.broadcasted_iota(jnp.int32, (n_cmp, 1), 0) * CMP_STRIDE + (2 * CMP_STRIDE - 1))
        valid = c_dist >= 0
        s_c = jnp.dot(kc_ref[:, kcol], q_t, preferred_element_type=F32) - slope_row * c_dist.astype(F32)
        s_c = jnp.where(valid, s_c, NEG_BIG)
        e_c = jnp.exp(s_c - jnp.max(s_c, axis=0, keepdims=True))
        p_c = jnp.where(valid, e_c / jnp.sum(e_c, axis=0, keepdims=True), 0.0)
        o_c = lax.dot_general(kc_ref[:, vcol], p_c.astype(BF16), (((0,), (0,)), ((), ())),
                              preferred_element_type=F32)
        psum = p_c[:, 0:qb] + p_c[:, qb:2 * qb] + p_c[:, 2 * qb:3 * qb] + p_c[:, 3 * qb:4 * qb]
        ps_hi = psum.astype(BF16)
        ps_lo = (psum - ps_hi.astype(F32)).astype(BF16)
        imp = (jnp.dot(pool_ref[...], ps_hi, preferred_element_type=F32)
               + jnp.dot(pool_ref[...], ps_lo, preferred_element_type=F32))
        imp = jnp.where(blk_col * SLC_BLOCK <= tok_row, imp, NEG_BIG)
        imp = jnp.where((blk_col == 0) | (blk_col == tok_row // SLC_BLOCK), POS_BIG, imp)
        imp = jnp.where(blk_col < n_slc, imp, NEG_INF)
        sel_f = _topk_mask_rows(imp, min(SLC_TOPK, n_slc))
        used_col = jnp.max(sel_f, axis=1, keepdims=True)
        sel_t = sel_f.astype(BF16)
        bias2 = slope_row * rel2

        def tiles_step(carry, ps, kcols, vcols, mask_fn):
            m, l, acc = carry
            rows, scores, shifts = [], [], []
            m_new = m
            for p in ps:
                r0 = pl.multiple_of(jnp.maximum(p, 0) * 256, 256)
                off = ((2 * p - i) * qb).astype(F32)
                s = jnp.dot(kvb_ref[pl.ds(r0, 256), kcols], q_t, preferred_element_type=F32) + bias2
                s = jnp.where(mask_fn(p, off), s, NEG_BIG)
                shift = slope_row * off
                m_new = jnp.maximum(m_new, jnp.max(s, axis=0, keepdims=True) + shift)
                rows.append(r0)
                scores.append(s)
                shifts.append(shift)
            alpha = jnp.exp(m - m_new)
            l = alpha * l
            acc = alpha * acc
            for r0, s, shift in zip(rows, scores, shifts):
                pr = jnp.exp(s - (m_new - shift))
                l = l + jnp.sum(pr, axis=0, keepdims=True)
                acc = acc + lax.dot_general(kvb_ref[pl.ds(r0, 256), vcols], pr.astype(BF16),
                                            (((0,), (0,)), ((), ())), preferred_element_type=F32)
            return m_new, l, acc

        def sel_mask(p, off):
            expand = (exp_col == 4 * p + exp_row).astype(BF16)
            m1 = jnp.dot(expand, sel_t, preferred_element_type=F32) > 0.5
            return jnp.concatenate([m1] * C_GROUP, axis=1)

        def sel_causal_mask(p, off):
            return sel_mask(p, off) & (rel2 + off <= 0.0)

        def win_mask(p, off):
            d = rel2 + jnp.where(p >= 0, off, float(WINDOW))
            return (d <= 0.0) & (d > -float(WINDOW))

        def sel_body(pp, carry):
            used = jnp.max(jnp.where(blk_col // 8 == pp, used_col, 0.0))
            return lax.cond(used > 0.0, lambda cr: tiles_step(cr, (2 * pp, 2 * pp + 1), kcol, vcol, sel_mask),
                            lambda cr: cr, carry)

        init = (jnp.full((1, nq4), 0.5 * NEG_BIG, F32), jnp.zeros((1, nq4), F32), jnp.zeros((HEAD_DIM, nq4), F32))
        carry = lax.fori_loop(0, half // 2, sel_body, init)
        _, l_s, acc_s = lax.cond(
            half % 2 == 1,
            lambda cr: tiles_step(cr, (half - 1, half), kcol, vcol, sel_causal_mask),
            lambda cr: tiles_step(cr, (half,), kcol, vcol, sel_causal_mask), carry)

        wk = slice(256 + h * HEAD_DIM, 256 + (h + 1) * HEAD_DIM)
        wv = slice(384 + h * HEAD_DIM, 384 + (h + 1) * HEAD_DIM)
        n_wp = WINDOW // 256
        _, l_w, acc_w = tiles_step(init, tuple(half - d for d in range(n_wp, -1, -1)), wk, wv, win_mask)

        def gate_row(j):
            return jnp.concatenate([gate_t[3 * hd + j:3 * hd + j + 1, :] for hd in heads], axis=1)

        o_t = gate_row(0) * o_c + gate_row(1) * (acc_s / l_s) + gate_row(2) * (acc_w / l_w)
        o = o_t.T
        for g, hd in enumerate(heads):
            o_ref[hd] = o[g * qb:(g + 1) * qb]


def _nsa_prompt(q, gate, kcvc, kvb, nb, t_len):
    assert t_len % 256 == 0
    nq = t_len // Q_BLOCK
    n_cmp = kcvc.shape[1]
    n_slc = -(-t_len // SLC_BLOCK)
    nbl = -(-n_slc // 128) * 128
    pool = (jnp.arange(nbl)[:, None] == jnp.arange(n_cmp)[None, :] // (SLC_BLOCK // CMP_STRIDE)).astype(BF16)
    return pl.pallas_call(
        functools.partial(_nsa_prompt_kernel_t, n_slc=n_slc),
        grid=(nb, nq),
        in_specs=[pl.BlockSpec((Q_BLOCK, C_WIDTH), lambda b, i: (b * nq + i, 0)),
                  pl.BlockSpec((Q_BLOCK, 128), lambda b, i: (b * nq + i, 0)),
                  pl.BlockSpec((None, n_cmp, 256), lambda b, i: (b, 0, 0)),
                  pl.BlockSpec((t_len, 512), lambda b, i: (b, 0)),
                  pl.BlockSpec((nbl, n_cmp), lambda b, i: (0, 0))],
        out_specs=pl.BlockSpec((None, C_HEADS, Q_BLOCK, HEAD_DIM), lambda b, i: (b, 0, i, 0)),
        out_shape=jax.ShapeDtypeStruct((nb, C_HEADS, t_len, HEAD_DIM), F32),
        compiler_params=_cparams(("parallel", "arbitrary")),
        name="nsa_prompt",
    )(q, gate, kcvc, kvb, pool)


SLC_PAGES = 16


def _nsa_sample_kernel(pt_ref, *refs, pg, dec_seq, past_len, n_buf):
    pages = refs[:pg]
    (q_ref, gate_ref, kc_ref, new_ref, buf_ref, pool_ref, o_ref,
     sel_scr, ocw_scr, m_scr, l_scr, acc_scr) = refs[pg:]
    i = pl.program_id(1)
    n_parts = pl.num_programs(1)
    rows = C_GROUP * dec_seq
    n_cmp = kc_ref.shape[0]
    n_blk = pool_ref.shape[1]
    ridx = lax.broadcasted_iota(jnp.int32, (rows, 1), 0)
    tq = ridx % dec_seq
    t_abs = past_len + tq
    lane128 = lax.broadcasted_iota(jnp.int32, (1, 128), 1)

    def slope_of(h):
        s = jnp.full((rows, 1), _alibi_slope(C_GROUP * h), F32)
        for g in range(1, C_GROUP):
            s = jnp.where(ridx // dec_seq == g, _alibi_slope(C_GROUP * h + g), s)
        return s

    @pl.when(i == 0)
    def _():
        same_tok = (lax.broadcasted_iota(jnp.int32, (rows, rows), 0) % dec_seq
                    == lax.broadcasted_iota(jnp.int32, (rows, rows), 1) % dec_seq).astype(BF16)
        for h in range(C_KV_HEADS):
            slope = slope_of(h)
            qh = (q_ref[h] * (HEAD_DIM ** -0.5)).astype(BF16)
            kcol = slice(h * HEAD_DIM, (h + 1) * HEAD_DIM)
            vcol = slice(128 + h * HEAD_DIM, 128 + (h + 1) * HEAD_DIM)
            c_dist = t_abs - (lax.broadcasted_iota(jnp.int32, (1, n_cmp), 1) * CMP_STRIDE + (2 * CMP_STRIDE - 1))
            s_c = _dot_nt(qh, kc_ref[:, kcol]) - slope * c_dist.astype(F32)
            p_c = _masked_softmax(s_c, c_dist >= 0)
            o_c = jnp.dot(p_c.astype(BF16), kc_ref[:, vcol], preferred_element_type=F32)
            hi = p_c.astype(BF16)
            lo = (p_c - hi.astype(F32)).astype(BF16)
            psum = (jnp.dot(same_tok, hi, preferred_element_type=F32)
                    + jnp.dot(same_tok, lo, preferred_element_type=F32))
            imp = _dot2(psum, pool_ref[...])
            blk = lax.broadcasted_iota(jnp.int32, (1, n_blk), 1)
            imp = jnp.where(blk == 0, POS_BIG, imp)
            sel_scr[h] = _topk_mask(imp, min(SLC_TOPK, n_blk + 1) - 1)
            wk = slice(256 + h * HEAD_DIM, 256 + (h + 1) * HEAD_DIM)
            wv = slice(384 + h * HEAD_DIM, 384 + (h + 1) * HEAD_DIM)
            d_buf = (n_buf + tq) - lax.broadcasted_iota(jnp.int32, (1, n_buf), 1)
            s_b = _dot(qh, buf_ref[0, h]) - slope * d_buf.astype(F32)
            carry = _flash_step(_flash_init(rows), s_b, (d_buf >= 0) & (d_buf < WINDOW), buf_ref[1, h], True)
            d_new = tq - lane128
            s_n = _dot_nt(qh, new_ref[:, wk]) - slope * d_new.astype(F32)
            _, l_w, acc_w = _flash_step(carry, s_n, (d_new >= 0) & (lane128 < dec_seq),
                                        new_ref[:, wv].astype(BF16))
            gt = gate_ref[h]
            ocw_scr[h] = gt[:, 0:1] * o_c + gt[:, 2:3] * (acc_w / l_w)
            m0, l0, a0 = _flash_init(rows)
            m_scr[h] = m0
            l_scr[h] = l0
            acc_scr[h] = a0

    nk = pg * PAGE
    key_pos = i * nk + lax.broadcasted_iota(jnp.int32, (1, nk), 1)
    expand = (lax.broadcasted_iota(jnp.int32, (n_blk, nk), 0)
              == i * (nk // SLC_BLOCK) + lax.broadcasted_iota(jnp.int32, (n_blk, nk), 1) // SLC_BLOCK).astype(BF16)
    for h in range(C_KV_HEADS):
        slope = slope_of(h)
        qh = (q_ref[h] * (HEAD_DIM ** -0.5)).astype(BF16)
        k_all = jnp.concatenate([r[0, h] for r in pages], axis=1).astype(BF16)
        v_all = jnp.concatenate([r[1, h] for r in pages], axis=1).astype(BF16)
        dist = t_abs - key_pos
        selx = jnp.dot(sel_scr[h].astype(BF16), expand, preferred_element_type=F32)
        s = jnp.dot(qh, k_all, preferred_element_type=F32) - slope * dist.astype(F32)
        carry = _flash_step((m_scr[h], l_scr[h], acc_scr[h]), s, (dist >= 0) & (selx > 0.5), v_all, True)
        m_scr[h], l_scr[h], acc_scr[h] = carry

    @pl.when(i == n_parts - 1)
    def _():
        for h in range(C_KV_HEADS):
            slope = slope_of(h)
            qh = (q_ref[h] * (HEAD_DIM ** -0.5)).astype(BF16)
            kcol = slice(h * HEAD_DIM, (h + 1) * HEAD_DIM)
            vcol = slice(128 + h * HEAD_DIM, 128 + (h + 1) * HEAD_DIM)
            d_new = tq - lane128
            s_n = _dot_nt(qh, new_ref[:, kcol]) - slope * d_new.astype(F32)
            _, l_s, acc_s = _flash_step((m_scr[h], l_scr[h], acc_scr[h]), s_n,
                                        (d_new >= 0) & (lane128 < dec_seq), new_ref[:, vcol].astype(BF16))
            o_ref[h] = ocw_scr[h] + gate_ref[h][:, 1:2] * (acc_s / l_s)


def _nsa_sample(page_table, q16, gate16, kcvc, cache_slc_t, new_rows, win_buf_t, layer, dec_seq):
    ns, n_pages = page_table.shape
    pg = min(SLC_PAGES, n_pages)
    past_len = n_pages * PAGE
    n_buf = win_buf_t.shape[-1]
    n_cmp = kcvc.shape[1]
    n_blk = past_len // SLC_BLOCK
    rows = C_GROUP * dec_seq
    pool = (jnp.arange(n_cmp)[:, None] // (SLC_BLOCK // CMP_STRIDE) == jnp.arange(n_blk)[None, :]).astype(BF16)
    per_seq = lambda shape: pl.BlockSpec((None,) + shape, lambda s, i, pt: (s,) + (0,) * len(shape))

    kv_tile = (2, C_KV_HEADS, HEAD_DIM)

    def page_spec(j):
        return pl.BlockSpec((None, None) + kv_tile + (PAGE,),
                            lambda s, i, pt: (layer, pt[s, i * pg + j], 0, 0, 0, 0))

    return pl.pallas_call(
        functools.partial(_nsa_sample_kernel, pg=pg, dec_seq=dec_seq, past_len=past_len, n_buf=n_buf),
        grid_spec=pltpu.PrefetchScalarGridSpec(
            num_scalar_prefetch=1,
            grid=(ns, n_pages // pg),
            in_specs=[page_spec(j) for j in range(pg)]
            + [per_seq((C_KV_HEADS, rows, HEAD_DIM)), per_seq((C_KV_HEADS, rows, 3)), per_seq((n_cmp, 256)),
               per_seq((128, 512)),
               pl.BlockSpec((None, None) + kv_tile + (n_buf,), lambda s, i, pt: (layer, s, 0, 0, 0, 0)),
               pl.BlockSpec((n_cmp, n_blk), lambda s, i, pt: (0, 0))],
            out_specs=per_seq((C_KV_HEADS, rows, HEAD_DIM)),
            scratch_shapes=[pltpu.VMEM((C_KV_HEADS, rows, n_blk), F32), pltpu.VMEM((C_KV_HEADS, rows, HEAD_DIM), F32),
                            pltpu.VMEM((C_KV_HEADS, rows, 1), F32), pltpu.VMEM((C_KV_HEADS, rows, 1), F32),
                            pltpu.VMEM((C_KV_HEADS, rows, HEAD_DIM), F32)],
        ),
        out_shape=jax.ShapeDtypeStruct((ns, C_KV_HEADS, rows, HEAD_DIM), F32),
        compiler_params=_cparams(("parallel", "arbitrary")),
        name="nsa_sample",
    )(page_table, *([cache_slc_t] * pg), q16, gate16, kcvc, new_rows, win_buf_t, pool)


def _outproj_kernel(x_ref, ya_ref, yb_ref, yc_ref, w_ref, o_ref):
    acc = x_ref[...] + _dot(ya_ref[...], w_ref[0:256, :]) + _dot(yb_ref[...], w_ref[256:512, :])
    for hd in range(C_HEADS):
        r0 = 512 + hd * HEAD_DIM
        acc = acc + _dot(yc_ref[hd], w_ref[r0:r0 + HEAD_DIM, :])
    o_ref[...] = acc


def _outproj(x, ya, yb, yc, w_out, nb, t_len, tm):
    nt = t_len // tm
    rows = lambda w: pl.BlockSpec((tm, w), lambda b, i: (b * nt + i, 0))
    return pl.pallas_call(
        _outproj_kernel,
        grid=(nb, nt),
        in_specs=[rows(1024), rows(256), rows(256),
                  pl.BlockSpec((None, C_HEADS, tm, HEAD_DIM), lambda b, i: (b, 0, i, 0)),
                  pl.BlockSpec((1024, 1024), lambda b, i: (0, 0))],
        out_specs=rows(1024),
        out_shape=jax.ShapeDtypeStruct(x.shape, F32),
        compiler_params=_cparams(("parallel", "parallel")),
        name="outproj",
    )(x, ya, yb, yc, w_out)


def _top_vals_rows(x, k, rows_out):
    cols = x.shape[1]
    orow = lax.broadcasted_iota(jnp.int32, (rows_out, cols), 0)
    acc = jnp.full((rows_out, cols), NEG_INF, F32)
    for j in range(k):
        m = jnp.max(x, axis=0, keepdims=True)
        x = jnp.where(x == m, NEG_INF, x)
        acc = jnp.where(orow == j, m, acc)
    return acc


def _peer_route_kernel(x_ref, g_ref, wq_ref, qg_ref, sk_ref, h2_ref, e1_ref, e2_ref, th_ref):
    x = x_ref[...]
    hb = (x * lax.rsqrt(jnp.mean(x * x, axis=-1, keepdims=True) + 1e-6) * g_ref[...]).astype(BF16)
    h2_ref[...] = hb
    qp = jnp.dot(hb, wq_ref[...], preferred_element_type=F32)
    k = PEER_TOPK
    for hd in range(PEER_HEADS):
        qn = []
        for c in range(2):
            qc = qp[:, (2 * hd + c) * N_KEYS:(2 * hd + c + 1) * N_KEYS]
            qn.append((qc * lax.rsqrt(jnp.mean(qc * qc, axis=-1, keepdims=True) + 1e-6)
                       * qg_ref[:, c * 128:(c + 1) * 128]).astype(BF16))
        for ts in range(x.shape[0] // 128):
            tsl = slice(ts * 128, (ts + 1) * 128)
            st = [_dot_nt(sk_ref[hd, c], qn[c][tsl]) for c in range(2)]
            top = [_top_vals_rows(s, k + 1, 24) for s in st]
            cand = jnp.concatenate([top[0][0:1, :] + top[1]]
                                   + [top[0][a:a + 1, :] + top[1][0:8, :] for a in range(1, k + 1)], axis=0)
            best = _top_vals_rows(cand, k + 1, 24)
            tau = 0.5 * (best[k - 1:k, :] + best[k:k + 1, :])
            z = jnp.sum(jnp.where(cand >= tau, jnp.exp(cand - best[0:1, :]), 0.0), axis=0, keepdims=True)
            m2 = top[1][0:1, :]
            e1_ref[hd, :, tsl] = jnp.exp(st[0] - top[0][0:1, :]) / z
            e2_ref[hd, :, tsl] = jnp.exp(st[1] - m2)
            th_ref[hd, :, tsl] = jnp.exp((tau - m2) - st[0])


def _peer_dense_kernel(x_ref, h2_ref, u_ref, vt_ref, e1_ref, e2_ref, th_ref, o_ref, acc_ref, act_ref, w_ref, *, ea):
    j = pl.program_id(1)
    tm = h2_ref.shape[0]

    @pl.when(j == 0)
    def _():
        acc_ref[...] = jnp.zeros(acc_ref.shape, F32)

    h2 = h2_ref[...]
    hb = N_KEYS // 2
    ga = PEER_GA
    n_grp = ea // ga
    n_ts = tm // 128

    def experts(grp):
        gsl = slice(grp * ga * N_KEYS, (grp + 1) * ga * N_KEYS)
        act_ref[gsl, :] = jax.nn.gelu(_dot_nt(u_ref[gsl, :], h2))

    experts(0)
    for grp in range(n_grp):
        firsts = [grp * ga + k for k in range(ga)]
        th_rows = [[th_ref[hd, pl.ds(j * ea + aa, 1), :] for aa in firsts] for hd in range(PEER_HEADS)]
        e1_rows = [[e1_ref[hd, pl.ds(j * ea + aa, 1), :] for aa in firsts] for hd in range(PEER_HEADS)]
        for ts in range(n_ts):
            if ts == n_ts // 2 and grp + 1 < n_grp:
                experts(grp + 1)
            tsl = slice(ts * 128, (ts + 1) * 128)
            for bh in range(2):
                bsl = slice(bh * hb, (bh + 1) * hb)
                gates = [jnp.zeros((hb, 128), F32) for _ in range(ga)]
                for hd in range(PEER_HEADS):
                    e2 = e2_ref[hd, bsl, tsl]
                    for k in range(ga):
                        hit = e2 >= th_rows[hd][k][:, tsl]
                        gates[k] = gates[k] + jnp.where(hit, e2, 0.0) * e1_rows[hd][k][:, tsl]
                for k, aa in enumerate(firsts):
                    rsl = slice(aa * N_KEYS + bh * hb, aa * N_KEYS + (bh + 1) * hb)
                    w_ref[rsl, tsl] = (act_ref[rsl, tsl] * gates[k]).astype(BF16)
        gsl = slice(grp * ga * N_KEYS, (grp + 1) * ga * N_KEYS)
        acc_ref[...] += jnp.dot(vt_ref[:, gsl], w_ref[gsl, :], preferred_element_type=F32)

    @pl.when(j == pl.num_programs(1) - 1)
    def _():
        o_ref[...] = x_ref[...] + acc_ref[...].T


PEER_EA = 16
PEER_GA = 4


def _peer(x1, p, tm):
    h2, e1, e2, th = _peer_route(x1, p, tm)
    return _peer_dense(x1, h2, e1, e2, th, p, tm)


def _peer_route(x1, p, tm):
    n = x1.shape[0]
    nt = n // tm
    full = lambda shape: pl.BlockSpec(shape, lambda i: (0,) * len(shape))
    tk = pl.BlockSpec((PEER_HEADS, N_KEYS, tm), lambda i: (0, 0, i))
    sd = jax.ShapeDtypeStruct((PEER_HEADS, N_KEYS, n), F32)
    return pl.pallas_call(
        _peer_route_kernel,
        grid=(nt,),
        in_specs=[pl.BlockSpec((tm, 1024), lambda i: (i, 0)), full((1, 1024)), full((1024, 2048)), full((1, 256)),
                  full((PEER_HEADS, 2, N_KEYS, N_KEYS))],
        out_specs=[pl.BlockSpec((tm, 1024), lambda i: (i, 0)), tk, tk, tk],
        out_shape=[jax.ShapeDtypeStruct((n, 1024), BF16), sd, sd, sd],
        compiler_params=_cparams(("parallel",)),
        name="peer_route",
    )(x1, p["ln2_g"], p["p_wq"], p["p_qnorm_g"], p["p_subkeys"])


def _peer_dense(x1, h2, e1, e2, th, p, tm):
    n = x1.shape[0]
    nt = n // tm
    et = PEER_EA * N_KEYS
    n_exp = p["p_u"].shape[0]
    tk2 = pl.BlockSpec((PEER_HEADS, N_KEYS, tm), lambda i, j: (0, 0, i))
    return pl.pallas_call(
        functools.partial(_peer_dense_kernel, ea=PEER_EA),
        grid=(nt, n_exp // et),
        in_specs=[pl.BlockSpec((tm, 1024), lambda i, j: (i, 0)), pl.BlockSpec((tm, 1024), lambda i, j: (i, 0)),
                  pl.BlockSpec((et, 1024), lambda i, j: (j, 0)), pl.BlockSpec((1024, et), lambda i, j: (0, j)),
                  tk2, tk2, tk2],
        out_specs=pl.BlockSpec((tm, 1024), lambda i, j: (i, 0)),
        out_shape=jax.ShapeDtypeStruct((n, 1024), F32),
        scratch_shapes=[pltpu.VMEM((1024, tm), F32), pltpu.VMEM((et, tm), F32), pltpu.VMEM((et, tm), BF16)],
        compiler_params=_cparams(("parallel", "arbitrary")),
        name="peer_dense",
    )(x1, h2, p["p_u"], p["p_vt"], e1, e2, th)


def _prep_layer(l, ln1_g, ln2_g, w_in, w_out, a_vnorm_g, a_ws, a_bs, b_mu, b_w0, b_w2, b_a0, b_a2, b_g2, b_kk,
                b_ka, b_rk, b_lnx_g, b_lnx_b, c_qnorm_g, c_knorm_g, c_cmp_w1, c_cmp_b1, c_cmp_w2, p_wq,
                p_qnorm_g, p_subkeys, p_u, p_v, dec_seq):
    p = {}
    p["ln1_g"] = ln1_g[l][None, :]
    p["ln2_g"] = ln2_g[l][None, :]
    w = w_in[l]
    z = lambda c: jnp.zeros((w.shape[0], c), w.dtype)
    p["w_in"] = jnp.concatenate([w[:, :1472], z(64), w[:, 1472:], z(104)], axis=1).astype(BF16)
    p["a_vnorm_g"] = a_vnorm_g[l][None, :]
    wm = jnp.where(jnp.tril(jnp.ones((CHUNK, CHUNK), bool)), a_ws[l], 0.0)
    p["wm_prompt"] = wm.astype(BF16)
    p["brow_prompt"] = jnp.repeat(a_bs[l].T, HEAD_DIM, axis=1)
    reps = CHUNK // dec_seq
    eye = jnp.eye(reps, dtype=F32)
    p["wm_sample"] = jnp.stack([jnp.kron(eye, wm[g, :dec_seq, :dec_seq]) for g in range(4)]).astype(BF16)
    p["brow_sample"] = jnp.tile(p["brow_prompt"][:dec_seq], (reps, 1))
    p["c_qnorm_g"] = jnp.tile(c_qnorm_g[l], C_HEADS)[None, :]
    p["c_knorm_g12"] = jnp.stack([jnp.tile(c_knorm_g[l, 1], 2), jnp.tile(c_knorm_g[l, 2], 2)])
    p["bd512"] = _block_ones(512, HEAD_DIM)
    p["bd256"] = _block_ones(256, HEAD_DIM)
    p["bd128"] = _block_ones(128, HEAD_DIM)
    eye2 = jnp.eye(2, dtype=F32)
    w1h = c_cmp_w1[l].reshape(2, 2, CMP_STRIDE, HEAD_DIM, HEAD_DIM)
    p["c_w1e"] = jnp.einsum("kjsdc,kK,hH->jsKHdkhc", w1h, eye2, eye2).reshape(2, 4096, 256).astype(BF16)
    p["c_b1e"] = jnp.broadcast_to(c_cmp_b1[l][:, None, :], (2, 2, HEAD_DIM)).reshape(1, 256)
    p["c_w2e"] = jnp.einsum("kcd,kK,hH->khcKHd", c_cmp_w2[l], eye2, eye2).reshape(256, 256).astype(BF16)
    p["c_kg0"] = jnp.tile(c_knorm_g[l, 0], 2)[None, :]
    row = lambda a: a.reshape(1, -1)
    p["b_mu"] = row(b_mu[l])
    p["b_w0"] = row(b_w0[l])
    p["b_a0"] = row(b_a0[l])
    zz = jnp.zeros((64, 256), F32)
    p["b_w2a"] = jnp.concatenate([jnp.concatenate([b_w2[l], zz], axis=1),
                                  jnp.concatenate([zz, b_a2[l]], axis=1)], axis=0)
    p["b_g2"] = b_g2[l]
    p["b_kk"] = row(b_kk[l])
    p["b_ka"] = row(b_ka[l])
    p["b_rk"] = row(b_rk[l])
    p["b_lnx_g"] = row(b_lnx_g[l])
    p["b_lnx_b"] = row(b_lnx_b[l])
    p["w_out"] = w_out[l].astype(BF16)
    p["p_wq"] = p_wq[l].astype(BF16)
    p["p_qnorm_g"] = row(p_qnorm_g[l])
    p["p_subkeys"] = p_subkeys[l].astype(BF16)
    p["p_u"] = p_u[l].astype(BF16)
    p["p_vt"] = p_v[l].T.astype(BF16)
    return p


def _st_in(wkv):
    b = wkv.shape[0]
    return wkv.transpose(0, 3, 1, 2).reshape(b, HEAD_DIM, 256)


def _st_out(st):
    b = st.shape[0]
    return st.reshape(b, HEAD_DIM, B_HEADS, HEAD_DIM).transpose(0, 2, 3, 1)


def kernel(x_prompt, x_sample, cache_cmp_kv, cache_slc_kv, cache_win_kv, state_wkv, state_shift, page_table, ln1_g, ln2_g, w_in, w_out, a_vnorm_g, a_ws, a_bs, b_mu, b_w0, b_w2, b_a0, b_a2, b_g2, b_kk, b_ka, b_rk, b_lnx_g, b_lnx_b, c_qnorm_g, c_knorm_g, c_cmp_w1, c_cmp_b1, c_cmp_w2, p_wq, p_qnorm_g, p_subkeys, p_u, p_v):
    weights = (ln1_g, ln2_g, w_in, w_out, a_vnorm_g, a_ws, a_bs, b_mu, b_w0, b_w2, b_a0, b_a2, b_g2, b_kk, b_ka,
               b_rk, b_lnx_g, b_lnx_b, c_qnorm_g, c_knorm_g, c_cmp_w1, c_cmp_b1, c_cmp_w2, p_wq, p_qnorm_g,
               p_subkeys, p_u, p_v)
    nb, t_len, d_model = x_prompt.shape
    ns, dec_seq, _ = x_sample.shape
    depth = ln1_g.shape[0]
    n_pool = cache_cmp_kv.shape[1]
    n_s = ns * dec_seq
    n_sp = -(-n_s // 128) * 128
    rwkv_c = 64
    dec_c = 8
    pad_rows = lambda a, n: jnp.pad(a, ((0, n - a.shape[0]),) + ((0, 0),) * (a.ndim - 1))
    xp = x_prompt.reshape(nb * t_len, d_model)
    xs = pad_rows(x_sample.reshape(n_s, d_model), n_sp)
    kv5 = lambda a, b, t: a.reshape(b, t, 2, C_KV_HEADS, HEAD_DIM)
    cache_cmp_t = cache_cmp_kv.transpose(0, 1, 3, 4, 5, 2)
    cache_slc_t = cache_slc_kv.transpose(0, 1, 3, 4, 5, 2)
    cache_win_t = cache_win_kv.transpose(0, 1, 3, 4, 5, 2)
    outs_p, outs_s = [], []
    for l in range(depth):
        p = _prep_layer(l, *weights, dec_seq=dec_seq)
        ya, _, bproj, q, cmp, slc, win, gate, kvb = _inproj(xp, p, p["wm_prompt"], p["brow_prompt"], 256)
        bp3 = bproj.reshape(nb, t_len, B_PROJ)
        prev = jnp.concatenate([jnp.zeros((nb, 1, B_PROJ), F32), bp3[:, :-1]], axis=1).reshape(nb * t_len, B_PROJ)
        yb, st = _rwkv(bproj, prev, jnp.zeros((nb, HEAD_DIM, 256), F32), p, nb, t_len // rwkv_c, rwkv_c, rwkv_c,
                       n_sub=4)
        kcvc = _compress(cmp, p, nb)
        yc = _nsa_prompt(q, gate, kcvc, kvb, nb, t_len)
        x1 = _outproj(xp, ya, yb, yc, p["w_out"], nb, t_len, 512)
        xp = _peer(x1, p, 512)
        n_win = min(WINDOW, t_len)
        outs_p.append((kv5(cmp, nb, t_len), kv5(slc, nb, t_len), kv5(win, nb, t_len)[:, t_len - n_win:],
                       _st_out(st), bp3[:, -1]))
        sya, sv, sbproj, sq, scmp, sslc, swin, sgate, _ = _inproj(xs, p, p["wm_sample"], p["brow_sample"], 128)
        sb3 = sbproj[:n_s].reshape(ns, dec_seq, B_PROJ)
        sprev = jnp.concatenate([state_shift[l][:, None], sb3[:, :-1]], axis=1)
        pad_c = lambda a: jnp.pad(a, ((0, 0), (0, dec_c - dec_seq), (0, 0))).reshape(ns * dec_c, B_PROJ)
        syb, sst = _rwkv(pad_c(sb3), pad_c(sprev), _st_in(state_wkv[l]), p, ns, 1, dec_c, dec_seq)
        syb = pad_rows(syb.reshape(ns, dec_c, 256)[:, :dec_seq].reshape(n_s, 256), n_sp)
        skc = _compress_paged(cache_cmp_t, page_table, p, l)
        q16 = (sq[:n_s].reshape(ns, dec_seq, C_KV_HEADS, C_GROUP, HEAD_DIM).transpose(0, 2, 3, 1, 4)
               .reshape(ns, C_KV_HEADS, C_GROUP * dec_seq, HEAD_DIM))
        g16 = (sgate[:n_s, :3 * C_HEADS].reshape(ns, dec_seq, C_KV_HEADS, C_GROUP, 3).transpose(0, 2, 3, 1, 4)
               .reshape(ns, C_KV_HEADS, C_GROUP * dec_seq, 3))
        new_rows = jnp.concatenate([sslc[:n_s], swin[:n_s]], axis=1).reshape(ns, dec_seq, 512)
        new_rows = jnp.pad(new_rows, ((0, 0), (0, 128 - dec_seq), (0, 0)))
        win_buf = cache_win_kv[l].reshape(ns, -1, 256)
        so = _nsa_sample(page_table, q16, g16, skc, cache_slc_t, new_rows, cache_win_t, l, dec_seq)
        syc = (so.reshape(ns, C_KV_HEADS, C_GROUP, dec_seq, HEAD_DIM).transpose(1, 2, 0, 3, 4)
               .reshape(C_HEADS, n_s, HEAD_DIM))
        syc = jnp.pad(syc, ((0, 0), (0, n_sp - n_s), (0, 0)))[None]
        sx1 = _outproj(xs, sya, syb, syc, p["w_out"], 1, n_sp, 128)
        xs = _peer(sx1, p, 128)
        swin3 = swin[:n_s].reshape(ns, dec_seq, 256)
        win_new = jnp.concatenate([win_buf, swin3], axis=1)[:, dec_seq:]
        outs_s.append((kv5(scmp[:n_s], ns, dec_seq), kv5(sslc[:n_s], ns, dec_seq),
                       kv5(win_new, ns, win_buf.shape[1]), _st_out(sst), sb3[:, -1],
                       sv[:n_s].reshape(ns, dec_seq, A_WIDTH)))
    stk = lambda lst, i: jnp.stack([s[i] for s in lst], axis=0)
    return (xp.reshape(nb, t_len, d_model), xs[:n_s].reshape(ns, dec_seq, d_model),
            stk(outs_p, 0), stk(outs_p, 1), stk(outs_p, 2), stk(outs_p, 3), stk(outs_p, 4),
            stk(outs_s, 0), stk(outs_s, 1), stk(outs_s, 2), stk(outs_s, 3), stk(outs_s, 4), stk(outs_s, 5))
```

```python
import functools
import math

import jax
import jax.numpy as jnp
from jax import lax
from jax.experimental import pallas as pl
from jax.experimental.pallas import tpu as pltpu

F32 = jnp.float32
BF16 = jnp.bfloat16
HIGHEST = lax.Precision.HIGHEST

HEAD_DIM = 64
CHUNK = 128
A_WIDTH = 256
B_WIDTH = 256
B_HEADS = 4
B_PROJ = 960
C_HEADS = 8
C_KV_HEADS = 2
C_GROUP = 4
C_WIDTH = 512
CMP_STRIDE = 16
SLC_BLOCK = 64
SLC_TOPK = 16
WINDOW = 512
Q_BLOCK = 128
PAGE = 128
PEER_HEADS = 8
N_KEYS = 128
PEER_TOPK = 16
RWKV_GN_EPS = 64e-5
NEG_BIG = -1e30
POS_BIG = 1e30
NEG_INF = float("-inf")
VMEM_LIMIT = 56 * 1024 * 1024


def _cparams(sem):
    return pltpu.CompilerParams(dimension_semantics=sem, vmem_limit_bytes=VMEM_LIMIT)


def _dot(a, b):
    return jnp.dot(a.astype(BF16), b.astype(BF16), preferred_element_type=F32)


def _dot_nt(a, b):
    return lax.dot_general(a.astype(BF16), b.astype(BF16), (((1,), (1,)), ((), ())),
                           preferred_element_type=F32)


def _dot2(x, m):
    hi = x.astype(BF16)
    lo = (x - hi.astype(F32)).astype(BF16)
    return (jnp.dot(hi, m, preferred_element_type=F32) + jnp.dot(lo, m, preferred_element_type=F32))


def _dot3(a, b, dims):
    a_hi = a.astype(BF16)
    b_hi = b.astype(BF16)
    a_lo = (a - a_hi.astype(F32)).astype(BF16)
    b_lo = (b - b_hi.astype(F32)).astype(BF16)
    dg = lambda x, y: lax.dot_general(x, y, (dims, ((), ())), preferred_element_type=F32)
    return dg(a_hi, b_hi) + (dg(a_hi, b_lo) + dg(a_lo, b_hi))


def _hdot(a, b):
    return _dot3(a, b, ((1,), (0,)))


def _hdot_nt(a, b):
    return _dot3(a, b, ((1,), (1,)))


def _hdot_tn(a, b):
    return _dot3(a, b, ((0,), (0,)))


def _block_ones(width, group):
    i = jnp.arange(width)
    return (i[:, None] // group == i[None, :] // group).astype(BF16)


IN_PAD = 2944


def _inproj_kernel(x_ref, g_ref, w_ref, avg_ref, wm_ref, brow_ref, qg_ref, kg_ref, bd_ref,
                   ya_ref, v_ref, b_ref, q_ref, cmp_ref, slc_ref, win_ref, gate_ref, kvb_ref, *, tm):
    x = x_ref[...]
    h = x * lax.rsqrt(jnp.mean(x * x, axis=-1, keepdims=True) + 1e-6) * g_ref[...]
    z = jnp.dot(h.astype(BF16), w_ref[...], preferred_element_type=F32)
    b_ref[...] = z[:, 512:512 + B_PROJ]
    u = jax.nn.gelu(z[:, 0:256])
    gv = jax.nn.gelu(z[:, 256:512])
    v = gv * lax.rsqrt(jnp.mean(gv * gv, axis=-1, keepdims=True) + 1e-6) * avg_ref[...]
    v_ref[...] = v
    lane_g = lax.broadcasted_iota(jnp.int32, (CHUNK, A_WIDTH), 1) // HEAD_DIM
    for c in range(tm // CHUNK):
        vc = v[c * CHUNK:(c + 1) * CHUNK].astype(BF16)
        mixed = brow_ref[...]
        for g in range(4):
            mg = jnp.dot(wm_ref[g], vc, preferred_element_type=F32)
            mixed = mixed + jnp.where(lane_g == g, mg, 0.0)
        ya_ref[c * CHUNK:(c + 1) * CHUNK, :] = u[c * CHUNK:(c + 1) * CHUNK] * mixed
    bd = bd_ref[...]
    q = z[:, 1536:2048]
    q_ref[...] = q * lax.rsqrt(_dot2(q * q, bd) * (1.0 / HEAD_DIM) + 1e-6) * qg_ref[...]
    bd128 = bd[0:128, 0:128]
    cmp_ref[...] = z[:, 2048:2304]
    for j, (o_ref, off) in enumerate(((slc_ref, 2304), (win_ref, 2560))):
        k = z[:, off:off + 128]
        kn = k * lax.rsqrt(_dot2(k * k, bd128) * (1.0 / HEAD_DIM) + 1e-6) * kg_ref[j:j + 1, :]
        vv = z[:, off + 128:off + 256]
        o_ref[:, 0:128] = kn
        o_ref[:, 128:256] = vv
        kvb_ref[:, j * 256:j * 256 + 128] = kn.astype(BF16)
        kvb_ref[:, j * 256 + 128:j * 256 + 256] = vv.astype(BF16)
    gate_ref[...] = jax.nn.sigmoid(z[:, 2816:2944])


def _inproj(x, p, wm, brow, tm):
    n = x.shape[0]
    full = lambda shape: pl.BlockSpec(shape, lambda i: (0,) * len(shape))
    rows = lambda w: pl.BlockSpec((tm, w), lambda i: (i, 0))
    outs = [(A_WIDTH, F32), (A_WIDTH, F32), (B_PROJ, F32), (C_WIDTH, F32), (256, F32), (256, F32), (256, F32),
            (128, F32), (512, BF16)]
    return pl.pallas_call(
        functools.partial(_inproj_kernel, tm=tm),
        grid=(n // tm,),
        in_specs=[rows(1024), full((1, 1024)), full((1024, IN_PAD)), full((1, A_WIDTH)), full((4, CHUNK, CHUNK)),
                  full((CHUNK, A_WIDTH)), full((1, C_WIDTH)), full((2, 128)), full((512, 512))],
        out_specs=[rows(w) for w, _ in outs],
        out_shape=[jax.ShapeDtypeStruct((n, w), d) for w, d in outs],
        compiler_params=_cparams(("parallel",)),
        name="inproj",
    )(x, p["ln1_g"], p["w_in"], p["a_vnorm_g"], wm, brow, p["c_qnorm_g"], p["c_knorm_g12"], p["bd512"])


def _rwkv_prep_kernel(x_ref, xp_ref, mu_ref, w0_ref, w2a_ref, a0_ref, g2_ref, kkp_ref, ka_ref, rk_ref, bd_ref,
                      tri_ref, p2_ref, y0_ref, p3_ref, z_ref, g_ref, bonus_ref, *, c, t_valid):
    x = x_ref[...]
    xs = x + (xp_ref[...] - x) * mu_ref[...]
    r = xs[:, 0:256]
    k = xs[:, 256:512]
    v = xs[:, 512:768]
    wa = xs[:, 768:896]
    lane = lax.broadcasted_iota(jnp.int32, wa.shape, 1)
    pre = _hdot(jnp.where(lane < 64, jnp.tanh(wa), wa), w2a_ref[...])
    y = -(w0_ref[...] + pre[:, 0:256])
    softplus = jnp.maximum(y, 0.0) + jnp.log1p(jnp.exp(-jnp.abs(y)))
    ew = jnp.exp(-softplus - 0.5)
    a = jax.nn.sigmoid(a0_ref[...] + pre[:, 256:512])
    g_ref[...] = _hdot(jax.nn.sigmoid(xs[:, 896:960]), g2_ref[...])
    bd = bd_ref[...]
    kkr = k * kkp_ref[...]
    kk = kkr / jnp.maximum(jnp.sqrt(_dot2(kkr * kkr, bd)), 1e-12)
    k2 = k * (1.0 + (a - 1.0) * ka_ref[...])
    bonus_ref[...] = _dot2(r * k2 * rk_ref[...], bd) * v
    n_sub = x.shape[0] // c
    if t_valid < c:
        live = lax.broadcasted_iota(jnp.int32, ew.shape, 0) % c < t_valid
        ew = jnp.where(live, ew, 0.0)
        kk = jnp.where(live, kk, 0.0)
        k2 = jnp.where(live, k2, 0.0)
        v = jnp.where(live, v, 0.0)
    cum = jnp.concatenate([jnp.dot(tri_ref[...], -ew[s * c:(s + 1) * c], preferred_element_type=F32,
                                   precision=HIGHEST) for s in range(n_sub)], axis=0)
    gam = jnp.exp(cum)
    ginv = jnp.exp(-cum)
    g_ends = [gam[(s + 1) * c - 1:(s + 1) * c, :] for s in range(n_sub)]
    g_end_rows = jnp.concatenate([jnp.broadcast_to(ge, (c, ge.shape[1])) for ge in g_ends], axis=0)
    alpha_t = -kk * jnp.exp(cum + ew)
    bhat = kk * a * ginv
    khat = k2 * ginv
    rt = r * gam
    kbar = khat * g_end_rows
    bbar = bhat * g_end_rows
    row = lax.broadcasted_iota(jnp.int32, (c, c), 0)
    col = lax.broadcasted_iota(jnp.int32, (c, c), 1)
    eye_c = (row == col).astype(F32)
    r64 = lax.broadcasted_iota(jnp.int32, (HEAD_DIM, HEAD_DIM), 0)
    c64 = lax.broadcasted_iota(jnp.int32, (HEAD_DIM, HEAD_DIM), 1)
    chains = [(s, h) for s in range(n_sub) for h in range(B_HEADS)]
    hs = range(len(chains))
    rws = [slice(s * c, (s + 1) * c) for s, _ in chains]
    sls = [slice(h * HEAD_DIM, (h + 1) * HEAD_DIM) for _, h in chains]
    al = [alpha_t[rws[i], sls[i]] for i in hs]
    bh = [bhat[rws[i], sls[i]] for i in hs]
    kh = [khat[rws[i], sls[i]] for i in hs]
    rh = [rt[rws[i], sls[i]] for i in hs]
    vh = [v[rws[i], sls[i]] for i in hs]
    mb = [jnp.where(row > col, _hdot_nt(al[h], bh[h]), 0.0) for h in hs]
    mk = [jnp.where(row > col, _hdot_nt(al[h], kh[h]), 0.0) for h in hs]
    qk = [jnp.where(row >= col, _hdot_nt(rh[h], kh[h]), 0.0) for h in hs]
    qb = [jnp.where(row >= col, _hdot_nt(rh[h], bh[h]), 0.0) for h in hs]
    mkv = [_hdot(mk[h], vh[h]) for h in hs]
    tinv = [eye_c + mb[h] for h in hs]
    pw = mb
    for _ in range(int(math.log2(c)) - 1):
        pw = [_hdot(pw[h], pw[h]) for h in hs]
        tinv = [tinv[h] + _hdot(tinv[h], pw[h]) for h in hs]
    p1 = [_hdot(tinv[h], al[h]) for h in hs]
    u = [_hdot(tinv[h], mkv[h]) for h in hs]
    for i in hs:
        s = chains[i][0]
        rw, sl = rws[i], sls[i]
        mat_rows = slice(s * HEAD_DIM, (s + 1) * HEAD_DIM)
        p2_ref[rw, sl] = rh[i] + _hdot(qb[i], p1[i])
        y0_ref[rw, sl] = _hdot(qk[i], vh[i]) + _hdot(qb[i], u[i])
        p3_ref[mat_rows, sl] = (jnp.where(r64 == c64, g_ends[s][:, sl], 0.0)
                                + _hdot_tn(bbar[rw, sl], p1[i]))
        z_ref[mat_rows, sl] = _hdot_tn(kbar[rw, sl], vh[i]) + _hdot_tn(bbar[rw, sl], u[i])


def _rwkv_seq_kernel(st0_ref, p2_ref, y0_ref, p3_ref, z_ref, g_ref, bonus_ref, lg_ref, lb_ref,
                     y_ref, st_ref, st_scr):
    @pl.when(pl.program_id(1) == 0)
    def _():
        st_scr[...] = st0_ref[...]

    chains = [(b, slice(h * HEAD_DIM, (h + 1) * HEAD_DIM)) for b in range(st_scr.shape[0]) for h in range(B_HEADS)]
    st = [st_scr[b, :, sl] for b, sl in chains]
    ys = [_hdot(p2_ref[b, :, sl], st[i]) + y0_ref[b, :, sl] for i, (b, sl) in enumerate(chains)]
    new = [_hdot(p3_ref[b, :, sl], st[i]) + z_ref[b, :, sl] for i, (b, sl) in enumerate(chains)]
    for i, (b, sl) in enumerate(chains):
        st_scr[b, :, sl] = new[i]
        y = ys[i]
        mu = jnp.mean(y, axis=-1, keepdims=True)
        var = jnp.mean(jnp.square(y - mu), axis=-1, keepdims=True)
        yn = (y - mu) * lax.rsqrt(var + RWKV_GN_EPS) * lg_ref[:, sl] + lb_ref[:, sl]
        y_ref[b, :, sl] = (yn + bonus_ref[b, :, sl]) * g_ref[b, :, sl]
    st_ref[...] = st_scr[...]


def _rwkv(proj, prev, st0, p, nb, nch, c, t_valid, n_sub=1):
    n = proj.shape[0]
    full = lambda shape: pl.BlockSpec(shape, lambda b, i: (0,) * len(shape))
    nst = nch // n_sub
    rows_p = lambda w: pl.BlockSpec((n_sub * c, w), lambda b, i: (b * nst + i, 0))
    mats_p = pl.BlockSpec((n_sub * HEAD_DIM, 256), lambda b, i: (b * nst + i, 0))
    tri = (jnp.arange(c)[:, None] >= jnp.arange(c)[None, :]).astype(F32)
    p2, y0, p3, z, g, bonus = pl.pallas_call(
        functools.partial(_rwkv_prep_kernel, c=c, t_valid=t_valid),
        grid=(nb, nst),
        in_specs=[rows_p(B_PROJ), rows_p(B_PROJ), full((1, B_PROJ)), full((1, 256)), full((128, 512)),
                  full((1, 256)), full((64, 256)), full((1, 256)), full((1, 256)), full((1, 256)), full((256, 256)),
                  full((c, c))],
        out_specs=[rows_p(256), rows_p(256), mats_p, mats_p, rows_p(256), rows_p(256)],
        out_shape=[jax.ShapeDtypeStruct((n, 256), F32), jax.ShapeDtypeStruct((n, 256), F32),
                   jax.ShapeDtypeStruct((nb * nch * HEAD_DIM, 256), F32),
                   jax.ShapeDtypeStruct((nb * nch * HEAD_DIM, 256), F32),
                   jax.ShapeDtypeStruct((n, 256), F32), jax.ShapeDtypeStruct((n, 256), F32)],
        compiler_params=_cparams(("parallel", "parallel")),
        name="rwkv_prep",
    )(proj, prev, p["b_mu"], p["b_w0"], p["b_w2a"], p["b_a0"], p["b_g2"], p["b_kk"], p["b_ka"], p["b_rk"],
      p["bd256"], tri)
    bb = 2 if nb % 2 == 0 else 1
    t_len = nch * c
    seq3 = lambda a: a.reshape(nb, t_len, 256)
    mat3 = lambda a: a.reshape(nb, nch * HEAD_DIM, 256)
    st_spec = pl.BlockSpec((bb, HEAD_DIM, 256), lambda b, i: (b, 0, 0))
    rows3 = pl.BlockSpec((bb, c, 256), lambda b, i: (b, i, 0))
    mats3 = pl.BlockSpec((bb, HEAD_DIM, 256), lambda b, i: (b, i, 0))
    y, st = pl.pallas_call(
        _rwkv_seq_kernel,
        grid=(nb // bb, nch),
        in_specs=[st_spec, rows3, rows3, mats3, mats3, rows3, rows3, full((1, 256)), full((1, 256))],
        out_specs=[rows3, st_spec],
        out_shape=[jax.ShapeDtypeStruct((nb, t_len, 256), F32), jax.ShapeDtypeStruct((nb, HEAD_DIM, 256), F32)],
        scratch_shapes=[pltpu.VMEM((bb, HEAD_DIM, 256), F32)],
        compiler_params=_cparams(("parallel", "arbitrary")),
        name="rwkv_seq",
    )(st0, seq3(p2), seq3(y0), mat3(p3), mat3(z), seq3(g), seq3(bonus), p["b_lnx_g"], p["b_lnx_b"])
    return y.reshape(n, 256), st


def _compress_tail(a0, a1, n_out, b1_ref, w2_ref, kg_ref, bd_ref, o_ref):
    n = a1.shape[0]
    hid = jax.nn.gelu(a0 + pltpu.roll(a1, n - 1, 0) + b1_ref[...])[0:n_out]
    out = _dot(hid, w2_ref[...])
    k = out[:, 0:128]
    kn = k * lax.rsqrt(_dot2(k * k, bd_ref[...]) * (1.0 / HEAD_DIM) + 1e-6) * kg_ref[...]
    o_ref[:, 0:128] = kn.astype(BF16)
    o_ref[:, 128:256] = out[:, 128:256].astype(BF16)


def _compress_kernel(x_ref, w1_ref, b1_ref, w2_ref, kg_ref, bd_ref, o_ref):
    x = x_ref[...].astype(BF16)
    a0 = jnp.dot(x, w1_ref[0], preferred_element_type=F32)
    a1 = jnp.dot(x, w1_ref[1], preferred_element_type=F32)
    _compress_tail(a0, a1, x.shape[0], b1_ref, w2_ref, kg_ref, bd_ref, o_ref)


def _compress_paged_kernel(pt_ref, *refs, pg):
    pages = refs[:pg + 1]
    w1_ref, b1_ref, w2_ref, kg_ref, bd_ref, perm_ref, o_ref, rows_scr = refs[pg + 1:]
    ppp = PAGE // CMP_STRIDE
    for j, r in enumerate(pages):
        rows = _dot_nt(perm_ref[...], r[...].reshape(4 * HEAD_DIM, PAGE))
        for s in range(CMP_STRIDE):
            rows_scr[s, j * ppp:(j + 1) * ppp, :] = rows[s * ppp:(s + 1) * ppp, :]
    n_piece = (pg + 1) * ppp
    a0 = jnp.zeros((n_piece, 256), F32)
    a1 = jnp.zeros((n_piece, 256), F32)
    for s in range(CMP_STRIDE):
        xs = rows_scr[s].astype(BF16)
        a0 = a0 + jnp.dot(xs, w1_ref[0, s * 256:(s + 1) * 256, :], preferred_element_type=F32)
        a1 = a1 + jnp.dot(xs, w1_ref[1, s * 256:(s + 1) * 256, :], preferred_element_type=F32)
    _compress_tail(a0, a1, pg * 8, b1_ref, w2_ref, kg_ref, bd_ref, o_ref)


def _compress_weight_specs(index):
    full = lambda shape: pl.BlockSpec(shape, index(len(shape)))
    return [full((2, 4096, 256)), full((1, 256)), full((256, 256)), full((1, 128)), full((128, 128))]


def _compress(cmp_rows, p, nb):
    n_piece = cmp_rows.shape[0] // nb // CMP_STRIDE
    x = cmp_rows.reshape(nb, n_piece, 4096)
    return pl.pallas_call(
        _compress_kernel,
        grid=(nb,),
        in_specs=[pl.BlockSpec((None, n_piece, 4096), lambda b: (b, 0, 0))]
        + _compress_weight_specs(lambda r: (lambda b: (0,) * r)),
        out_specs=pl.BlockSpec((None, n_piece, 256), lambda b: (b, 0, 0)),
        out_shape=jax.ShapeDtypeStruct((nb, n_piece, 256), BF16),
        compiler_params=_cparams(("parallel",)),
        name="compress",
    )(x, p["c_w1e"], p["c_b1e"], p["c_w2e"], p["c_kg0"], p["bd128"])


CMP_PAGES = 32


def _compress_paged(cache_t, page_table, p, layer):
    ns, n_pages = page_table.shape
    pg = min(CMP_PAGES, n_pages)
    ppp = PAGE // CMP_STRIDE
    r = jnp.arange(PAGE)
    perm = (r[None, :] == (r[:, None] % ppp) * CMP_STRIDE + r[:, None] // ppp).astype(BF16)

    def page_spec(j):
        return pl.BlockSpec((None, None, 2, C_KV_HEADS, HEAD_DIM, PAGE),
                            lambda s, i, pt: (layer, pt[s, jnp.minimum(i * pg + j, n_pages - 1)], 0, 0, 0, 0))

    return pl.pallas_call(
        functools.partial(_compress_paged_kernel, pg=pg),
        grid_spec=pltpu.PrefetchScalarGridSpec(
            num_scalar_prefetch=1,
            grid=(ns, n_pages // pg),
            in_specs=[page_spec(j) for j in range(pg + 1)]
            + _compress_weight_specs(lambda r: (lambda s, i, pt: (0,) * r))
            + [pl.BlockSpec((PAGE, PAGE), lambda s, i, pt: (0, 0))],
            out_specs=pl.BlockSpec((None, pg * ppp, 256), lambda s, i, pt: (s, i, 0)),
            scratch_shapes=[pltpu.VMEM((CMP_STRIDE, (pg + 1) * ppp, 256), F32)],
        ),
        out_shape=jax.ShapeDtypeStruct((ns, n_pages * ppp, 256), BF16),
        compiler_params=_cparams(("parallel", "arbitrary")),
        name="compress_paged",
    )(page_table, *([cache_t] * (pg + 1)), p["c_w1e"], p["c_b1e"], p["c_w2e"], p["c_kg0"], p["bd128"], perm)


def _alibi_slope(head):
    return 2.0 ** (-8.0 * (head + 1.0) / C_HEADS)


def _topk_mask(x, k):
    nl = x.shape[-1]
    lane = lax.broadcasted_iota(jnp.int32, x.shape, x.ndim - 1).astype(F32)
    sel = jnp.zeros(x.shape, F32)
    for _ in range(k):
        m = jnp.max(x, axis=-1, keepdims=True)
        idx = jnp.min(jnp.where(x == m, lane, float(nl)), axis=-1, keepdims=True)
        hit = lane == idx
        sel = jnp.where(hit, 1.0, sel)
        x = jnp.where(hit, NEG_INF, x)
    return sel


def _masked_softmax(s, valid):
    s = jnp.where(valid, s, NEG_BIG)
    e = jnp.exp(s - jnp.max(s, axis=-1, keepdims=True))
    return jnp.where(valid, e / jnp.sum(e, axis=-1, keepdims=True), 0.0)


def _flash_step(carry, s, ok, v, v_is_t=False):
    m, l, acc = carry
    if ok is not None:
        s = jnp.where(ok, s, NEG_BIG)
    m_new = jnp.maximum(m, jnp.max(s, axis=-1, keepdims=True))
    alpha = jnp.exp(m - m_new)
    pr = jnp.exp(s - m_new)
    if ok is not None:
        pr = jnp.where(ok, pr, 0.0)
    l = alpha * l + jnp.sum(pr, axis=-1, keepdims=True)
    pv = _dot_nt(pr, v) if v_is_t else jnp.dot(pr.astype(BF16), v, preferred_element_type=F32)
    return m_new, l, alpha * acc + pv


def _flash_init(rows):
    return (jnp.full((rows, 1), NEG_BIG, F32), jnp.zeros((rows, 1), F32), jnp.zeros((rows, HEAD_DIM), F32))


def _topk_mask_rows(x, k):
    n = x.shape[0]
    ridx = lax.broadcasted_iota(jnp.int32, x.shape, 0).astype(F32)
    sel = jnp.zeros(x.shape, F32)
    for _ in range(k):
        m = jnp.max(x, axis=0, keepdims=True)
        idx = jnp.min(jnp.where(x == m, ridx, float(n)), axis=0, keepdims=True)
        hit = ridx == idx
        sel = jnp.where(hit, 1.0, sel)
        x = jnp.where(hit, NEG_INF, x)
    return sel


def _nsa_prompt_kernel_t(q_ref, gate_ref, kc_ref, kvb_ref, pool_ref, o_ref, *, n_slc):
    i = pl.program_id(1)
    qb = Q_BLOCK
    nq4 = C_GROUP * qb
    n_cmp = kc_ref.shape[0]
    nbl = pool_ref.shape[0]
    lane4 = lax.broadcasted_iota(jnp.int32, (1, nq4), 1)
    tq_row = i * qb + lane4 % qb
    tok_row = i * qb + lax.broadcasted_iota(jnp.int32, (1, qb), 1)
    gate_t = gate_ref[...].T
    blk_col = lax.broadcasted_iota(jnp.int32, (nbl, 1), 0)
    rel2 = (lax.broadcasted_iota(jnp.int32, (256, nq4), 0) - lane4 % qb).astype(F32)
    exp_row = lax.broadcasted_iota(jnp.int32, (256, nbl), 0) // SLC_BLOCK
    exp_col = lax.broadcasted_iota(jnp.int32, (256, nbl), 1)
    half = i // 2
    for h in range(C_KV_HEADS):
        heads = [C_GROUP * h + g for g in range(C_GROUP)]
        q_t = (jnp.concatenate([q_ref[:, hd * HEAD_DIM:(hd + 1) * HEAD_DIM] for hd in heads], axis=0)
               * (HEAD_DIM ** -0.5)).T.astype(BF16)
        slope_row = jnp.full((1, nq4), _alibi_slope(heads[0]), F32)
        for g in range(1, C_GROUP):
            slope_row = jnp.where(lane4 // qb == g, _alibi_slope(heads[g]), slope_row)
        kcol = slice(h * HEAD_DIM, (h + 1) * HEAD_DIM)
        vcol = slice(128 + h * HEAD_DIM, 128 + (h + 1) * HEAD_DIM)
        c_dist = tq_row - (lax.broadcasted_iota(jnp.int32, (n_cmp, 1), 0) * CMP_STRIDE + (2 * CMP_STRIDE - 1))
        valid = c_dist >= 0
        s_c = jnp.dot(kc_ref[:, kcol], q_t, preferred_element_type=F32) - slope_row * c_dist.astype(F32)
        s_c = jnp.where(valid, s_c, NEG_BIG)
        e_c = jnp.exp(s_c - jnp.max(s_c, axis=0, keepdims=True))
        p_c = jnp.where(valid, e_c / jnp.sum(e_c, axis=0, keepdims=True), 0.0)
        o_c = lax.dot_general(kc_ref[:, vcol], p_c.astype(BF16), (((0,), (0,)), ((), ())),
                              preferred_element_type=F32)
        psum = p_c[:, 0:qb] + p_c[:, qb:2 * qb] + p_c[:, 2 * qb:3 * qb] + p_c[:, 3 * qb:4 * qb]
        ps_hi = psum.astype(BF16)
        ps_lo = (psum - ps_hi.astype(F32)).astype(BF16)
        imp = (jnp.dot(pool_ref[...], ps_hi, preferred_element_type=F32)
               + jnp.dot(pool_ref[...], ps_lo, preferred_element_type=F32))
        imp = jnp.where(blk_col * SLC_BLOCK <= tok_row, imp, NEG_BIG)
        imp = jnp.where((blk_col == 0) | (blk_col == tok_row // SLC_BLOCK), POS_BIG, imp)
        imp = jnp.where(blk_col < n_slc, imp, NEG_INF)
        sel_f = _topk_mask_rows(imp, min(SLC_TOPK, n_slc))
        used_col = jnp.max(sel_f, axis=1, keepdims=True)
        sel_t = sel_f.astype(BF16)
        bias2 = slope_row * rel2

        def tiles_step(carry, ps, kcols, vcols, mask_fn):
            m, l, acc = carry
            rows, scores, shifts = [], [], []
            m_new = m
            for p in ps:
                r0 = pl.multiple_of(jnp.maximum(p, 0) * 256, 256)
                off = ((2 * p - i) * qb).astype(F32)
                s = jnp.dot(kvb_ref[pl.ds(r0, 256), kcols], q_t, preferred_element_type=F32) + bias2
                s = jnp.where(mask_fn(p, off), s, NEG_BIG)
                shift = slope_row * off
                m_new = jnp.maximum(m_new, jnp.max(s, axis=0, keepdims=True) + shift)
                rows.append(r0)
                scores.append(s)
                shifts.append(shift)
            alpha = jnp.exp(m - m_new)
            l = alpha * l
            acc = alpha * acc
            for r0, s, shift in zip(rows, scores, shifts):
                pr = jnp.exp(s - (m_new - shift))
                l = l + jnp.sum(pr, axis=0, keepdims=True)
                acc = acc + lax.dot_general(kvb_ref[pl.ds(r0, 256), vcols], pr.astype(BF16),
                                            (((0,), (0,)), ((), ())), preferred_element_type=F32)
            return m_new, l, acc

        def sel_mask(p, off):
            expand = (exp_col == 4 * p + exp_row).astype(BF16)
            m1 = jnp.dot(expand, sel_t, preferred_element_type=F32) > 0.5
            return jnp.concatenate([m1] * C_GROUP, axis=1)

        def sel_causal_mask(p, off):
            return sel_mask(p, off) & (rel2 + off <= 0.0)

        def win_mask(p, off):
            d = rel2 + jnp.where(p >= 0, off, float(WINDOW))
            return (d <= 0.0) & (d > -float(WINDOW))

        def sel_body(pp, carry):
            used = jnp.max(jnp.where(blk_col // 8 == pp, used_col, 0.0))
            return lax.cond(used > 0.0, lambda cr: tiles_step(cr, (2 * pp, 2 * pp + 1), kcol, vcol, sel_mask),
                            lambda cr: cr, carry)

        init = (jnp.full((1, nq4), 0.5 * NEG_BIG, F32), jnp.zeros((1, nq4), F32), jnp.zeros((HEAD_DIM, nq4), F32))
        carry = lax.fori_loop(0, half // 2, sel_body, init)
        _, l_s, acc_s = lax.cond(
            half % 2 == 1,
            lambda cr: tiles_step(cr, (half - 1, half), kcol, vcol, sel_causal_mask),
            lambda cr: tiles_step(cr, (half,), kcol, vcol, sel_causal_mask), carry)

        wk = slice(256 + h * HEAD_DIM, 256 + (h + 1) * HEAD_DIM)
        wv = slice(384 + h * HEAD_DIM, 384 + (h + 1) * HEAD_DIM)
        n_wp = WINDOW // 256
        _, l_w, acc_w = tiles_step(init, tuple(half - d for d in range(n_wp, -1, -1)), wk, wv, win_mask)

        def gate_row(j):
            return jnp.concatenate([gate_t[3 * hd + j:3 * hd + j + 1, :] for hd in heads], axis=1)

        o_t = gate_row(0) * o_c + gate_row(1) * (acc_s / l_s) + gate_row(2) * (acc_w / l_w)
        o = o_t.T
        for g, hd in enumerate(heads):
            o_ref[hd] = o[g * qb:(g + 1) * qb]


def _nsa_prompt(q, gate, kcvc, kvb, nb, t_len):
    assert t_len % 256 == 0
    nq = t_len // Q_BLOCK
    n_cmp = kcvc.shape[1]
    n_slc = -(-t_len // SLC_BLOCK)
    nbl = -(-n_slc // 128) * 128
    pool = (jnp.arange(nbl)[:, None] == jnp.arange(n_cmp)[None, :] // (SLC_BLOCK // CMP_STRIDE)).astype(BF16)
    return pl.pallas_call(
        functools.partial(_nsa_prompt_kernel_t, n_slc=n_slc),
        grid=(nb, nq),
        in_specs=[pl.BlockSpec((Q_BLOCK, C_WIDTH), lambda b, i: (b * nq + i, 0)),
                  pl.BlockSpec((Q_BLOCK, 128), lambda b, i: (b * nq + i, 0)),
                  pl.BlockSpec((None, n_cmp, 256), lambda b, i: (b, 0, 0)),
                  pl.BlockSpec((t_len, 512), lambda b, i: (b, 0)),
                  pl.BlockSpec((nbl, n_cmp), lambda b, i: (0, 0))],
        out_specs=pl.BlockSpec((None, C_HEADS, Q_BLOCK, HEAD_DIM), lambda b, i: (b, 0, i, 0)),
        out_shape=jax.ShapeDtypeStruct((nb, C_HEADS, t_len, HEAD_DIM), F32),
        compiler_params=_cparams(("parallel", "arbitrary")),
        name="nsa_prompt",
    )(q, gate, kcvc, kvb, pool)


SLC_PAGES = 16


def _nsa_sample_kernel(pt_ref, *refs, pg, dec_seq, past_len, n_buf):
    pages = refs[:pg]
    (q_ref, gate_ref, kc_ref, new_ref, buf_ref, pool_ref, o_ref,
     sel_scr, ocw_scr, m_scr, l_scr, acc_scr) = refs[pg:]
    i = pl.program_id(1)
    n_parts = pl.num_programs(1)
    rows = C_GROUP * dec_seq
    n_cmp = kc_ref.shape[0]
    n_blk = pool_ref.shape[1]
    ridx = lax.broadcasted_iota(jnp.int32, (rows, 1), 0)
    tq = ridx % dec_seq
    t_abs = past_len + tq
    lane128 = lax.broadcasted_iota(jnp.int32, (1, 128), 1)

    def slope_of(h):
        s = jnp.full((rows, 1), _alibi_slope(C_GROUP * h), F32)
        for g in range(1, C_GROUP):
            s = jnp.where(ridx // dec_seq == g, _alibi_slope(C_GROUP * h + g), s)
        return s

    @pl.when(i == 0)
    def _():
        same_tok = (lax.broadcasted_iota(jnp.int32, (rows, rows), 0) % dec_seq
                    == lax.broadcasted_iota(jnp.int32, (rows, rows), 1) % dec_seq).astype(BF16)
        for h in range(C_KV_HEADS):
            slope = slope_of(h)
            qh = (q_ref[h] * (HEAD_DIM ** -0.5)).astype(BF16)
            kcol = slice(h * HEAD_DIM, (h + 1) * HEAD_DIM)
            vcol = slice(128 + h * HEAD_DIM, 128 + (h + 1) * HEAD_DIM)
            c_dist = t_abs - (lax.broadcasted_iota(jnp.int32, (1, n_cmp), 1) * CMP_STRIDE + (2 * CMP_STRIDE - 1))
            s_c = _dot_nt(qh, kc_ref[:, kcol]) - slope * c_dist.astype(F32)
            p_c = _masked_softmax(s_c, c_dist >= 0)
            o_c = jnp.dot(p_c.astype(BF16), kc_ref[:, vcol], preferred_element_type=F32)
            hi = p_c.astype(BF16)
            lo = (p_c - hi.astype(F32)).astype(BF16)
            psum = (jnp.dot(same_tok, hi, preferred_element_type=F32)
                    + jnp.dot(same_tok, lo, preferred_element_type=F32))
            imp = _dot2(psum, pool_ref[...])
            blk = lax.broadcasted_iota(jnp.int32, (1, n_blk), 1)
            imp = jnp.where(blk == 0, POS_BIG, imp)
            sel_scr[h] = _topk_mask(imp, min(SLC_TOPK, n_blk + 1) - 1)
            wk = slice(256 + h * HEAD_DIM, 256 + (h + 1) * HEAD_DIM)
            wv = slice(384 + h * HEAD_DIM, 384 + (h + 1) * HEAD_DIM)
            d_buf = (n_buf + tq) - lax.broadcasted_iota(jnp.int32, (1, n_buf), 1)
            s_b = _dot(qh, buf_ref[0, h]) - slope * d_buf.astype(F32)
            carry = _flash_step(_flash_init(rows), s_b, (d_buf >= 0) & (d_buf < WINDOW), buf_ref[1, h], True)
            d_new = tq - lane128
            s_n = _dot_nt(qh, new_ref[:, wk]) - slope * d_new.astype(F32)
            _, l_w, acc_w = _flash_step(carry, s_n, (d_new >= 0) & (lane128 < dec_seq),
                                        new_ref[:, wv].astype(BF16))
            gt = gate_ref[h]
            ocw_scr[h] = gt[:, 0:1] * o_c + gt[:, 2:3] * (acc_w / l_w)
            m0, l0, a0 = _flash_init(rows)
            m_scr[h] = m0
            l_scr[h] = l0
            acc_scr[h] = a0

    nk = pg * PAGE
    key_pos = i * nk + lax.broadcasted_iota(jnp.int32, (1, nk), 1)
    expand = (lax.broadcasted_iota(jnp.int32, (n_blk, nk), 0)
              == i * (nk // SLC_BLOCK) + lax.broadcasted_iota(jnp.int32, (n_blk, nk), 1) // SLC_BLOCK).astype(BF16)
    hs = range(C_KV_HEADS)
    dist = t_abs - key_pos
    dist_f = dist.astype(F32)
    qh = [(q_ref[h] * (HEAD_DIM ** -0.5)).astype(BF16) for h in hs]
    k_all = [jnp.concatenate([r[0, h] for r in pages], axis=1).astype(BF16) for h in hs]
    v_all = [jnp.concatenate([r[1, h] for r in pages], axis=1).astype(BF16) for h in hs]
    selx = [jnp.dot(sel_scr[h].astype(BF16), expand, preferred_element_type=F32) for h in hs]
    ok = [(dist >= 0) & (selx[h] > 0.5) for h in hs]
    s = [jnp.where(ok[h], jnp.dot(qh[h], k_all[h], preferred_element_type=F32) - slope_of(h) * dist_f, NEG_BIG)
         for h in hs]
    m_old = [m_scr[h] for h in hs]
    m_new = [jnp.maximum(m_old[h], jnp.max(s[h], axis=-1, keepdims=True)) for h in hs]
    alpha = [jnp.exp(m_old[h] - m_new[h]) for h in hs]
    pr = [jnp.where(ok[h], jnp.exp(s[h] - m_new[h]), 0.0) for h in hs]
    pv = [_dot_nt(pr[h], v_all[h]) for h in hs]
    for h in hs:
        m_scr[h] = m_new[h]
        l_scr[h] = alpha[h] * l_scr[h] + jnp.sum(pr[h], axis=-1, keepdims=True)
        acc_scr[h] = alpha[h] * acc_scr[h] + pv[h]

    @pl.when(i == n_parts - 1)
    def _():
        for h in range(C_KV_HEADS):
            slope = slope_of(h)
            qh = (q_ref[h] * (HEAD_DIM ** -0.5)).astype(BF16)
            kcol = slice(h * HEAD_DIM, (h + 1) * HEAD_DIM)
            vcol = slice(128 + h * HEAD_DIM, 128 + (h + 1) * HEAD_DIM)
            d_new = tq - lane128
            s_n = _dot_nt(qh, new_ref[:, kcol]) - slope * d_new.astype(F32)
            _, l_s, acc_s = _flash_step((m_scr[h], l_scr[h], acc_scr[h]), s_n,
                                        (d_new >= 0) & (lane128 < dec_seq), new_ref[:, vcol].astype(BF16))
            o_ref[h] = ocw_scr[h] + gate_ref[h][:, 1:2] * (acc_s / l_s)


def _nsa_sample(page_table, q16, gate16, kcvc, cache_slc_t, new_rows, win_buf_t, layer, dec_seq):
    ns, n_pages = page_table.shape
    pg = min(SLC_PAGES, n_pages)
    past_len = n_pages * PAGE
    n_buf = win_buf_t.shape[-1]
    n_cmp = kcvc.shape[1]
    n_blk = past_len // SLC_BLOCK
    rows = C_GROUP * dec_seq
    pool = (jnp.arange(n_cmp)[:, None] // (SLC_BLOCK // CMP_STRIDE) == jnp.arange(n_blk)[None, :]).astype(BF16)
    per_seq = lambda shape: pl.BlockSpec((None,) + shape, lambda s, i, pt: (s,) + (0,) * len(shape))

    kv_tile = (2, C_KV_HEADS, HEAD_DIM)

    def page_spec(j):
        return pl.BlockSpec((None, None) + kv_tile + (PAGE,),
                            lambda s, i, pt: (layer, pt[s, i * pg + j], 0, 0, 0, 0))

    return pl.pallas_call(
        functools.partial(_nsa_sample_kernel, pg=pg, dec_seq=dec_seq, past_len=past_len, n_buf=n_buf),
        grid_spec=pltpu.PrefetchScalarGridSpec(
            num_scalar_prefetch=1,
            grid=(ns, n_pages // pg),
            in_specs=[page_spec(j) for j in range(pg)]
            + [per_seq((C_KV_HEADS, rows, HEAD_DIM)), per_seq((C_KV_HEADS, rows, 3)), per_seq((n_cmp, 256)),
               per_seq((128, 512)),
               pl.BlockSpec((None, None) + kv_tile + (n_buf,), lambda s, i, pt: (layer, s, 0, 0, 0, 0)),
               pl.BlockSpec((n_cmp, n_blk), lambda s, i, pt: (0, 0))],
            out_specs=per_seq((C_KV_HEADS, rows, HEAD_DIM)),
            scratch_shapes=[pltpu.VMEM((C_KV_HEADS, rows, n_blk), F32), pltpu.VMEM((C_KV_HEADS, rows, HEAD_DIM), F32),
                            pltpu.VMEM((C_KV_HEADS, rows, 1), F32), pltpu.VMEM((C_KV_HEADS, rows, 1), F32),
                            pltpu.VMEM((C_KV_HEADS, rows, HEAD_DIM), F32)],
        ),
        out_shape=jax.ShapeDtypeStruct((ns, C_KV_HEADS, rows, HEAD_DIM), F32),
        compiler_params=_cparams(("parallel", "arbitrary")),
        name="nsa_sample",
    )(page_table, *([cache_slc_t] * pg), q16, gate16, kcvc, new_rows, win_buf_t, pool)


def _outproj_kernel(x_ref, ya_ref, yb_ref, yc_ref, w_ref, o_ref):
    acc = x_ref[...] + _dot(ya_ref[...], w_ref[0:256, :]) + _dot(yb_ref[...], w_ref[256:512, :])
    for hd in range(C_HEADS):
        r0 = 512 + hd * HEAD_DIM
        acc = acc + _dot(yc_ref[hd], w_ref[r0:r0 + HEAD_DIM, :])
    o_ref[...] = acc


def _outproj(x, ya, yb, yc, w_out, nb, t_len, tm):
    nt = t_len // tm
    rows = lambda w: pl.BlockSpec((tm, w), lambda b, i: (b * nt + i, 0))
    return pl.pallas_call(
        _outproj_kernel,
        grid=(nb, nt),
        in_specs=[rows(1024), rows(256), rows(256),
                  pl.BlockSpec((None, C_HEADS, tm, HEAD_DIM), lambda b, i: (b, 0, i, 0)),
                  pl.BlockSpec((1024, 1024), lambda b, i: (0, 0))],
        out_specs=rows(1024),
        out_shape=jax.ShapeDtypeStruct(x.shape, F32),
        compiler_params=_cparams(("parallel", "parallel")),
        name="outproj",
    )(x, ya, yb, yc, w_out)


def _top_vals_rows(x, k, rows_out):
    cols = x.shape[1]
    orow = lax.broadcasted_iota(jnp.int32, (rows_out, cols), 0)
    acc = jnp.full((rows_out, cols), NEG_INF, F32)
    for j in range(k):
        m = jnp.max(x, axis=0, keepdims=True)
        x = jnp.where(x == m, NEG_INF, x)
        acc = jnp.where(orow == j, m, acc)
    return acc


def _peer_route_kernel(x_ref, g_ref, wq_ref, qg_ref, sk_ref, h2_ref, e1_ref, e2_ref, th_ref):
    x = x_ref[...]
    hb = (x * lax.rsqrt(jnp.mean(x * x, axis=-1, keepdims=True) + 1e-6) * g_ref[...]).astype(BF16)
    h2_ref[...] = hb
    qp = jnp.dot(hb, wq_ref[...], preferred_element_type=F32)
    k = PEER_TOPK
    for hd in range(PEER_HEADS):
        qn = []
        for c in range(2):
            qc = qp[:, (2 * hd + c) * N_KEYS:(2 * hd + c + 1) * N_KEYS]
            qn.append((qc * lax.rsqrt(jnp.mean(qc * qc, axis=-1, keepdims=True) + 1e-6)
                       * qg_ref[:, c * 128:(c + 1) * 128]).astype(BF16))
        for ts in range(x.shape[0] // 128):
            tsl = slice(ts * 128, (ts + 1) * 128)
            st = [_dot_nt(sk_ref[hd, c], qn[c][tsl]) for c in range(2)]
            top = [_top_vals_rows(s, k + 1, 24) for s in st]
            cand = jnp.concatenate([top[0][0:1, :] + top[1]]
                                   + [top[0][a:a + 1, :] + top[1][0:8, :] for a in range(1, k + 1)], axis=0)
            best = _top_vals_rows(cand, k + 1, 24)
            tau = 0.5 * (best[k - 1:k, :] + best[k:k + 1, :])
            z = jnp.sum(jnp.where(cand >= tau, jnp.exp(cand - best[0:1, :]), 0.0), axis=0, keepdims=True)
            m2 = top[1][0:1, :]
            e1_ref[hd, :, tsl] = jnp.exp(st[0] - top[0][0:1, :]) / z
            e2_ref[hd, :, tsl] = jnp.exp(st[1] - m2)
            th_ref[hd, :, tsl] = jnp.exp((tau - m2) - st[0])


def _peer_dense_kernel(x_ref, h2_ref, u_ref, vt_ref, e1_ref, e2_ref, th_ref, o_ref, acc_ref, act_ref, w_ref, *, ea):
    j = pl.program_id(1)
    tm = h2_ref.shape[0]

    @pl.when(j == 0)
    def _():
        acc_ref[...] = jnp.zeros(acc_ref.shape, F32)

    h2 = h2_ref[...]
    hb = N_KEYS // 2
    ga = PEER_GA
    n_grp = ea // ga
    n_ts = tm // 128

    def experts(grp):
        gsl = slice(grp * ga * N_KEYS, (grp + 1) * ga * N_KEYS)
        act_ref[gsl, :] = jax.nn.gelu(_dot_nt(u_ref[gsl, :], h2))

    experts(0)
    for grp in range(n_grp):
        firsts = [grp * ga + k for k in range(ga)]
        th_rows = [[th_ref[hd, pl.ds(j * ea + aa, 1), :] for aa in firsts] for hd in range(PEER_HEADS)]
        e1_rows = [[e1_ref[hd, pl.ds(j * ea + aa, 1), :] for aa in firsts] for hd in range(PEER_HEADS)]
        for ts in range(n_ts):
            if ts == n_ts // 2 and grp + 1 < n_grp:
                experts(grp + 1)
            tsl = slice(ts * 128, (ts + 1) * 128)
            for bh in range(2):
                bsl = slice(bh * hb, (bh + 1) * hb)
                gates = [jnp.zeros((hb, 128), F32) for _ in range(ga)]
                for hd in range(PEER_HEADS):
                    e2 = e2_ref[hd, bsl, tsl]
                    for k in range(ga):
                        hit = e2 >= th_rows[hd][k][:, tsl]
                        gates[k] = gates[k] + jnp.where(hit, e2, 0.0) * e1_rows[hd][k][:, tsl]
                for k, aa in enumerate(firsts):
                    rsl = slice(aa * N_KEYS + bh * hb, aa * N_KEYS + (bh + 1) * hb)
                    w_ref[rsl, tsl] = (act_ref[rsl, tsl] * gates[k]).astype(BF16)
        gsl = slice(grp * ga * N_KEYS, (grp + 1) * ga * N_KEYS)
        acc_ref[...] += jnp.dot(vt_ref[:, gsl], w_ref[gsl, :], preferred_element_type=F32)

    @pl.when(j == pl.num_programs(1) - 1)
    def _():
        o_ref[...] = x_ref[...] + acc_ref[...].T


PEER_EA = 16
PEER_GA = 4


def _peer(x1, p, tm):
    h2, e1, e2, th = _peer_route(x1, p, tm)
    return _peer_dense(x1, h2, e1, e2, th, p, tm)


def _peer_route(x1, p, tm):
    n = x1.shape[0]
    nt = n // tm
    full = lambda shape: pl.BlockSpec(shape, lambda i: (0,) * len(shape))
    tk = pl.BlockSpec((PEER_HEADS, N_KEYS, tm), lambda i: (0, 0, i))
    sd = jax.ShapeDtypeStruct((PEER_HEADS, N_KEYS, n), F32)
    return pl.pallas_call(
        _peer_route_kernel,
        grid=(nt,),
        in_specs=[pl.BlockSpec((tm, 1024), lambda i: (i, 0)), full((1, 1024)), full((1024, 2048)), full((1, 256)),
                  full((PEER_HEADS, 2, N_KEYS, N_KEYS))],
        out_specs=[pl.BlockSpec((tm, 1024), lambda i: (i, 0)), tk, tk, tk],
        out_shape=[jax.ShapeDtypeStruct((n, 1024), BF16), sd, sd, sd],
        compiler_params=_cparams(("parallel",)),
        name="peer_route",
    )(x1, p["ln2_g"], p["p_wq"], p["p_qnorm_g"], p["p_subkeys"])


def _peer_dense(x1, h2, e1, e2, th, p, tm):
    n = x1.shape[0]
    nt = n // tm
    et = PEER_EA * N_KEYS
    n_exp = p["p_u"].shape[0]
    tk2 = pl.BlockSpec((PEER_HEADS, N_KEYS, tm), lambda i, j: (0, 0, i))
    return pl.pallas_call(
        functools.partial(_peer_dense_kernel, ea=PEER_EA),
        grid=(nt, n_exp // et),
        in_specs=[pl.BlockSpec((tm, 1024), lambda i, j: (i, 0)), pl.BlockSpec((tm, 1024), lambda i, j: (i, 0)),
                  pl.BlockSpec((et, 1024), lambda i, j: (j, 0)), pl.BlockSpec((1024, et), lambda i, j: (0, j)),
                  tk2, tk2, tk2],
        out_specs=pl.BlockSpec((tm, 1024), lambda i, j: (i, 0)),
        out_shape=jax.ShapeDtypeStruct((n, 1024), F32),
        scratch_shapes=[pltpu.VMEM((1024, tm), F32), pltpu.VMEM((et, tm), F32), pltpu.VMEM((et, tm), BF16)],
        compiler_params=_cparams(("parallel", "arbitrary")),
        name="peer_dense",
    )(x1, h2, p["p_u"], p["p_vt"], e1, e2, th)


def _prep_layer(l, ln1_g, ln2_g, w_in, w_out, a_vnorm_g, a_ws, a_bs, b_mu, b_w0, b_w2, b_a0, b_a2, b_g2, b_kk,
                b_ka, b_rk, b_lnx_g, b_lnx_b, c_qnorm_g, c_knorm_g, c_cmp_w1, c_cmp_b1, c_cmp_w2, p_wq,
                p_qnorm_g, p_subkeys, p_u, p_v, dec_seq):
    p = {}
    p["ln1_g"] = ln1_g[l][None, :]
    p["ln2_g"] = ln2_g[l][None, :]
    w = w_in[l]
    z = lambda c: jnp.zeros((w.shape[0], c), w.dtype)
    p["w_in"] = jnp.concatenate([w[:, :1472], z(64), w[:, 1472:], z(104)], axis=1).astype(BF16)
    p["a_vnorm_g"] = a_vnorm_g[l][None, :]
    wm = jnp.where(jnp.tril(jnp.ones((CHUNK, CHUNK), bool)), a_ws[l], 0.0)
    p["wm_prompt"] = wm.astype(BF16)
    p["brow_prompt"] = jnp.repeat(a_bs[l].T, HEAD_DIM, axis=1)
    reps = CHUNK // dec_seq
    eye = jnp.eye(reps, dtype=F32)
    p["wm_sample"] = jnp.stack([jnp.kron(eye, wm[g, :dec_seq, :dec_seq]) for g in range(4)]).astype(BF16)
    p["brow_sample"] = jnp.tile(p["brow_prompt"][:dec_seq], (reps, 1))
    p["c_qnorm_g"] = jnp.tile(c_qnorm_g[l], C_HEADS)[None, :]
    p["c_knorm_g12"] = jnp.stack([jnp.tile(c_knorm_g[l, 1], 2), jnp.tile(c_knorm_g[l, 2], 2)])
    p["bd512"] = _block_ones(512, HEAD_DIM)
    p["bd256"] = _block_ones(256, HEAD_DIM)
    p["bd128"] = _block_ones(128, HEAD_DIM)
    eye2 = jnp.eye(2, dtype=F32)
    w1h = c_cmp_w1[l].reshape(2, 2, CMP_STRIDE, HEAD_DIM, HEAD_DIM)
    p["c_w1e"] = jnp.einsum("kjsdc,kK,hH->jsKHdkhc", w1h, eye2, eye2).reshape(2, 4096, 256).astype(BF16)
    p["c_b1e"] = jnp.broadcast_to(c_cmp_b1[l][:, None, :], (2, 2, HEAD_DIM)).reshape(1, 256)
    p["c_w2e"] = jnp.einsum("kcd,kK,hH->khcKHd", c_cmp_w2[l], eye2, eye2).reshape(256, 256).astype(BF16)
    p["c_kg0"] = jnp.tile(c_knorm_g[l, 0], 2)[None, :]
    row = lambda a: a.reshape(1, -1)
    p["b_mu"] = row(b_mu[l])
    p["b_w0"] = row(b_w0[l])
    p["b_a0"] = row(b_a0[l])
    zz = jnp.zeros((64, 256), F32)
    p["b_w2a"] = jnp.concatenate([jnp.concatenate([b_w2[l], zz], axis=1),
                                  jnp.concatenate([zz, b_a2[l]], axis=1)], axis=0)
    p["b_g2"] = b_g2[l]
    p["b_kk"] = row(b_kk[l])
    p["b_ka"] = row(b_ka[l])
    p["b_rk"] = row(b_rk[l])
    p["b_lnx_g"] = row(b_lnx_g[l])
    p["b_lnx_b"] = row(b_lnx_b[l])
    p["w_out"] = w_out[l].astype(BF16)
    p["p_wq"] = p_wq[l].astype(BF16)
    p["p_qnorm_g"] = row(p_qnorm_g[l])
    p["p_subkeys"] = p_subkeys[l].astype(BF16)
    p["p_u"] = p_u[l].astype(BF16)
    p["p_vt"] = p_v[l].T.astype(BF16)
    return p


def _st_in(wkv):
    b = wkv.shape[0]
    return wkv.transpose(0, 3, 1, 2).reshape(b, HEAD_DIM, 256)


def _st_out(st):
    b = st.shape[0]
    return st.reshape(b, HEAD_DIM, B_HEADS, HEAD_DIM).transpose(0, 2, 3, 1)


def kernel(x_prompt, x_sample, cache_cmp_kv, cache_slc_kv, cache_win_kv, state_wkv, state_shift, page_table, ln1_g, ln2_g, w_in, w_out, a_vnorm_g, a_ws, a_bs, b_mu, b_w0, b_w2, b_a0, b_a2, b_g2, b_kk, b_ka, b_rk, b_lnx_g, b_lnx_b, c_qnorm_g, c_knorm_g, c_cmp_w1, c_cmp_b1, c_cmp_w2, p_wq, p_qnorm_g, p_subkeys, p_u, p_v):
    weights = (ln1_g, ln2_g, w_in, w_out, a_vnorm_g, a_ws, a_bs, b_mu, b_w0, b_w2, b_a0, b_a2, b_g2, b_kk, b_ka,
               b_rk, b_lnx_g, b_lnx_b, c_qnorm_g, c_knorm_g, c_cmp_w1, c_cmp_b1, c_cmp_w2, p_wq, p_qnorm_g,
               p_subkeys, p_u, p_v)
    nb, t_len, d_model = x_prompt.shape
    ns, dec_seq, _ = x_sample.shape
    depth = ln1_g.shape[0]
    n_pool = cache_cmp_kv.shape[1]
    n_s = ns * dec_seq
    n_sp = -(-n_s // 128) * 128
    rwkv_c = 64
    dec_c = 8
    pad_rows = lambda a, n: jnp.pad(a, ((0, n - a.shape[0]),) + ((0, 0),) * (a.ndim - 1))
    xp = x_prompt.reshape(nb * t_len, d_model)
    xs = pad_rows(x_sample.reshape(n_s, d_model), n_sp)
    kv5 = lambda a, b, t: a.reshape(b, t, 2, C_KV_HEADS, HEAD_DIM)
    cache_cmp_t = cache_cmp_kv.transpose(0, 1, 3, 4, 5, 2)
    cache_slc_t = cache_slc_kv.transpose(0, 1, 3, 4, 5, 2)
    cache_win_t = cache_win_kv.transpose(0, 1, 3, 4, 5, 2)
    outs_p, outs_s = [], []
    for l in range(depth):
        p = _prep_layer(l, *weights, dec_seq=dec_seq)
        ya, _, bproj, q, cmp, slc, win, gate, kvb = _inproj(xp, p, p["wm_prompt"], p["brow_prompt"], 256)
        bp3 = bproj.reshape(nb, t_len, B_PROJ)
        prev = jnp.concatenate([jnp.zeros((nb, 1, B_PROJ), F32), bp3[:, :-1]], axis=1).reshape(nb * t_len, B_PROJ)
        yb, st = _rwkv(bproj, prev, jnp.zeros((nb, HEAD_DIM, 256), F32), p, nb, t_len // rwkv_c, rwkv_c, rwkv_c,
                       n_sub=4)
        kcvc = _compress(cmp, p, nb)
        yc = _nsa_prompt(q, gate, kcvc, kvb, nb, t_len)
        x1 = _outproj(xp, ya, yb, yc, p["w_out"], nb, t_len, 512)
        xp = _peer(x1, p, 512)
        n_win = min(WINDOW, t_len)
        outs_p.append((kv5(cmp, nb, t_len), kv5(slc, nb, t_len), kv5(win, nb, t_len)[:, t_len - n_win:],
                       _st_out(st), bp3[:, -1]))
        sya, sv, sbproj, sq, scmp, sslc, swin, sgate, _ = _inproj(xs, p, p["wm_sample"], p["brow_sample"], 128)
        sb3 = sbproj[:n_s].reshape(ns, dec_seq, B_PROJ)
        sprev = jnp.concatenate([state_shift[l][:, None], sb3[:, :-1]], axis=1)
        pad_c = lambda a: jnp.pad(a, ((0, 0), (0, dec_c - dec_seq), (0, 0))).reshape(ns * dec_c, B_PROJ)
        syb, sst = _rwkv(pad_c(sb3), pad_c(sprev), _st_in(state_wkv[l]), p, ns, 1, dec_c, dec_seq)
        syb = pad_rows(syb.reshape(ns, dec_c, 256)[:, :dec_seq].reshape(n_s, 256), n_sp)
        skc = _compress_paged(cache_cmp_t, page_table, p, l)
        q16 = (sq[:n_s].reshape(ns, dec_seq, C_KV_HEADS, C_GROUP, HEAD_DIM).transpose(0, 2, 3, 1, 4)
               .reshape(ns, C_KV_HEADS, C_GROUP * dec_seq, HEAD_DIM))
        g16 = (sgate[:n_s, :3 * C_HEADS].reshape(ns, dec_seq, C_KV_HEADS, C_GROUP, 3).transpose(0, 2, 3, 1, 4)
               .reshape(ns, C_KV_HEADS, C_GROUP * dec_seq, 3))
        new_rows = jnp.concatenate([sslc[:n_s], swin[:n_s]], axis=1).reshape(ns, dec_seq, 512)
        new_rows = jnp.pad(new_rows, ((0, 0), (0, 128 - dec_seq), (0, 0)))
        win_buf = cache_win_kv[l].reshape(ns, -1, 256)
        so = _nsa_sample(page_table, q16, g16, skc, cache_slc_t, new_rows, cache_win_t, l, dec_seq)
        syc = (so.reshape(ns, C_KV_HEADS, C_GROUP, dec_seq, HEAD_DIM).transpose(1, 2, 0, 3, 4)
               .reshape(C_HEADS, n_s, HEAD_DIM))
        syc = jnp.pad(syc, ((0, 0), (0, n_sp - n_s), (0, 0)))[None]
        sx1 = _outproj(xs, sya, syb, syc, p["w_out"], 1, n_sp, 128)
        xs = _peer(sx1, p, 128)
        swin3 = swin[:n_s].reshape(ns, dec_seq, 256)
        win_new = jnp.concatenate([win_buf, swin3], axis=1)[:, dec_seq:]
        outs_s.append((kv5(scmp[:n_s], ns, dec_seq), kv5(sslc[:n_s], ns, dec_seq),
                       kv5(win_new, ns, win_buf.shape[1]), _st_out(sst), sb3[:, -1],
                       sv[:n_s].reshape(ns, dec_seq, A_WIDTH)))
    stk = lambda lst, i: jnp.stack([s[i] for s in lst], axis=0)
    return (xp.reshape(nb, t_len, d_model), xs[:n_s].reshape(ns, dec_seq, d_model),
            stk(outs_p, 0), stk(outs_p, 1), stk(outs_p, 2), stk(outs_p, 3), stk(outs_p, 4),
            stk(outs_s, 0), stk(outs_s, 1), stk(outs_s, 2), stk(outs_s, 3), stk(outs_s, 4), stk(outs_s, 5))
```

```python
import functools
import math

import jax
import jax.numpy as jnp
from jax import lax
from jax.experimental import pallas as pl
from jax.experimental.pallas import tpu as pltpu

F32 = jnp.float32
BF16 = jnp.bfloat16
HIGHEST = lax.Precision.HIGHEST

HEAD_DIM = 64
CHUNK = 128
A_WIDTH = 256
B_WIDTH = 256
B_HEADS = 4
B_PROJ = 960
C_HEADS = 8
C_KV_HEADS = 2
C_GROUP = 4
C_WIDTH = 512
CMP_STRIDE = 16
SLC_BLOCK = 64
SLC_TOPK = 16
WINDOW = 512
Q_BLOCK = 128
PAGE = 128
PEER_HEADS = 8
N_KEYS = 128
PEER_TOPK = 16
RWKV_GN_EPS = 64e-5
NEG_BIG = -1e30
POS_BIG = 1e30
NEG_INF = float("-inf")
VMEM_LIMIT = 56 * 1024 * 1024


def _cparams(sem):
    return pltpu.CompilerParams(dimension_semantics=sem, vmem_limit_bytes=VMEM_LIMIT)


def _dot(a, b):
    return jnp.dot(a.astype(BF16), b.astype(BF16), preferred_element_type=F32)


def _dot_nt(a, b):
    return lax.dot_general(a.astype(BF16), b.astype(BF16), (((1,), (1,)), ((), ())),
                           preferred_element_type=F32)


def _dot2(x, m):
    hi = x.astype(BF16)
    lo = (x - hi.astype(F32)).astype(BF16)
    return (jnp.dot(hi, m, preferred_element_type=F32) + jnp.dot(lo, m, preferred_element_type=F32))


def _dot3(a, b, dims):
    a_hi = a.astype(BF16)
    b_hi = b.astype(BF16)
    a_lo = (a - a_hi.astype(F32)).astype(BF16)
    b_lo = (b - b_hi.astype(F32)).astype(BF16)
    dg = lambda x, y: lax.dot_general(x, y, (dims, ((), ())), preferred_element_type=F32)
    return dg(a_hi, b_hi) + (dg(a_hi, b_lo) + dg(a_lo, b_hi))


def _hdot(a, b):
    return _dot3(a, b, ((1,), (0,)))


def _hdot_nt(a, b):
    return _dot3(a, b, ((1,), (1,)))


def _hdot_tn(a, b):
    return _dot3(a, b, ((0,), (0,)))


def _block_ones(width, group):
    i = jnp.arange(width)
    return (i[:, None] // group == i[None, :] // group).astype(BF16)


IN_PAD = 2944


def _inproj_kernel(x_ref, g_ref, w_ref, avg_ref, wm_ref, brow_ref, qg_ref, kg_ref, bd_ref,
                   ya_ref, v_ref, b_ref, q_ref, cmp_ref, slc_ref, win_ref, gate_ref, kvb_ref, *, tm):
    x = x_ref[...]
    h = x * lax.rsqrt(jnp.mean(x * x, axis=-1, keepdims=True) + 1e-6) * g_ref[...]
    z = jnp.dot(h.astype(BF16), w_ref[...], preferred_element_type=F32)
    b_ref[...] = z[:, 512:512 + B_PROJ]
    u = jax.nn.gelu(z[:, 0:256])
    gv = jax.nn.gelu(z[:, 256:512])
    v = gv * lax.rsqrt(jnp.mean(gv * gv, axis=-1, keepdims=True) + 1e-6) * avg_ref[...]
    v_ref[...] = v
    lane_g = lax.broadcasted_iota(jnp.int32, (CHUNK, A_WIDTH), 1) // HEAD_DIM
    for c in range(tm // CHUNK):
        vc = v[c * CHUNK:(c + 1) * CHUNK].astype(BF16)
        mixed = brow_ref[...]
        for g in range(4):
            mg = jnp.dot(wm_ref[g], vc, preferred_element_type=F32)
            mixed = mixed + jnp.where(lane_g == g, mg, 0.0)
        ya_ref[c * CHUNK:(c + 1) * CHUNK, :] = u[c * CHUNK:(c + 1) * CHUNK] * mixed
    bd = bd_ref[...]
    q = z[:, 1536:2048]
    q_ref[...] = q * lax.rsqrt(_dot2(q * q, bd) * (1.0 / HEAD_DIM) + 1e-6) * qg_ref[...]
    bd128 = bd[0:128, 0:128]
    cmp_ref[...] = z[:, 2048:2304]
    for j, (o_ref, off) in enumerate(((slc_ref, 2304), (win_ref, 2560))):
        k = z[:, off:off + 128]
        kn = k * lax.rsqrt(_dot2(k * k, bd128) * (1.0 / HEAD_DIM) + 1e-6) * kg_ref[j:j + 1, :]
        vv = z[:, off + 128:off + 256]
        o_ref[:, 0:128] = kn
        o_ref[:, 128:256] = vv
        kvb_ref[:, j * 256:j * 256 + 128] = kn.astype(BF16)
        kvb_ref[:, j * 256 + 128:j * 256 + 256] = vv.astype(BF16)
    gate_ref[...] = jax.nn.sigmoid(z[:, 2816:2944])


def _inproj(x, p, wm, brow, tm):
    n = x.shape[0]
    full = lambda shape: pl.BlockSpec(shape, lambda i: (0,) * len(shape))
    rows = lambda w: pl.BlockSpec((tm, w), lambda i: (i, 0))
    outs = [(A_WIDTH, F32), (A_WIDTH, F32), (B_PROJ, F32), (C_WIDTH, F32), (256, F32), (256, F32), (256, F32),
            (128, F32), (512, BF16)]
    return pl.pallas_call(
        functools.partial(_inproj_kernel, tm=tm),
        grid=(n // tm,),
        in_specs=[rows(1024), full((1, 1024)), full((1024, IN_PAD)), full((1, A_WIDTH)), full((4, CHUNK, CHUNK)),
                  full((CHUNK, A_WIDTH)), full((1, C_WIDTH)), full((2, 128)), full((512, 512))],
        out_specs=[rows(w) for w, _ in outs],
        out_shape=[jax.ShapeDtypeStruct((n, w), d) for w, d in outs],
        compiler_params=_cparams(("parallel",)),
        name="inproj",
    )(x, p["ln1_g"], p["w_in"], p["a_vnorm_g"], wm, brow, p["c_qnorm_g"], p["c_knorm_g12"], p["bd512"])


def _rwkv_prep_kernel(x_ref, xp_ref, mu_ref, w0_ref, w2a_ref, a0_ref, g2_ref, kkp_ref, ka_ref, rk_ref, bd_ref,
                      tri_ref, p2_ref, y0_ref, p3_ref, z_ref, g_ref, bonus_ref, *, c, t_valid):
    x = x_ref[...]
    xs = x + (xp_ref[...] - x) * mu_ref[...]
    r = xs[:, 0:256]
    k = xs[:, 256:512]
    v = xs[:, 512:768]
    wa = xs[:, 768:896]
    lane = lax.broadcasted_iota(jnp.int32, wa.shape, 1)
    pre = _hdot(jnp.where(lane < 64, jnp.tanh(wa), wa), w2a_ref[...])
    y = -(w0_ref[...] + pre[:, 0:256])
    softplus = jnp.maximum(y, 0.0) + jnp.log1p(jnp.exp(-jnp.abs(y)))
    ew = jnp.exp(-softplus - 0.5)
    a = jax.nn.sigmoid(a0_ref[...] + pre[:, 256:512])
    g_ref[...] = _hdot(jax.nn.sigmoid(xs[:, 896:960]), g2_ref[...])
    bd = bd_ref[...]
    kkr = k * kkp_ref[...]
    kk = kkr / jnp.maximum(jnp.sqrt(_dot2(kkr * kkr, bd)), 1e-12)
    k2 = k * (1.0 + (a - 1.0) * ka_ref[...])
    bonus_ref[...] = _dot2(r * k2 * rk_ref[...], bd) * v
    n_sub = x.shape[0] // c
    if t_valid < c:
        live = lax.broadcasted_iota(jnp.int32, ew.shape, 0) % c < t_valid
        ew = jnp.where(live, ew, 0.0)
        kk = jnp.where(live, kk, 0.0)
        k2 = jnp.where(live, k2, 0.0)
        v = jnp.where(live, v, 0.0)
    cum = jnp.concatenate([jnp.dot(tri_ref[...], -ew[s * c:(s + 1) * c], preferred_element_type=F32,
                                   precision=HIGHEST) for s in range(n_sub)], axis=0)
    gam = jnp.exp(cum)
    ginv = jnp.exp(-cum)
    g_ends = [gam[(s + 1) * c - 1:(s + 1) * c, :] for s in range(n_sub)]
    g_end_rows = jnp.concatenate([jnp.broadcast_to(ge, (c, ge.shape[1])) for ge in g_ends], axis=0)
    alpha_t = -kk * jnp.exp(cum + ew)
    bhat = kk * a * ginv
    khat = k2 * ginv
    rt = r * gam
    kbar = khat * g_end_rows
    bbar = bhat * g_end_rows
    row = lax.broadcasted_iota(jnp.int32, (c, c), 0)
    col = lax.broadcasted_iota(jnp.int32, (c, c), 1)
    eye_c = (row == col).astype(F32)
    r64 = lax.broadcasted_iota(jnp.int32, (HEAD_DIM, HEAD_DIM), 0)
    c64 = lax.broadcasted_iota(jnp.int32, (HEAD_DIM, HEAD_DIM), 1)
    chains = [(s, h) for s in range(n_sub) for h in range(B_HEADS)]
    hs = range(len(chains))
    rws = [slice(s * c, (s + 1) * c) for s, _ in chains]
    sls = [slice(h * HEAD_DIM, (h + 1) * HEAD_DIM) for _, h in chains]
    al = [alpha_t[rws[i], sls[i]] for i in hs]
    bh = [bhat[rws[i], sls[i]] for i in hs]
    kh = [khat[rws[i], sls[i]] for i in hs]
    rh = [rt[rws[i], sls[i]] for i in hs]
    vh = [v[rws[i], sls[i]] for i in hs]
    mb = [jnp.where(row > col, _hdot_nt(al[h], bh[h]), 0.0) for h in hs]
    mk = [jnp.where(row > col, _hdot_nt(al[h], kh[h]), 0.0) for h in hs]
    qk = [jnp.where(row >= col, _hdot_nt(rh[h], kh[h]), 0.0) for h in hs]
    qb = [jnp.where(row >= col, _hdot_nt(rh[h], bh[h]), 0.0) for h in hs]
    mkv = [_hdot(mk[h], vh[h]) for h in hs]
    tinv = [eye_c + mb[h] for h in hs]
    pw = mb
    for _ in range(int(math.log2(c)) - 1):
        pw = [_hdot(pw[h], pw[h]) for h in hs]
        tinv = [tinv[h] + _hdot(tinv[h], pw[h]) for h in hs]
    p1 = [_hdot(tinv[h], al[h]) for h in hs]
    u = [_hdot(tinv[h], mkv[h]) for h in hs]
    for i in hs:
        s = chains[i][0]
        rw, sl = rws[i], sls[i]
        mat_rows = slice(s * HEAD_DIM, (s + 1) * HEAD_DIM)
        p2_ref[rw, sl] = rh[i] + _hdot(qb[i], p1[i])
        y0_ref[rw, sl] = _hdot(qk[i], vh[i]) + _hdot(qb[i], u[i])
        p3_ref[mat_rows, sl] = (jnp.where(r64 == c64, g_ends[s][:, sl], 0.0)
                                + _hdot_tn(bbar[rw, sl], p1[i]))
        z_ref[mat_rows, sl] = _hdot_tn(kbar[rw, sl], vh[i]) + _hdot_tn(bbar[rw, sl], u[i])


def _rwkv_seq_kernel(st0_ref, p2_ref, y0_ref, p3_ref, z_ref, g_ref, bonus_ref, lg_ref, lb_ref,
                     y_ref, st_ref, st_scr):
    @pl.when(pl.program_id(1) == 0)
    def _():
        st_scr[...] = st0_ref[...]

    chains = [(b, slice(h * HEAD_DIM, (h + 1) * HEAD_DIM)) for b in range(st_scr.shape[0]) for h in range(B_HEADS)]
    st = [st_scr[b, :, sl] for b, sl in chains]
    ys = [_hdot(p2_ref[b, :, sl], st[i]) + y0_ref[b, :, sl] for i, (b, sl) in enumerate(chains)]
    new = [_hdot(p3_ref[b, :, sl], st[i]) + z_ref[b, :, sl] for i, (b, sl) in enumerate(chains)]
    for i, (b, sl) in enumerate(chains):
        st_scr[b, :, sl] = new[i]
        y = ys[i]
        mu = jnp.mean(y, axis=-1, keepdims=True)
        var = jnp.mean(jnp.square(y - mu), axis=-1, keepdims=True)
        yn = (y - mu) * lax.rsqrt(var + RWKV_GN_EPS) * lg_ref[:, sl] + lb_ref[:, sl]
        y_ref[b, :, sl] = (yn + bonus_ref[b, :, sl]) * g_ref[b, :, sl]
    st_ref[...] = st_scr[...]


def _rwkv(proj, prev, st0, p, nb, nch, c, t_valid, n_sub=1):
    n = proj.shape[0]
    full = lambda shape: pl.BlockSpec(shape, lambda b, i: (0,) * len(shape))
    nst = nch // n_sub
    rows_p = lambda w: pl.BlockSpec((n_sub * c, w), lambda b, i: (b * nst + i, 0))
    mats_p = pl.BlockSpec((n_sub * HEAD_DIM, 256), lambda b, i: (b * nst + i, 0))
    tri = (jnp.arange(c)[:, None] >= jnp.arange(c)[None, :]).astype(F32)
    p2, y0, p3, z, g, bonus = pl.pallas_call(
        functools.partial(_rwkv_prep_kernel, c=c, t_valid=t_valid),
        grid=(nb, nst),
        in_specs=[rows_p(B_PROJ), rows_p(B_PROJ), full((1, B_PROJ)), full((1, 256)), full((128, 512)),
                  full((1, 256)), full((64, 256)), full((1, 256)), full((1, 256)), full((1, 256)), full((256, 256)),
                  full((c, c))],
        out_specs=[rows_p(256), rows_p(256), mats_p, mats_p, rows_p(256), rows_p(256)],
        out_shape=[jax.ShapeDtypeStruct((n, 256), F32), jax.ShapeDtypeStruct((n, 256), F32),
                   jax.ShapeDtypeStruct((nb * nch * HEAD_DIM, 256), F32),
                   jax.ShapeDtypeStruct((nb * nch * HEAD_DIM, 256), F32),
                   jax.ShapeDtypeStruct((n, 256), F32), jax.ShapeDtypeStruct((n, 256), F32)],
        compiler_params=_cparams(("parallel", "parallel")),
        name="rwkv_prep",
    )(proj, prev, p["b_mu"], p["b_w0"], p["b_w2a"], p["b_a0"], p["b_g2"], p["b_kk"], p["b_ka"], p["b_rk"],
      p["bd256"], tri)
    bb = 2 if nb % 2 == 0 else 1
    t_len = nch * c
    seq3 = lambda a: a.reshape(nb, t_len, 256)
    mat3 = lambda a: a.reshape(nb, nch * HEAD_DIM, 256)
    st_spec = pl.BlockSpec((bb, HEAD_DIM, 256), lambda b, i: (b, 0, 0))
    rows3 = pl.BlockSpec((bb, c, 256), lambda b, i: (b, i, 0))
    mats3 = pl.BlockSpec((bb, HEAD_DIM, 256), lambda b, i: (b, i, 0))
    y, st = pl.pallas_call(
        _rwkv_seq_kernel,
        grid=(nb // bb, nch),
        in_specs=[st_spec, rows3, rows3, mats3, mats3, rows3, rows3, full((1, 256)), full((1, 256))],
        out_specs=[rows3, st_spec],
        out_shape=[jax.ShapeDtypeStruct((nb, t_len, 256), F32), jax.ShapeDtypeStruct((nb, HEAD_DIM, 256), F32)],
        scratch_shapes=[pltpu.VMEM((bb, HEAD_DIM, 256), F32)],
        compiler_params=_cparams(("parallel", "arbitrary")),
        name="rwkv_seq",
    )(st0, seq3(p2), seq3(y0), mat3(p3), mat3(z), seq3(g), seq3(bonus), p["b_lnx_g"], p["b_lnx_b"])
    return y.reshape(n, 256), st


def _compress_tail(a0, a1, n_out, b1_ref, w2_ref, kg_ref, bd_ref, o_ref):
    n = a1.shape[0]
    hid = jax.nn.gelu(a0 + pltpu.roll(a1, n - 1, 0) + b1_ref[...])[0:n_out]
    out = _dot(hid, w2_ref[...])
    k = out[:, 0:128]
    kn = k * lax.rsqrt(_dot2(k * k, bd_ref[...]) * (1.0 / HEAD_DIM) + 1e-6) * kg_ref[...]
    o_ref[:, 0:128] = kn.astype(BF16)
    o_ref[:, 128:256] = out[:, 128:256].astype(BF16)


def _compress_kernel(x_ref, w1_ref, b1_ref, w2_ref, kg_ref, bd_ref, o_ref):
    x = x_ref[...].astype(BF16)
    a0 = jnp.dot(x, w1_ref[0], preferred_element_type=F32)
    a1 = jnp.dot(x, w1_ref[1], preferred_element_type=F32)
    _compress_tail(a0, a1, x.shape[0], b1_ref, w2_ref, kg_ref, bd_ref, o_ref)


def _compress_paged_kernel(pt_ref, *refs, pg):
    pages = refs[:pg + 1]
    w1_ref, b1_ref, w2_ref, kg_ref, bd_ref, perm_ref, o_ref, rows_scr = refs[pg + 1:]
    ppp = PAGE // CMP_STRIDE
    for j, r in enumerate(pages):
        rows = _dot_nt(perm_ref[...], r[...].reshape(4 * HEAD_DIM, PAGE))
        for s in range(CMP_STRIDE):
            rows_scr[s, j * ppp:(j + 1) * ppp, :] = rows[s * ppp:(s + 1) * ppp, :]
    n_piece = (pg + 1) * ppp
    a0 = jnp.zeros((n_piece, 256), F32)
    a1 = jnp.zeros((n_piece, 256), F32)
    for s in range(CMP_STRIDE):
        xs = rows_scr[s].astype(BF16)
        a0 = a0 + jnp.dot(xs, w1_ref[0, s * 256:(s + 1) * 256, :], preferred_element_type=F32)
        a1 = a1 + jnp.dot(xs, w1_ref[1, s * 256:(s + 1) * 256, :], preferred_element_type=F32)
    _compress_tail(a0, a1, pg * 8, b1_ref, w2_ref, kg_ref, bd_ref, o_ref)


def _compress_weight_specs(index):
    full = lambda shape: pl.BlockSpec(shape, index(len(shape)))
    return [full((2, 4096, 256)), full((1, 256)), full((256, 256)), full((1, 128)), full((128, 128))]


def _compress(cmp_rows, p, nb):
    n_piece = cmp_rows.shape[0] // nb // CMP_STRIDE
    x = cmp_rows.reshape(nb, n_piece, 4096)
    return pl.pallas_call(
        _compress_kernel,
        grid=(nb,),
        in_specs=[pl.BlockSpec((None, n_piece, 4096), lambda b: (b, 0, 0))]
        + _compress_weight_specs(lambda r: (lambda b: (0,) * r)),
        out_specs=pl.BlockSpec((None, n_piece, 256), lambda b: (b, 0, 0)),
        out_shape=jax.ShapeDtypeStruct((nb, n_piece, 256), BF16),
        compiler_params=_cparams(("parallel",)),
        name="compress",
    )(x, p["c_w1e"], p["c_b1e"], p["c_w2e"], p["c_kg0"], p["bd128"])


CMP_PAGES = 32


def _compress_paged(cache_t, page_table, p, layer):
    ns, n_pages = page_table.shape
    pg = min(CMP_PAGES, n_pages)
    ppp = PAGE // CMP_STRIDE
    r = jnp.arange(PAGE)
    perm = (r[None, :] == (r[:, None] % ppp) * CMP_STRIDE + r[:, None] // ppp).astype(BF16)

    def page_spec(j):
        return pl.BlockSpec((None, None, 2, C_KV_HEADS, HEAD_DIM, PAGE),
                            lambda s, i, pt: (layer, pt[s, jnp.minimum(i * pg + j, n_pages - 1)], 0, 0, 0, 0))

    return pl.pallas_call(
        functools.partial(_compress_paged_kernel, pg=pg),
        grid_spec=pltpu.PrefetchScalarGridSpec(
            num_scalar_prefetch=1,
            grid=(ns, n_pages // pg),
            in_specs=[page_spec(j) for j in range(pg + 1)]
            + _compress_weight_specs(lambda r: (lambda s, i, pt: (0,) * r))
            + [pl.BlockSpec((PAGE, PAGE), lambda s, i, pt: (0, 0))],
            out_specs=pl.BlockSpec((None, pg * ppp, 256), lambda s, i, pt: (s, i, 0)),
            scratch_shapes=[pltpu.VMEM((CMP_STRIDE, (pg + 1) * ppp, 256), F32)],
        ),
        out_shape=jax.ShapeDtypeStruct((ns, n_pages * ppp, 256), BF16),
        compiler_params=_cparams(("parallel", "arbitrary")),
        name="compress_paged",
    )(page_table, *([cache_t] * (pg + 1)), p["c_w1e"], p["c_b1e"], p["c_w2e"], p["c_kg0"], p["bd128"], perm)


def _alibi_slope(head):
    return 2.0 ** (-8.0 * (head + 1.0) / C_HEADS)


def _topk_mask(x, k):
    nl = x.shape[-1]
    lane = lax.broadcasted_iota(jnp.int32, x.shape, x.ndim - 1).astype(F32)
    sel = jnp.zeros(x.shape, F32)
    for _ in range(k):
        m = jnp.max(x, axis=-1, keepdims=True)
        idx = jnp.min(jnp.where(x == m, lane, float(nl)), axis=-1, keepdims=True)
        hit = lane == idx
        sel = jnp.where(hit, 1.0, sel)
        x = jnp.where(hit, NEG_INF, x)
    return sel


def _masked_softmax(s, valid):
    s = jnp.where(valid, s, NEG_BIG)
    e = jnp.exp(s - jnp.max(s, axis=-1, keepdims=True))
    return jnp.where(valid, e / jnp.sum(e, axis=-1, keepdims=True), 0.0)


def _flash_step(carry, s, ok, v, v_is_t=False):
    m, l, acc = carry
    if ok is not None:
        s = jnp.where(ok, s, NEG_BIG)
    m_new = jnp.maximum(m, jnp.max(s, axis=-1, keepdims=True))
    alpha = jnp.exp(m - m_new)
    pr = jnp.exp(s - m_new)
    if ok is not None:
        pr = jnp.where(ok, pr, 0.0)
    l = alpha * l + jnp.sum(pr, axis=-1, keepdims=True)
    pv = _dot_nt(pr, v) if v_is_t else jnp.dot(pr.astype(BF16), v, preferred_element_type=F32)
    return m_new, l, alpha * acc + pv


def _flash_init(rows):
    return (jnp.full((rows, 1), NEG_BIG, F32), jnp.zeros((rows, 1), F32), jnp.zeros((rows, HEAD_DIM), F32))


def _topk_mask_rows(x, k):
    n = x.shape[0]
    ridx = lax.broadcasted_iota(jnp.int32, x.shape, 0).astype(F32)
    sel = jnp.zeros(x.shape, F32)
    for _ in range(k):
        m = jnp.max(x, axis=0, keepdims=True)
        idx = jnp.min(jnp.where(x == m, ridx, float(n)), axis=0, keepdims=True)
        hit = ridx == idx
        sel = jnp.where(hit, 1.0, sel)
        x = jnp.where(hit, NEG_INF, x)
    return sel


def _nsa_prompt_kernel_t(q_ref, gate_ref, kc_ref, kvb_ref, pool_ref, o_ref, *, n_slc):
    i = pl.program_id(1)
    qb = Q_BLOCK
    nq4 = C_GROUP * qb
    n_cmp = kc_ref.shape[0]
    nbl = pool_ref.shape[0]
    lane4 = lax.broadcasted_iota(jnp.int32, (1, nq4), 1)
    tq_row = i * qb + lane4 % qb
    tok_row = i * qb + lax.broadcasted_iota(jnp.int32, (1, qb), 1)
    gate_t = gate_ref[...].T
    blk_col = lax.broadcasted_iota(jnp.int32, (nbl, 1), 0)
    rel2 = (lax.broadcasted_iota(jnp.int32, (256, nq4), 0) - lane4 % qb).astype(F32)
    exp_row = lax.broadcasted_iota(jnp.int32, (256, nbl), 0) // SLC_BLOCK
    exp_col = lax.broadcasted_iota(jnp.int32, (256, nbl), 1)
    half = i // 2
    for h in range(C_KV_HEADS):
        heads = [C_GROUP * h + g for g in range(C_GROUP)]
        q_t = (jnp.concatenate([q_ref[:, hd * HEAD_DIM:(hd + 1) * HEAD_DIM] for hd in heads], axis=0)
               * (HEAD_DIM ** -0.5)).T.astype(BF16)
        slope_row = jnp.full((1, nq4), _alibi_slope(heads[0]), F32)
        for g in range(1, C_GROUP):
            slope_row = jnp.where(lane4 // qb == g, _alibi_slope(heads[g]), slope_row)
        kcol = slice(h * HEAD_DIM, (h + 1) * HEAD_DIM)
        vcol = slice(128 + h * HEAD_DIM, 128 + (h + 1) * HEAD_DIM)
        c_dist = tq_row - (lax.broadcasted_iota(jnp.int32, (n_cmp, 1), 0) * CMP_STRIDE + (2 * CMP_STRIDE - 1))
        valid = c_dist >= 0
        s_c = jnp.dot(kc_ref[:, kcol], q_t, preferred_element_type=F32) - slope_row * c_dist.astype(F32)
        s_c = jnp.where(valid, s_c, NEG_BIG)
        e_c = jnp.exp(s_c - jnp.max(s_c, axis=0, keepdims=True))
        p_c = jnp.where(valid, e_c / jnp.sum(e_c, axis=0, keepdims=True), 0.0)
        o_c = lax.dot_general(kc_ref[:, vcol], p_c.astype(BF16), (((0,), (0,)), ((), ())),
                              preferred_element_type=F32)
        psum = p_c[:, 0:qb] + p_c[:, qb:2 * qb] + p_c[:, 2 * qb:3 * qb] + p_c[:, 3 * qb:4 * qb]
        ps_hi = psum.astype(BF16)
        ps_lo = (psum - ps_hi.astype(F32)).astype(BF16)
        imp = (jnp.dot(pool_ref[...], ps_hi, preferred_element_type=F32)
               + jnp.dot(pool_ref[...], ps_lo, preferred_element_type=F32))
        imp = jnp.where(blk_col * SLC_BLOCK <= tok_row, imp, NEG_BIG)
        imp = jnp.where((blk_col == 0) | (blk_col == tok_row // SLC_BLOCK), POS_BIG, imp)
        imp = jnp.where(blk_col < n_slc, imp, NEG_INF)
        sel_f = _topk_mask_rows(imp, min(SLC_TOPK, n_slc))
        used_col = jnp.max(sel_f, axis=1, keepdims=True)
        sel_t = sel_f.astype(BF16)
        bias2 = slope_row * rel2

        def tiles_step(carry, ps, kcols, vcols, mask_fn):
            m, l, acc = carry
            rows, scores, shifts = [], [], []
            m_new = m
            for p in ps:
                r0 = pl.multiple_of(jnp.maximum(p, 0) * 256, 256)
                off = ((2 * p - i) * qb).astype(F32)
                s = jnp.dot(kvb_ref[pl.ds(r0, 256), kcols], q_t, preferred_element_type=F32) + bias2
                s = jnp.where(mask_fn(p, off), s, NEG_BIG)
                shift = slope_row * off
                m_new = jnp.maximum(m_new, jnp.max(s, axis=0, keepdims=True) + shift)
                rows.append(r0)
                scores.append(s)
                shifts.append(shift)
            alpha = jnp.exp(m - m_new)
            l = alpha * l
            acc = alpha * acc
            for r0, s, shift in zip(rows, scores, shifts):
                pr = jnp.exp(s - (m_new - shift))
                l = l + jnp.sum(pr, axis=0, keepdims=True)
                acc = acc + lax.dot_general(kvb_ref[pl.ds(r0, 256), vcols], pr.astype(BF16),
                                            (((0,), (0,)), ((), ())), preferred_element_type=F32)
            return m_new, l, acc

        def sel_mask(p, off):
            expand = (exp_col == 4 * p + exp_row).astype(BF16)
            m1 = jnp.dot(expand, sel_t, preferred_element_type=F32) > 0.5
            return jnp.concatenate([m1] * C_GROUP, axis=1)

        def sel_causal_mask(p, off):
            return sel_mask(p, off) & (rel2 + off <= 0.0)

        def win_mask(p, off):
            d = rel2 + jnp.where(p >= 0, off, float(WINDOW))
            return (d <= 0.0) & (d > -float(WINDOW))

        def sel_body(pp, carry):
            used = jnp.max(jnp.where(blk_col // 8 == pp, used_col, 0.0))
            return lax.cond(used > 0.0, lambda cr: tiles_step(cr, (2 * pp, 2 * pp + 1), kcol, vcol, sel_mask),
                            lambda cr: cr, carry)

        init = (jnp.full((1, nq4), 0.5 * NEG_BIG, F32), jnp.zeros((1, nq4), F32), jnp.zeros((HEAD_DIM, nq4), F32))
        carry = lax.fori_loop(0, half // 2, sel_body, init)
        _, l_s, acc_s = lax.cond(
            half % 2 == 1,
            lambda cr: tiles_step(cr, (half - 1, half), kcol, vcol, sel_causal_mask),
            lambda cr: tiles_step(cr, (half,), kcol, vcol, sel_causal_mask), carry)

        wk = slice(256 + h * HEAD_DIM, 256 + (h + 1) * HEAD_DIM)
        wv = slice(384 + h * HEAD_DIM, 384 + (h + 1) * HEAD_DIM)
        n_wp = WINDOW // 256
        _, l_w, acc_w = tiles_step(init, tuple(half - d for d in range(n_wp, -1, -1)), wk, wv, win_mask)

        def gate_row(j):
            return jnp.concatenate([gate_t[3 * hd + j:3 * hd + j + 1, :] for hd in heads], axis=1)

        o_t = gate_row(0) * o_c + gate_row(1) * (acc_s / l_s) + gate_row(2) * (acc_w / l_w)
        o = o_t.T
        for g, hd in enumerate(heads):
            o_ref[hd] = o[g * qb:(g + 1) * qb]


def _nsa_prompt(q, gate, kcvc, kvb, nb, t_len):
    assert t_len % 256 == 0
    nq = t_len // Q_BLOCK
    n_cmp = kcvc.shape[1]
    n_slc = -(-t_len // SLC_BLOCK)
    nbl = -(-n_slc // 128) * 128
    pool = (jnp.arange(nbl)[:, None] == jnp.arange(n_cmp)[None, :] // (SLC_BLOCK // CMP_STRIDE)).astype(BF16)
    return pl.pallas_call(
        functools.partial(_nsa_prompt_kernel_t, n_slc=n_slc),
        grid=(nb, nq),
        in_specs=[pl.BlockSpec((Q_BLOCK, C_WIDTH), lambda b, i: (b * nq + i, 0)),
                  pl.BlockSpec((Q_BLOCK, 128), lambda b, i: (b * nq + i, 0)),
                  pl.BlockSpec((None, n_cmp, 256), lambda b, i: (b, 0, 0)),
                  pl.BlockSpec((t_len, 512), lambda b, i: (b, 0)),
                  pl.BlockSpec((nbl, n_cmp), lambda b, i: (0, 0))],
        out_specs=pl.BlockSpec((None, C_HEADS, Q_BLOCK, HEAD_DIM), lambda b, i: (b, 0, i, 0)),
        out_shape=jax.ShapeDtypeStruct((nb, C_HEADS, t_len, HEAD_DIM), F32),
        compiler_params=_cparams(("parallel", "arbitrary")),
        name="nsa_prompt",
    )(q, gate, kcvc, kvb, pool)


SLC_PAGES = 32


def _nsa_sample_kernel(pt_ref, *refs, pg, dec_seq, past_len, n_buf):
    pages = refs[:pg]
    (q_ref, gate_ref, kc_ref, new_ref, buf_ref, pool_ref, o_ref,
     sel_scr, ocw_scr, m_scr, l_scr, acc_scr) = refs[pg:]
    i = pl.program_id(1)
    n_parts = pl.num_programs(1)
    rows = C_GROUP * dec_seq
    n_cmp = kc_ref.shape[0]
    n_blk = pool_ref.shape[1]
    ridx = lax.broadcasted_iota(jnp.int32, (rows, 1), 0)
    tq = ridx % dec_seq
    t_abs = past_len + tq
    lane128 = lax.broadcasted_iota(jnp.int32, (1, 128), 1)

    def slope_of(h):
        s = jnp.full((rows, 1), _alibi_slope(C_GROUP * h), F32)
        for g in range(1, C_GROUP):
            s = jnp.where(ridx // dec_seq == g, _alibi_slope(C_GROUP * h + g), s)
        return s

    @pl.when(i == 0)
    def _():
        same_tok = (lax.broadcasted_iota(jnp.int32, (rows, rows), 0) % dec_seq
                    == lax.broadcasted_iota(jnp.int32, (rows, rows), 1) % dec_seq).astype(BF16)
        for h in range(C_KV_HEADS):
            slope = slope_of(h)
            qh = (q_ref[h] * (HEAD_DIM ** -0.5)).astype(BF16)
            kcol = slice(h * HEAD_DIM, (h + 1) * HEAD_DIM)
            vcol = slice(128 + h * HEAD_DIM, 128 + (h + 1) * HEAD_DIM)
            c_dist = t_abs - (lax.broadcasted_iota(jnp.int32, (1, n_cmp), 1) * CMP_STRIDE + (2 * CMP_STRIDE - 1))
            s_c = _dot_nt(qh, kc_ref[:, kcol]) - slope * c_dist.astype(F32)
            p_c = _masked_softmax(s_c, c_dist >= 0)
            o_c = jnp.dot(p_c.astype(BF16), kc_ref[:, vcol], preferred_element_type=F32)
            hi = p_c.astype(BF16)
            lo = (p_c - hi.astype(F32)).astype(BF16)
            psum = (jnp.dot(same_tok, hi, preferred_element_type=F32)
                    + jnp.dot(same_tok, lo, preferred_element_type=F32))
            imp = _dot2(psum, pool_ref[...])
            blk = lax.broadcasted_iota(jnp.int32, (1, n_blk), 1)
            imp = jnp.where(blk == 0, POS_BIG, imp)
            sel_scr[h] = _topk_mask(imp, min(SLC_TOPK, n_blk + 1) - 1)
            wk = slice(256 + h * HEAD_DIM, 256 + (h + 1) * HEAD_DIM)
            wv = slice(384 + h * HEAD_DIM, 384 + (h + 1) * HEAD_DIM)
            d_buf = (n_buf + tq) - lax.broadcasted_iota(jnp.int32, (1, n_buf), 1)
            s_b = _dot(qh, buf_ref[0, h]) - slope * d_buf.astype(F32)
            carry = _flash_step(_flash_init(rows), s_b, (d_buf >= 0) & (d_buf < WINDOW), buf_ref[1, h], True)
            d_new = tq - lane128
            s_n = _dot_nt(qh, new_ref[:, wk]) - slope * d_new.astype(F32)
            _, l_w, acc_w = _flash_step(carry, s_n, (d_new >= 0) & (lane128 < dec_seq),
                                        new_ref[:, wv].astype(BF16))
            gt = gate_ref[h]
            ocw_scr[h] = gt[:, 0:1] * o_c + gt[:, 2:3] * (acc_w / l_w)
            m0, l0, a0 = _flash_init(rows)
            m_scr[h] = m0
            l_scr[h] = l0
            acc_scr[h] = a0

    nk = pg * PAGE
    key_pos = i * nk + lax.broadcasted_iota(jnp.int32, (1, nk), 1)
    expand = (lax.broadcasted_iota(jnp.int32, (n_blk, nk), 0)
              == i * (nk // SLC_BLOCK) + lax.broadcasted_iota(jnp.int32, (n_blk, nk), 1) // SLC_BLOCK).astype(BF16)
    hs = range(C_KV_HEADS)
    dist = t_abs - key_pos
    dist_f = dist.astype(F32)
    qh = [(q_ref[h] * (HEAD_DIM ** -0.5)).astype(BF16) for h in hs]
    k_all = [jnp.concatenate([r[0, h] for r in pages], axis=1).astype(BF16) for h in hs]
    v_all = [jnp.concatenate([r[1, h] for r in pages], axis=1).astype(BF16) for h in hs]
    selx = [jnp.dot(sel_scr[h].astype(BF16), expand, preferred_element_type=F32) for h in hs]
    ok = [(dist >= 0) & (selx[h] > 0.5) for h in hs]
    s = [jnp.where(ok[h], jnp.dot(qh[h], k_all[h], preferred_element_type=F32) - slope_of(h) * dist_f, NEG_BIG)
         for h in hs]
    m_old = [m_scr[h] for h in hs]
    m_new = [jnp.maximum(m_old[h], jnp.max(s[h], axis=-1, keepdims=True)) for h in hs]
    alpha = [jnp.exp(m_old[h] - m_new[h]) for h in hs]
    pr = [jnp.where(ok[h], jnp.exp(s[h] - m_new[h]), 0.0) for h in hs]
    pv = [_dot_nt(pr[h], v_all[h]) for h in hs]
    for h in hs:
        m_scr[h] = m_new[h]
        l_scr[h] = alpha[h] * l_scr[h] + jnp.sum(pr[h], axis=-1, keepdims=True)
        acc_scr[h] = alpha[h] * acc_scr[h] + pv[h]

    @pl.when(i == n_parts - 1)
    def _():
        for h in range(C_KV_HEADS):
            slope = slope_of(h)
            qh = (q_ref[h] * (HEAD_DIM ** -0.5)).astype(BF16)
            kcol = slice(h * HEAD_DIM, (h + 1) * HEAD_DIM)
            vcol = slice(128 + h * HEAD_DIM, 128 + (h + 1) * HEAD_DIM)
            d_new = tq - lane128
            s_n = _dot_nt(qh, new_ref[:, kcol]) - slope * d_new.astype(F32)
            _, l_s, acc_s = _flash_step((m_scr[h], l_scr[h], acc_scr[h]), s_n,
                                        (d_new >= 0) & (lane128 < dec_seq), new_ref[:, vcol].astype(BF16))
            o_ref[h] = ocw_scr[h] + gate_ref[h][:, 1:2] * (acc_s / l_s)


def _nsa_sample(page_table, q16, gate16, kcvc, cache_slc_t, new_rows, win_buf_t, layer, dec_seq):
    ns, n_pages = page_table.shape
    pg = min(SLC_PAGES, n_pages)
    past_len = n_pages * PAGE
    n_buf = win_buf_t.shape[-1]
    n_cmp = kcvc.shape[1]
    n_blk = past_len // SLC_BLOCK
    rows = C_GROUP * dec_seq
    pool = (jnp.arange(n_cmp)[:, None] // (SLC_BLOCK // CMP_STRIDE) == jnp.arange(n_blk)[None, :]).astype(BF16)
    per_seq = lambda shape: pl.BlockSpec((None,) + shape, lambda s, i, pt: (s,) + (0,) * len(shape))

    kv_tile = (2, C_KV_HEADS, HEAD_DIM)

    def page_spec(j):
        return pl.BlockSpec((None, None) + kv_tile + (PAGE,),
                            lambda s, i, pt: (layer, pt[s, i * pg + j], 0, 0, 0, 0))

    return pl.pallas_call(
        functools.partial(_nsa_sample_kernel, pg=pg, dec_seq=dec_seq, past_len=past_len, n_buf=n_buf),
        grid_spec=pltpu.PrefetchScalarGridSpec(
            num_scalar_prefetch=1,
            grid=(ns, n_pages // pg),
            in_specs=[page_spec(j) for j in range(pg)]
            + [per_seq((C_KV_HEADS, rows, HEAD_DIM)), per_seq((C_KV_HEADS, rows, 3)), per_seq((n_cmp, 256)),
               per_seq((128, 512)),
               pl.BlockSpec((None, None) + kv_tile + (n_buf,), lambda s, i, pt: (layer, s, 0, 0, 0, 0)),
               pl.BlockSpec((n_cmp, n_blk), lambda s, i, pt: (0, 0))],
            out_specs=per_seq((C_KV_HEADS, rows, HEAD_DIM)),
            scratch_shapes=[pltpu.VMEM((C_KV_HEADS, rows, n_blk), F32), pltpu.VMEM((C_KV_HEADS, rows, HEAD_DIM), F32),
                            pltpu.VMEM((C_KV_HEADS, rows, 1), F32), pltpu.VMEM((C_KV_HEADS, rows, 1), F32),
                            pltpu.VMEM((C_KV_HEADS, rows, HEAD_DIM), F32)],
        ),
        out_shape=jax.ShapeDtypeStruct((ns, C_KV_HEADS, rows, HEAD_DIM), F32),
        compiler_params=_cparams(("parallel", "arbitrary")),
        name="nsa_sample",
    )(page_table, *([cache_slc_t] * pg), q16, gate16, kcvc, new_rows, win_buf_t, pool)


def _outproj_kernel(x_ref, ya_ref, yb_ref, yc_ref, w_ref, o_ref):
    acc = x_ref[...] + _dot(ya_ref[...], w_ref[0:256, :]) + _dot(yb_ref[...], w_ref[256:512, :])
    for hd in range(C_HEADS):
        r0 = 512 + hd * HEAD_DIM
        acc = acc + _dot(yc_ref[hd], w_ref[r0:r0 + HEAD_DIM, :])
    o_ref[...] = acc


def _outproj(x, ya, yb, yc, w_out, nb, t_len, tm):
    nt = t_len // tm
    rows = lambda w: pl.BlockSpec((tm, w), lambda b, i: (b * nt + i, 0))
    return pl.pallas_call(
        _outproj_kernel,
        grid=(nb, nt),
        in_specs=[rows(1024), rows(256), rows(256),
                  pl.BlockSpec((None, C_HEADS, tm, HEAD_DIM), lambda b, i: (b, 0, i, 0)),
                  pl.BlockSpec((1024, 1024), lambda b, i: (0, 0))],
        out_specs=rows(1024),
        out_shape=jax.ShapeDtypeStruct(x.shape, F32),
        compiler_params=_cparams(("parallel", "parallel")),
        name="outproj",
    )(x, ya, yb, yc, w_out)


def _top_vals_rows(x, k, rows_out):
    cols = x.shape[1]
    orow = lax.broadcasted_iota(jnp.int32, (rows_out, cols), 0)
    acc = jnp.full((rows_out, cols), NEG_INF, F32)
    for j in range(k):
        m = jnp.max(x, axis=0, keepdims=True)
        x = jnp.where(x == m, NEG_INF, x)
        acc = jnp.where(orow == j, m, acc)
    return acc


def _peer_route_kernel(x_ref, g_ref, wq_ref, qg_ref, sk_ref, h2_ref, e1_ref, e2_ref, th_ref):
    x = x_ref[...]
    hb = (x * lax.rsqrt(jnp.mean(x * x, axis=-1, keepdims=True) + 1e-6) * g_ref[...]).astype(BF16)
    h2_ref[...] = hb
    qp = jnp.dot(hb, wq_ref[...], preferred_element_type=F32)
    k = PEER_TOPK
    for hd in range(PEER_HEADS):
        qn = []
        for c in range(2):
            qc = qp[:, (2 * hd + c) * N_KEYS:(2 * hd + c + 1) * N_KEYS]
            qn.append((qc * lax.rsqrt(jnp.mean(qc * qc, axis=-1, keepdims=True) + 1e-6)
                       * qg_ref[:, c * 128:(c + 1) * 128]).astype(BF16))
        for ts in range(x.shape[0] // 128):
            tsl = slice(ts * 128, (ts + 1) * 128)
            st = [_dot_nt(sk_ref[hd, c], qn[c][tsl]) for c in range(2)]
            top = [_top_vals_rows(s, k + 1, 24) for s in st]
            cand = jnp.concatenate([top[0][0:1, :] + top[1]]
                                   + [top[0][a:a + 1, :] + top[1][0:8, :] for a in range(1, k + 1)], axis=0)
            best = _top_vals_rows(cand, k + 1, 24)
            tau = 0.5 * (best[k - 1:k, :] + best[k:k + 1, :])
            z = jnp.sum(jnp.where(cand >= tau, jnp.exp(cand - best[0:1, :]), 0.0), axis=0, keepdims=True)
            m2 = top[1][0:1, :]
            e1_ref[hd, :, tsl] = jnp.exp(st[0] - top[0][0:1, :]) / z
            e2_ref[hd, :, tsl] = jnp.exp(st[1] - m2)
            th_ref[hd, :, tsl] = jnp.exp((tau - m2) - st[0])


def _peer_dense_kernel(x_ref, h2_ref, u_ref, vt_ref, e1_ref, e2_ref, th_ref, o_ref, acc_ref, act_ref, w_ref, *, ea):
    j = pl.program_id(1)
    tm = h2_ref.shape[0]

    @pl.when(j == 0)
    def _():
        acc_ref[...] = jnp.zeros(acc_ref.shape, F32)

    h2 = h2_ref[...]
    hb = N_KEYS // 2
    ga = PEER_GA
    n_grp = ea // ga
    n_ts = tm // 128

    def experts(grp):
        gsl = slice(grp * ga * N_KEYS, (grp + 1) * ga * N_KEYS)
        act_ref[gsl, :] = jax.nn.gelu(_dot_nt(u_ref[gsl, :], h2))

    experts(0)
    for grp in range(n_grp):
        firsts = [grp * ga + k for k in range(ga)]
        th_rows = [[th_ref[hd, pl.ds(j * ea + aa, 1), :] for aa in firsts] for hd in range(PEER_HEADS)]
        e1_rows = [[e1_ref[hd, pl.ds(j * ea + aa, 1), :] for aa in firsts] for hd in range(PEER_HEADS)]
        for ts in range(n_ts):
            if ts == n_ts // 2 and grp + 1 < n_grp:
                experts(grp + 1)
            tsl = slice(ts * 128, (ts + 1) * 128)
            for bh in range(2):
                bsl = slice(bh * hb, (bh + 1) * hb)
                gates = [jnp.zeros((hb, 128), F32) for _ in range(ga)]
                for hd in range(PEER_HEADS):
                    e2 = e2_ref[hd, bsl, tsl]
                    for k in range(ga):
                        hit = e2 >= th_rows[hd][k][:, tsl]
                        gates[k] = gates[k] + jnp.where(hit, e2, 0.0) * e1_rows[hd][k][:, tsl]
                for k, aa in enumerate(firsts):
                    rsl = slice(aa * N_KEYS + bh * hb, aa * N_KEYS + (bh + 1) * hb)
                    w_ref[rsl, tsl] = (act_ref[rsl, tsl] * gates[k]).astype(BF16)
        gsl = slice(grp * ga * N_KEYS, (grp + 1) * ga * N_KEYS)
        acc_ref[...] += jnp.dot(vt_ref[:, gsl], w_ref[gsl, :], preferred_element_type=F32)

    @pl.when(j == pl.num_programs(1) - 1)
    def _():
        o_ref[...] = x_ref[...] + acc_ref[...].T


PEER_EA = 16
PEER_GA = 4


def _peer(x1, p, tm):
    h2, e1, e2, th = _peer_route(x1, p, tm)
    return _peer_dense(x1, h2, e1, e2, th, p, tm)


def _peer_route(x1, p, tm):
    n = x1.shape[0]
    nt = n // tm
    full = lambda shape: pl.BlockSpec(shape, lambda i: (0,) * len(shape))
    tk = pl.BlockSpec((PEER_HEADS, N_KEYS, tm), lambda i: (0, 0, i))
    sd = jax.ShapeDtypeStruct((PEER_HEADS, N_KEYS, n), F32)
    return pl.pallas_call(
        _peer_route_kernel,
        grid=(nt,),
        in_specs=[pl.BlockSpec((tm, 1024), lambda i: (i, 0)), full((1, 1024)), full((1024, 2048)), full((1, 256)),
                  full((PEER_HEADS, 2, N_KEYS, N_KEYS))],
        out_specs=[pl.BlockSpec((tm, 1024), lambda i: (i, 0)), tk, tk, tk],
        out_shape=[jax.ShapeDtypeStruct((n, 1024), BF16), sd, sd, sd],
        compiler_params=_cparams(("parallel",)),
        name="peer_route",
    )(x1, p["ln2_g"], p["p_wq"], p["p_qnorm_g"], p["p_subkeys"])


def _peer_dense(x1, h2, e1, e2, th, p, tm):
    n = x1.shape[0]
    nt = n // tm
    et = PEER_EA * N_KEYS
    n_exp = p["p_u"].shape[0]
    tk2 = pl.BlockSpec((PEER_HEADS, N_KEYS, tm), lambda i, j: (0, 0, i))
    return pl.pallas_call(
        functools.partial(_peer_dense_kernel, ea=PEER_EA),
        grid=(nt, n_exp // et),
        in_specs=[pl.BlockSpec((tm, 1024), lambda i, j: (i, 0)), pl.BlockSpec((tm, 1024), lambda i, j: (i, 0)),
                  pl.BlockSpec((et, 1024), lambda i, j: (j, 0)), pl.BlockSpec((1024, et), lambda i, j: (0, j)),
                  tk2, tk2, tk2],
        out_specs=pl.BlockSpec((tm, 1024), lambda i, j: (i, 0)),
        out_shape=jax.ShapeDtypeStruct((n, 1024), F32),
        scratch_shapes=[pltpu.VMEM((1024, tm), F32), pltpu.VMEM((et, tm), F32), pltpu.VMEM((et, tm), BF16)],
        compiler_params=_cparams(("parallel", "arbitrary")),
        name="peer_dense",
    )(x1, h2, p["p_u"], p["p_vt"], e1, e2, th)


def _prep_layer(l, ln1_g, ln2_g, w_in, w_out, a_vnorm_g, a_ws, a_bs, b_mu, b_w0, b_w2, b_a0, b_a2, b_g2, b_kk,
                b_ka, b_rk, b_lnx_g, b_lnx_b, c_qnorm_g, c_knorm_g, c_cmp_w1, c_cmp_b1, c_cmp_w2, p_wq,
                p_qnorm_g, p_subkeys, p_u, p_v, dec_seq):
    p = {}
    p["ln1_g"] = ln1_g[l][None, :]
    p["ln2_g"] = ln2_g[l][None, :]
    w = w_in[l]
    z = lambda c: jnp.zeros((w.shape[0], c), w.dtype)
    p["w_in"] = jnp.concatenate([w[:, :1472], z(64), w[:, 1472:], z(104)], axis=1).astype(BF16)
    p["a_vnorm_g"] = a_vnorm_g[l][None, :]
    wm = jnp.where(jnp.tril(jnp.ones((CHUNK, CHUNK), bool)), a_ws[l], 0.0)
    p["wm_prompt"] = wm.astype(BF16)
    p["brow_prompt"] = jnp.repeat(a_bs[l].T, HEAD_DIM, axis=1)
    reps = CHUNK // dec_seq
    eye = jnp.eye(reps, dtype=F32)
    p["wm_sample"] = jnp.stack([jnp.kron(eye, wm[g, :dec_seq, :dec_seq]) for g in range(4)]).astype(BF16)
    p["brow_sample"] = jnp.tile(p["brow_prompt"][:dec_seq], (reps, 1))
    p["c_qnorm_g"] = jnp.tile(c_qnorm_g[l], C_HEADS)[None, :]
    p["c_knorm_g12"] = jnp.stack([jnp.tile(c_knorm_g[l, 1], 2), jnp.tile(c_knorm_g[l, 2], 2)])
    p["bd512"] = _block_ones(512, HEAD_DIM)
    p["bd256"] = _block_ones(256, HEAD_DIM)
    p["bd128"] = _block_ones(128, HEAD_DIM)
    eye2 = jnp.eye(2, dtype=F32)
    w1h = c_cmp_w1[l].reshape(2, 2, CMP_STRIDE, HEAD_DIM, HEAD_DIM)
    p["c_w1e"] = jnp.einsum("kjsdc,kK,hH->jsKHdkhc", w1h, eye2, eye2).reshape(2, 4096, 256).astype(BF16)
    p["c_b1e"] = jnp.broadcast_to(c_cmp_b1[l][:, None, :], (2, 2, HEAD_DIM)).reshape(1, 256)
    p["c_w2e"] = jnp.einsum("kcd,kK,hH->khcKHd", c_cmp_w2[l], eye2, eye2).reshape(256, 256).astype(BF16)
    p["c_kg0"] = jnp.tile(c_knorm_g[l, 0], 2)[None, :]
    row = lambda a: a.reshape(1, -1)
    p["b_mu"] = row(b_mu[l])
    p["b_w0"] = row(b_w0[l])
    p["b_a0"] = row(b_a0[l])
    zz = jnp.zeros((64, 256), F32)
    p["b_w2a"] = jnp.concatenate([jnp.concatenate([b_w2[l], zz], axis=1),
                                  jnp.concatenate([zz, b_a2[l]], axis=1)], axis=0)
    p["b_g2"] = b_g2[l]
    p["b_kk"] = row(b_kk[l])
    p["b_ka"] = row(b_ka[l])
    p["b_rk"] = row(b_rk[l])
    p["b_lnx_g"] = row(b_lnx_g[l])
    p["b_lnx_b"] = row(b_lnx_b[l])
    p["w_out"] = w_out[l].astype(BF16)
    p["p_wq"] = p_wq[l].astype(BF16)
    p["p_qnorm_g"] = row(p_qnorm_g[l])
    p["p_subkeys"] = p_subkeys[l].astype(BF16)
    p["p_u"] = p_u[l].astype(BF16)
    p["p_vt"] = p_v[l].T.astype(BF16)
    return p


def _st_in(wkv):
    b = wkv.shape[0]
    return wkv.transpose(0, 3, 1, 2).reshape(b, HEAD_DIM, 256)


def _st_out(st):
    b = st.shape[0]
    return st.reshape(b, HEAD_DIM, B_HEADS, HEAD_DIM).transpose(0, 2, 3, 1)


def kernel(x_prompt, x_sample, cache_cmp_kv, cache_slc_kv, cache_win_kv, state_wkv, state_shift, page_table, ln1_g, ln2_g, w_in, w_out, a_vnorm_g, a_ws, a_bs, b_mu, b_w0, b_w2, b_a0, b_a2, b_g2, b_kk, b_ka, b_rk, b_lnx_g, b_lnx_b, c_qnorm_g, c_knorm_g, c_cmp_w1, c_cmp_b1, c_cmp_w2, p_wq, p_qnorm_g, p_subkeys, p_u, p_v):
    weights = (ln1_g, ln2_g, w_in, w_out, a_vnorm_g, a_ws, a_bs, b_mu, b_w0, b_w2, b_a0, b_a2, b_g2, b_kk, b_ka,
               b_rk, b_lnx_g, b_lnx_b, c_qnorm_g, c_knorm_g, c_cmp_w1, c_cmp_b1, c_cmp_w2, p_wq, p_qnorm_g,
               p_subkeys, p_u, p_v)
    nb, t_len, d_model = x_prompt.shape
    ns, dec_seq, _ = x_sample.shape
    depth = ln1_g.shape[0]
    n_pool = cache_cmp_kv.shape[1]
    n_s = ns * dec_seq
    n_sp = -(-n_s // 128) * 128
    rwkv_c = 64
    dec_c = 8
    pad_rows = lambda a, n: jnp.pad(a, ((0, n - a.shape[0]),) + ((0, 0),) * (a.ndim - 1))
    xp = x_prompt.reshape(nb * t_len, d_model)
    xs = pad_rows(x_sample.reshape(n_s, d_model), n_sp)
    kv5 = lambda a, b, t: a.reshape(b, t, 2, C_KV_HEADS, HEAD_DIM)
    cache_cmp_t = cache_cmp_kv.transpose(0, 1, 3, 4, 5, 2)
    cache_slc_t = cache_slc_kv.transpose(0, 1, 3, 4, 5, 2)
    cache_win_t = cache_win_kv.transpose(0, 1, 3, 4, 5, 2)
    outs_p, outs_s = [], []
    for l in range(depth):
        p = _prep_layer(l, *weights, dec_seq=dec_seq)
        ya, _, bproj, q, cmp, slc, win, gate, kvb = _inproj(xp, p, p["wm_prompt"], p["brow_prompt"], 256)
        bp3 = bproj.reshape(nb, t_len, B_PROJ)
        prev = jnp.concatenate([jnp.zeros((nb, 1, B_PROJ), F32), bp3[:, :-1]], axis=1).reshape(nb * t_len, B_PROJ)
        yb, st = _rwkv(bproj, prev, jnp.zeros((nb, HEAD_DIM, 256), F32), p, nb, t_len // rwkv_c, rwkv_c, rwkv_c,
                       n_sub=4)
        kcvc = _compress(cmp, p, nb)
        yc = _nsa_prompt(q, gate, kcvc, kvb, nb, t_len)
        x1 = _outproj(xp, ya, yb, yc, p["w_out"], nb, t_len, 512)
        xp = _peer(x1, p, 512)
        n_win = min(WINDOW, t_len)
        outs_p.append((kv5(cmp, nb, t_len), kv5(slc, nb, t_len), kv5(win, nb, t_len)[:, t_len - n_win:],
                       _st_out(st), bp3[:, -1]))
        sya, sv, sbproj, sq, scmp, sslc, swin, sgate, _ = _inproj(xs, p, p["wm_sample"], p["brow_sample"], 128)
        sb3 = sbproj[:n_s].reshape(ns, dec_seq, B_PROJ)
        sprev = jnp.concatenate([state_shift[l][:, None], sb3[:, :-1]], axis=1)
        pad_c = lambda a: jnp.pad(a, ((0, 0), (0, dec_c - dec_seq), (0, 0))).reshape(ns * dec_c, B_PROJ)
        syb, sst = _rwkv(pad_c(sb3), pad_c(sprev), _st_in(state_wkv[l]), p, ns, 1, dec_c, dec_seq)
        syb = pad_rows(syb.reshape(ns, dec_c, 256)[:, :dec_seq].reshape(n_s, 256), n_sp)
        skc = _compress_paged(cache_cmp_t, page_table, p, l)
        q16 = (sq[:n_s].reshape(ns, dec_seq, C_KV_HEADS, C_GROUP, HEAD_DIM).transpose(0, 2, 3, 1, 4)
               .reshape(ns, C_KV_HEADS, C_GROUP * dec_seq, HEAD_DIM))
        g16 = (sgate[:n_s, :3 * C_HEADS].reshape(ns, dec_seq, C_KV_HEADS, C_GROUP, 3).transpose(0, 2, 3, 1, 4)
               .reshape(ns, C_KV_HEADS, C_GROUP * dec_seq, 3))
        new_rows = jnp.concatenate([sslc[:n_s], swin[:n_s]], axis=1).reshape(ns, dec_seq, 512)
        new_rows = jnp.pad(new_rows, ((0, 0), (0, 128 - dec_seq), (0, 0)))
        win_buf = cache_win_kv[l].reshape(ns, -1, 256)
        so = _nsa_sample(page_table, q16, g16, skc, cache_slc_t, new_rows, cache_win_t, l, dec_seq)
        syc = (so.reshape(ns, C_KV_HEADS, C_GROUP, dec_seq, HEAD_DIM).transpose(1, 2, 0, 3, 4)
               .reshape(C_HEADS, n_s, HEAD_DIM))
        syc = jnp.pad(syc, ((0, 0), (0, n_sp - n_s), (0, 0)))[None]
        sx1 = _outproj(xs, sya, syb, syc, p["w_out"], 1, n_sp, 128)
        xs = _peer(sx1, p, 128)
        swin3 = swin[:n_s].reshape(ns, dec_seq, 256)
        win_new = jnp.concatenate([win_buf, swin3], axis=1)[:, dec_seq:]
        outs_s.append((kv5(scmp[:n_s], ns, dec_seq), kv5(sslc[:n_s], ns, dec_seq),
                       kv5(win_new, ns, win_buf.shape[1]), _st_out(sst), sb3[:, -1],
                       sv[:n_s].reshape(ns, dec_seq, A_WIDTH)))
    stk = lambda lst, i: jnp.stack([s[i] for s in lst], axis=0)
    return (xp.reshape(nb, t_len, d_model), xs[:n_s].reshape(ns, dec_seq, d_model),
            stk(outs_p, 0), stk(outs_p, 1), stk(outs_p, 2), stk(outs_p, 3), stk(outs_p, 4),
            stk(outs_s, 0), stk(outs_s, 1), stk(outs_s, 2), stk(outs_s, 3), stk(outs_s, 4), stk(outs_s, 5))
```
